```python
import math
import jax, jax.numpy as jnp
from jax import lax
import numpy as np

D_MODEL = 1024
BATCH = 4
SEQ = 4096
DEPTH = 2
DEC_BATCH = 32
DEC_SEQ = 8
PAST_LEN = 16384
PAGE_SIZE = 128

HEAD_DIM = 64
MOBA_HEADS = 4
MOBA_BLOCK = 256
MOBA_TOPK = 3
NSA_HEADS = 4
NSA_CMP_LEN = 32
NSA_CMP_STRIDE = 16
NSA_SEL_BLOCK = 64
NSA_TOPN = 16
NSA_WINDOW = 512
NSA_N_KV = 6
SSD_HEADS = 8
SSD_HEAD_DIM = 64
SSD_INNER = SSD_HEADS * SSD_HEAD_DIM
SSD_GROUPS = 2
SSD_STATE = 128
SSD_CONV = 4
SSD_CHUNK = 256
SSD_CONV_DIM = SSD_INNER + 2 * SSD_GROUPS * SSD_STATE
N_BUCKETS = 32
MAX_DISTANCE = 128
N_ATTN_HEADS = MOBA_HEADS + NSA_HEADS
N_EXPERTS = 64
TOP_K = 8
N_ROUTE_GROUPS = 8
TOPK_GROUPS = 4
D_EXPERT = 256
D_SHARED = 256
ROUTE_SCALE = 2.5
MOE_BLOCK = 128
Q_BLOCK = 128
EPS = 1e-6
MOBA_W = MOBA_HEADS * HEAD_DIM
NSA_W = NSA_HEADS * HEAD_DIM
IN_SIZES = (MOBA_W, MOBA_W, MOBA_W, NSA_W, NSA_N_KV * HEAD_DIM, NSA_HEADS * 3, SSD_INNER, SSD_CONV_DIM, SSD_HEADS, 3 * D_MODEL)
IN_WIDTH = sum(IN_SIZES)

kernel_name = 'hybrid_moba_nsa_ssd_moe_step'


def round_up(n, m):
    return -(-n // m) * m


def rmsnorm(x, w):
    x32 = x.astype(jnp.float32)
    y = x32 * lax.rsqrt(jnp.mean(x32 * x32, axis=-1, keepdims=True) + EPS)
    return (y * w.astype(jnp.float32)).astype(x.dtype)


def masked_softmax(logits, mask):
    s = jnp.where(mask, logits.astype(jnp.float32), -jnp.inf)
    m = jnp.max(s, axis=-1, keepdims=True)
    m = jnp.where(jnp.isfinite(m), m, 0.0)
    e = jnp.exp(s - m)
    d = jnp.sum(e, axis=-1, keepdims=True)
    return e / jnp.where(d > 0, d, 1.0)


def rel_bucket(dist):
    n = jnp.maximum(dist, 0)
    exact = N_BUCKETS // 2
    nf = jnp.maximum(n, 1).astype(jnp.float32)
    large = exact + (jnp.log(nf / exact) / math.log(MAX_DISTANCE / exact) * (N_BUCKETS - exact)).astype(jnp.int32)
    return jnp.where(n < exact, n, jnp.minimum(large, N_BUCKETS - 1))


def modulation(c, w, b):
    mod = jax.nn.silu(c) @ w + b
    return [m[:, None, :] for m in jnp.split(mod, 6, axis=-1)]


def mixer_inputs(h, w_in):
    b, l, _ = h.shape
    splits = [int(v) for v in np.cumsum(IN_SIZES)[:-1]]
    q_m, k_m, v_m, q_n, kv_n, g_n, z, xbc, dt_raw, g_br = jnp.split(h @ w_in, splits, axis=-1)
    kv_m = jnp.stack([k_m, v_m], axis=2).reshape(b, l, 2, MOBA_HEADS, HEAD_DIM)
    return (q_m.reshape(b, l, MOBA_HEADS, HEAD_DIM), kv_m,
            q_n.reshape(b, l, NSA_HEADS, HEAD_DIM), kv_n.reshape(b, l, NSA_N_KV, HEAD_DIM),
            jax.nn.sigmoid(g_n).reshape(b, l, NSA_HEADS, 3), z, xbc, dt_raw, jax.nn.sigmoid(g_br))


def compress_tokens(kv, w1, b1, w2, b2, pos):
    b, lf = kv.shape[:2]
    half = NSA_CMP_STRIDE * HEAD_DIM
    nh = lf // NSA_CMP_STRIDE
    halves = kv.reshape(b, nh, NSA_CMP_STRIDE, 2, HEAD_DIM).transpose(0, 1, 3, 2, 4).reshape(b, nh, 2, half)
    pa = jnp.einsum('bnkf,kfe->bnke', halves, w1[:, :half])
    pb = jnp.einsum('bnkf,kfe->bnke', halves, w1[:, half:])
    pos_term = jnp.einsum('kf,kfe->ke', pos.reshape(2, NSA_CMP_LEN * HEAD_DIM), w1) + b1
    hid = jax.nn.gelu(pa[:, :-1] + pb[:, 1:] + pos_term)
    return jnp.einsum('bnke,keo->bnko', hid, w2) + b2


def attn_context(moba_full, nsa_full, cmp_params):
    b, lf = moba_full.shape[:2]
    nbm = lf // MOBA_BLOCK
    blk = moba_full.reshape(b, nbm, MOBA_BLOCK, 2, MOBA_HEADS, HEAD_DIM).transpose(3, 0, 4, 1, 2, 5)
    kb, vb = blk[0], blk[1]
    kmean = jnp.mean(kb.astype(jnp.float32), axis=3)
    cmp = compress_tokens(nsa_full[:, :, 0:2], *cmp_params)
    nbs = lf // NSA_SEL_BLOCK
    sel = nsa_full[:, :, 2:4].reshape(b, nbs, NSA_SEL_BLOCK, 2, HEAD_DIM)
    return kb, vb, kmean, cmp, sel[:, :, :, 0], sel[:, :, :, 1]


def attn_chunk(q_m, q_n, g_n, q0, ctx, band, rel_bias):
    kb, vb, kmean, cmp, sel_k, sel_v = ctx
    b, q = q_m.shape[:2]
    scale = HEAD_DIM ** -0.5
    qpos = q0 + jnp.arange(q)
    t_m = rel_bias[:, :MOBA_HEADS].T
    t_n = rel_bias[:, MOBA_HEADS:].T
    nbm = kb.shape[2]
    ob = q0 // MOBA_BLOCK
    gate = jnp.einsum('bqhd,bhnd->bhqn', q_m.astype(jnp.float32), kmean)
    gate = jnp.where(jnp.arange(nbm) < ob, gate, -jnp.inf)
    kk = min(MOBA_TOPK, nbm)
    gval, gidx = lax.top_k(gate, kk)
    bi = jnp.arange(b)[:, None, None, None]
    hi = jnp.arange(MOBA_HEADS)[None, :, None, None]
    kg = kb[bi, hi, gidx]
    vg = vb[bi, hi, gidx]
    r = jnp.arange(MOBA_BLOCK)
    kpos = gidx[..., None] * MOBA_BLOCK + r
    s_sel = jnp.einsum('bqhd,bhqjkd->bhqjk', q_m, kg) * scale + t_m[hi[..., None], rel_bucket(qpos[:, None, None] - kpos)]
    m_sel = jnp.broadcast_to((gval > -jnp.inf)[..., None], s_sel.shape)
    k_own = lax.dynamic_index_in_dim(kb, ob, axis=2, keepdims=False)
    v_own = lax.dynamic_index_in_dim(vb, ob, axis=2, keepdims=False)
    d_own = qpos[:, None] - (ob * MOBA_BLOCK + r)[None, :]
    s_own = jnp.einsum('bqhd,bhkd->bhqk', q_m, k_own) * scale + t_m[:, rel_bucket(d_own)]
    m_own = jnp.broadcast_to(d_own >= 0, s_own.shape)
    nsel = kk * MOBA_BLOCK
    logits = jnp.concatenate([s_sel.reshape(b, MOBA_HEADS, q, nsel), s_own], axis=-1)
    mask = jnp.concatenate([m_sel.reshape(b, MOBA_HEADS, q, nsel), m_own], axis=-1)
    pr = masked_softmax(logits, mask).astype(vb.dtype)
    o_m = (jnp.einsum('bhqjk,bhqjkd->bqhd', pr[..., :nsel].reshape(b, MOBA_HEADS, q, kk, MOBA_BLOCK), vg)
           + jnp.einsum('bhqk,bhkd->bqhd', pr[..., nsel:], v_own))
    nc = cmp.shape[1]
    epos = jnp.arange(nc) * NSA_CMP_STRIDE + (NSA_CMP_LEN - 1)
    d_c = qpos[:, None] - epos[None, :]
    s_c = jnp.einsum('bqhd,bnd->bhqn', q_n, cmp[:, :, 0]) * scale + t_n[:, rel_bucket(d_c)]
    p_c = masked_softmax(s_c, d_c >= 0)
    o_c = jnp.einsum('bhqn,bnd->bqhd', p_c.astype(cmp.dtype), cmp[:, :, 1])
    nbs = sel_k.shape[1]
    per = NSA_SEL_BLOCK // NSA_CMP_STRIDE
    imp = jnp.pad(jnp.sum(p_c, axis=1), ((0, 0), (0, 0), (0, nbs * per - nc))).reshape(b, q, nbs, per).sum(-1)
    cur = qpos // NSA_SEL_BLOCK
    j = jnp.arange(nbs)
    forced = (j == 0) | (j == cur[:, None]) | (j == cur[:, None] - 1)
    score = jnp.where(forced, jnp.inf, jnp.where(j <= cur[:, None], imp, -jnp.inf))
    nn_ = min(NSA_TOPN, nbs)
    sval, sidx = lax.top_k(score, nn_)
    bq = jnp.arange(b)[:, None, None]
    ks = sel_k[bq, sidx]
    vs = sel_v[bq, sidx]
    d_s = qpos[:, None, None] - (sidx[..., None] * NSA_SEL_BLOCK + jnp.arange(NSA_SEL_BLOCK))
    m_s = (sval > -jnp.inf)[..., None] & (d_s >= 0)
    hn = jnp.arange(NSA_HEADS)[None, :, None, None, None]
    s_s = jnp.einsum('bqhd,bqjkd->bhqjk', q_n, ks) * scale + t_n[hn, rel_bucket(d_s)[:, None]]
    wsel = nn_ * NSA_SEL_BLOCK
    p_s = masked_softmax(s_s.reshape(b, NSA_HEADS, q, wsel), m_s.reshape(b, 1, q, wsel))
    o_s = jnp.einsum('bhqm,bqmd->bqhd', p_s.astype(vs.dtype), vs.reshape(b, q, wsel, HEAD_DIM))
    kpos_w = q0 - NSA_WINDOW + jnp.arange(band.shape[1])
    d_w = qpos[:, None] - kpos_w[None, :]
    m_w = (kpos_w[None, :] >= 0) & (d_w >= 0) & (d_w < NSA_WINDOW)
    s_w = jnp.einsum('bqhd,brd->bhqr', q_n, band[:, :, 0]) * scale + t_n[:, rel_bucket(d_w)]
    p_w = masked_softmax(s_w, m_w)
    o_w = jnp.einsum('bhqr,brd->bqhd', p_w.astype(band.dtype), band[:, :, 1])
    o_n = g_n[..., 0:1] * o_c + g_n[..., 1:2] * o_s + g_n[..., 2:3] * o_w
    return o_m, o_n


def ssd_scan(x, dt, a, bm, cm, init):
    b, l, h, p = x.shape
    cl = min(SSD_CHUNK, l)
    lp = round_up(l, cl)
    pad = lp - l
    if pad:
        x = jnp.pad(x, ((0, 0), (0, pad), (0, 0), (0, 0)))
        dt = jnp.pad(dt, ((0, 0), (0, pad), (0, 0)))
        bm = jnp.pad(bm, ((0, 0), (0, pad), (0, 0), (0, 0)))
        cm = jnp.pad(cm, ((0, 0), (0, pad), (0, 0), (0, 0)))
    nc = lp // cl
    rep = h // SSD_GROUPS
    bh = jnp.repeat(bm, rep, axis=2).astype(jnp.float32).reshape(b, nc, cl, h, SSD_STATE)
    ch = jnp.repeat(cm, rep, axis=2).astype(jnp.float32).reshape(b, nc, cl, h, SSD_STATE)
    xdt = (x.astype(jnp.float32) * dt[..., None]).reshape(b, nc, cl, h, p)
    acum = jnp.cumsum((dt * a).reshape(b, nc, cl, h).transpose(0, 3, 1, 2), axis=-1)
    seg = acum[..., :, None] - acum[..., None, :]
    tri = jnp.tril(jnp.ones((cl, cl), dtype=bool))
    lmat = jnp.exp(jnp.where(tri, seg, -jnp.inf))
    cb = jnp.einsum('bclhn,bcshn->bhcls', ch, bh)
    y_diag = jnp.einsum('bhcls,bcshp->bclhp', cb * lmat, xdt)
    decay = jnp.exp(acum[..., -1:] - acum)
    states = jnp.einsum('bclhn,bhcl,bclhp->bchpn', bh, decay, xdt)
    chunk_decay = jnp.exp(acum[..., -1])

    def step(carry, inp):
        st, dec = inp
        return carry * dec[:, :, None, None] + st, carry

    fin, prev = lax.scan(step, init.astype(jnp.float32), (states.transpose(1, 0, 2, 3, 4), chunk_decay.transpose(2, 0, 1)))
    prev = prev.transpose(1, 0, 2, 3, 4)
    y_off = jnp.einsum('bclhn,bchpn,bhcl->bclhp', ch, prev, jnp.exp(acum))
    y = (y_diag + y_off).reshape(b, lp, h, p)[:, :l]
    return y.astype(x.dtype), fin


def ssd_branch(z, xbc, dt_raw, conv_prev, ssm_init, p):
    b, l, cdim = xbc.shape
    xpad = jnp.concatenate([conv_prev.astype(xbc.dtype), xbc], axis=1)
    conv = lax.conv_general_dilated(xpad, p['ssd_conv_w'][:, None, :].astype(xbc.dtype), (1,), 'VALID',
                                    dimension_numbers=('NWC', 'WIO', 'NWC'), feature_group_count=cdim) + p['ssd_conv_b']
    new_conv = xpad[:, l:]
    act = jax.nn.silu(conv)
    gn = SSD_GROUPS * SSD_STATE
    xs = act[..., :SSD_INNER].reshape(b, l, SSD_HEADS, SSD_HEAD_DIM)
    bm = act[..., SSD_INNER:SSD_INNER + gn].reshape(b, l, SSD_GROUPS, SSD_STATE)
    cm = act[..., SSD_INNER + gn:].reshape(b, l, SSD_GROUPS, SSD_STATE)
    dt = jax.nn.softplus(dt_raw.astype(jnp.float32) + p['ssd_dt_bias'].astype(jnp.float32))
    a = -jnp.exp(p['ssd_a_log'].astype(jnp.float32))
    y, fin = ssd_scan(xs, dt, a, bm, cm, ssm_init)
    y = y + p['ssd_d'][:, None] * xs
    y = (y.reshape(b, l, SSD_INNER) * jax.nn.silu(z)).astype(jnp.float32).reshape(b, l, SSD_GROUPS, SSD_INNER // SSD_GROUPS)
    y = (y * lax.rsqrt(jnp.mean(y * y, axis=-1, keepdims=True) + EPS)).reshape(b, l, SSD_INNER)
    return (y * p['ssd_norm_w']).astype(z.dtype), new_conv, fin


def mixer_output(o_m, o_n, y_ssd, g_br, p):
    b, l = o_m.shape[:2]
    ya = o_m.reshape(b, l, MOBA_W) @ p['w_branch_moba']
    yb = o_n.reshape(b, l, NSA_W) @ p['w_branch_nsa']
    yc = y_ssd @ p['w_branch_ssd']
    ga, gb, gc = jnp.split(g_br, 3, axis=-1)
    return (ga * ya + gb * yb + gc * yc) @ p['w_out']


def prompt_mixer(h, p, rel_bias):
    b, s, _ = h.shape
    q_m, kv_m, q_n, kv_n, g_n, z, xbc, dt_raw, g_br = mixer_inputs(h, p['w_in'])
    lf = round_up(s, MOBA_BLOCK)
    moba_full = jnp.pad(kv_m, ((0, 0), (0, lf - s), (0, 0), (0, 0), (0, 0)))
    nsa_full = jnp.pad(kv_n[:, :, :4], ((0, 0), (0, lf - s), (0, 0), (0, 0)))
    ctx = attn_context(moba_full, nsa_full, p['cmp'])
    band_all = jnp.pad(kv_n[:, :, 4:], ((0, 0), (NSA_WINDOW, 0), (0, 0), (0, 0)))

    def chunk(i):
        q0 = i * Q_BLOCK
        sl = lambda arr: lax.dynamic_slice_in_dim(arr, q0, Q_BLOCK, axis=1)
        band = lax.dynamic_slice_in_dim(band_all, q0, NSA_WINDOW + Q_BLOCK, axis=1)
        return attn_chunk(sl(q_m), sl(q_n), sl(g_n), q0, ctx, band, rel_bias)

    o_m, o_n = lax.map(chunk, jnp.arange(s // Q_BLOCK))
    o_m = o_m.transpose(1, 0, 2, 3, 4).reshape(b, s, MOBA_HEADS, HEAD_DIM)
    o_n = o_n.transpose(1, 0, 2, 3, 4).reshape(b, s, NSA_HEADS, HEAD_DIM)
    conv0 = jnp.zeros((b, SSD_CONV - 1, SSD_CONV_DIM), h.dtype)
    ssm0 = jnp.zeros((b, SSD_HEADS, SSD_HEAD_DIM, SSD_STATE), jnp.float32)
    y_ssd, conv_new, ssm_new = ssd_branch(z, xbc, dt_raw, conv0, ssm0, p)
    keep = min(NSA_WINDOW, s)
    return mixer_output(o_m, o_n, y_ssd, g_br, p), (kv_m, kv_n[:, :, :4], kv_n[:, s - keep:, 4:], conv_new, ssm_new)


def sample_mixer(h, p, rel_bias, moba_past, nsa_past, win_buf, conv_prev, ssm_prev):
    b, t, _ = h.shape
    past = moba_past.shape[1]
    q_m, kv_m, q_n, kv_n, g_n, z, xbc, dt_raw, g_br = mixer_inputs(h, p['w_in'])
    lf = round_up(past + t, MOBA_BLOCK)
    tail = lf - past - t
    moba_full = jnp.concatenate([moba_past.astype(kv_m.dtype), kv_m, jnp.zeros((b, tail, 2, MOBA_HEADS, HEAD_DIM), kv_m.dtype)], axis=1)
    nsa_full = jnp.concatenate([nsa_past.astype(kv_n.dtype), kv_n[:, :, :4], jnp.zeros((b, tail, 4, HEAD_DIM), kv_n.dtype)], axis=1)
    ctx = attn_context(moba_full, nsa_full, p['cmp'])
    wb = win_buf.shape[1]
    band = jnp.concatenate([jnp.zeros((b, NSA_WINDOW - wb, 2, HEAD_DIM), kv_n.dtype), win_buf.astype(kv_n.dtype), kv_n[:, :, 4:]], axis=1)
    o_m, o_n = attn_chunk(q_m, q_n, g_n, past, ctx, band, rel_bias)
    y_ssd, conv_new, ssm_new = ssd_branch(z, xbc, dt_raw, conv_prev, ssm_prev, p)
    return mixer_output(o_m, o_n, y_ssd, g_br, p), (kv_m, kv_n[:, :, :4], kv_n[:, :, 4:], conv_new, ssm_new)


def routed_experts(h, eidx, gate_w, wg, wu, wd):
    n, d = h.shape
    nk = n * TOP_K
    e_flat = eidx.reshape(nk)
    order = jnp.argsort(e_flat)
    e_s = e_flat[order]
    tok_s = (jnp.arange(nk, dtype=jnp.int32) // TOP_K)[order]
    w_s = gate_w.reshape(nk)[order]
    cnt = jnp.bincount(e_flat, length=N_EXPERTS)
    padded = (cnt + MOE_BLOCK - 1) // MOE_BLOCK * MOE_BLOCK
    ends = jnp.cumsum(padded)
    dest = (ends - padded)[e_s] + jnp.arange(nk) - (jnp.cumsum(cnt) - cnt)[e_s]
    n_blk = -(-(nk + N_EXPERTS * (MOE_BLOCK - 1)) // MOE_BLOCK)
    rows = n_blk * MOE_BLOCK
    row_tok = jnp.full((rows,), n, jnp.int32).at[dest].set(tok_s)
    row_w = jnp.zeros((rows,), gate_w.dtype).at[dest].set(w_s)
    blk_e = jnp.minimum(jnp.searchsorted(ends, jnp.arange(n_blk) * MOE_BLOCK, side='right'), N_EXPERTS - 1)
    x_rows = jnp.concatenate([h, jnp.zeros((1, d), h.dtype)], axis=0)[row_tok].reshape(n_blk, MOE_BLOCK, d)

    def expert_block(args):
        xb, e = args
        return (jax.nn.silu(xb @ wg[e]) * (xb @ wu[e])) @ wd[e]

    y_rows = lax.map(expert_block, (x_rows, blk_e)).reshape(rows, d)
    y = jnp.zeros((n + 1, d), jnp.float32).at[row_tok].add(y_rows * row_w[:, None])
    return y[:n].astype(h.dtype)


def moe_ffn(h, p):
    n = h.shape[0]
    s = jax.nn.sigmoid((h @ p['router_w']).astype(jnp.float32))
    sc = s + p['router_b']
    per = N_EXPERTS // N_ROUTE_GROUPS
    gscore = jnp.sum(lax.top_k(sc.reshape(n, N_ROUTE_GROUPS, per), 2)[0], axis=-1)
    _, gi = lax.top_k(gscore, TOPK_GROUPS)
    gmask = jnp.any(gi[:, :, None] == jnp.arange(N_ROUTE_GROUPS), axis=1)
    emask = jnp.repeat(gmask, per, axis=1)
    _, ei = lax.top_k(jnp.where(emask, sc, -jnp.inf), TOP_K)
    w = jnp.take_along_axis(s, ei, axis=1)
    w = w / jnp.sum(w, axis=-1, keepdims=True) * ROUTE_SCALE
    routed = routed_experts(h, ei, w, p['exp_w_gate'], p['exp_w_up'], p['exp_w_down'])
    shared = (jax.nn.silu(h @ p['shared_w_gate']) * (h @ p['shared_w_up'])) @ p['shared_w_down']
    return routed + shared


def residual_layer(x, c, p, mixer_fn):
    sh1, sc1, g1, sh2, sc2, g2 = modulation(c, p['ada_w'], p['ada_b'])
    h = rmsnorm(x, p['norm'][0]) * (1 + sc1) + sh1
    m, state = mixer_fn(h)
    x = x + g1 * rmsnorm(m, p['norm'][1])
    h2 = rmsnorm(x, p['norm'][2]) * (1 + sc2) + sh2
    b, l, d = x.shape
    f = moe_ffn(h2.reshape(b * l, d), p).reshape(b, l, d)
    x = x + g2 * rmsnorm(f, p['norm'][3])
    return x, state


def setup_inputs(seed: int = 0) -> dict:
    key = jax.random.key(seed)
    ks = iter(jax.random.split(key, 48))
    nrm = lambda shape, sc: jax.random.normal(next(ks), shape, jnp.float32) * sc
    n_pages = PAST_LEN // PAGE_SIZE
    n_pool = DEC_BATCH * n_pages * 5 // 4
    win_buf = min(NSA_WINDOW, PAST_LEN)
    d = D_MODEL
    inp = {}
    inp['x_prompt'] = nrm((BATCH, SEQ, d), 1.0)
    inp['x_sample'] = nrm((DEC_BATCH, DEC_SEQ, d), 1.0)
    inp['c_prompt'] = nrm((BATCH, d), 1.0)
    inp['c_sample'] = nrm((DEC_BATCH, d), 1.0)
    inp['cache_moba_kv'] = nrm((DEPTH, n_pool, PAGE_SIZE, 2, MOBA_HEADS, HEAD_DIM), 1.0)
    inp['cache_nsa_kv'] = nrm((DEPTH, n_pool, PAGE_SIZE, 4, HEAD_DIM), 1.0)
    inp['cache_nsa_win'] = nrm((DEPTH, DEC_BATCH, win_buf, 2, HEAD_DIM), 1.0)
    inp['state_ssd_conv'] = nrm((DEPTH, DEC_BATCH, SSD_CONV - 1, SSD_CONV_DIM), 1.0)
    inp['state_ssd'] = nrm((DEPTH, DEC_BATCH, SSD_HEADS, SSD_HEAD_DIM, SSD_STATE), 0.3)
    inp['page_table'] = jax.random.permutation(next(ks), n_pool)[:DEC_BATCH * n_pages].reshape(DEC_BATCH, n_pages).astype(jnp.int32)
    inp['rel_bias'] = nrm((N_BUCKETS, N_ATTN_HEADS), 0.5)
    inp['ada_w'] = nrm((DEPTH, d, 6 * d), 0.5 * d ** -0.5)
    inp['ada_b'] = nrm((DEPTH, 6 * d), 0.02)
    inp['norm_w'] = 1.0 + nrm((DEPTH, 4, d), 0.1)
    inp['w_in'] = nrm((DEPTH, d, IN_WIDTH), d ** -0.5)
    inp['nsa_cmp_w1'] = nrm((DEPTH, 2, NSA_CMP_LEN * HEAD_DIM, HEAD_DIM), (NSA_CMP_LEN * HEAD_DIM) ** -0.5)
    inp['nsa_cmp_b1'] = nrm((DEPTH, 2, HEAD_DIM), 0.02)
    inp['nsa_cmp_w2'] = nrm((DEPTH, 2, HEAD_DIM, HEAD_DIM), HEAD_DIM ** -0.5)
    inp['nsa_cmp_b2'] = nrm((DEPTH, 2, HEAD_DIM), 0.02)
    inp['nsa_cmp_pos'] = nrm((DEPTH, 2, NSA_CMP_LEN, HEAD_DIM), 0.5)
    inp['ssd_conv_w'] = nrm((DEPTH, SSD_CONV, SSD_CONV_DIM), SSD_CONV ** -0.5)
    inp['ssd_conv_b'] = nrm((DEPTH, SSD_CONV_DIM), 0.02)
    dt0 = jnp.exp(jax.random.uniform(next(ks), (DEPTH, SSD_HEADS), jnp.float32, math.log(1e-3), math.log(1e-1)))
    inp['ssd_dt_bias'] = dt0 + jnp.log(-jnp.expm1(-dt0))
    inp['ssd_a_log'] = jnp.log(jax.random.uniform(next(ks), (DEPTH, SSD_HEADS), jnp.float32, 1.0, 16.0))
    inp['ssd_d'] = 1.0 + nrm((DEPTH, SSD_HEADS), 0.1)
    inp['ssd_norm_w'] = 1.0 + nrm((DEPTH, SSD_INNER), 0.1)
    inp['w_branch_moba'] = nrm((DEPTH, MOBA_W, d), MOBA_W ** -0.5)
    inp['w_branch_nsa'] = nrm((DEPTH, NSA_W, d), NSA_W ** -0.5)
    inp['w_branch_ssd'] = nrm((DEPTH, SSD_INNER, d), SSD_INNER ** -0.5)
    inp['w_out'] = nrm((DEPTH, d, d), d ** -0.5)
    inp['router_w'] = nrm((DEPTH, d, N_EXPERTS), d ** -0.5)
    inp['router_b'] = nrm((DEPTH, N_EXPERTS), 0.01)
    inp['exp_w_gate'] = nrm((DEPTH, N_EXPERTS, d, D_EXPERT), d ** -0.5)
    inp['exp_w_up'] = nrm((DEPTH, N_EXPERTS, d, D_EXPERT), d ** -0.5)
    inp['exp_w_down'] = nrm((DEPTH, N_EXPERTS, D_EXPERT, d), D_EXPERT ** -0.5)
    inp['shared_w_gate'] = nrm((DEPTH, d, D_SHARED), d ** -0.5)
    inp['shared_w_up'] = nrm((DEPTH, d, D_SHARED), d ** -0.5)
    inp['shared_w_down'] = nrm((DEPTH, D_SHARED, d), D_SHARED ** -0.5)
    return inp


def reference(x_prompt, x_sample, c_prompt, c_sample, cache_moba_kv, cache_nsa_kv, cache_nsa_win, state_ssd_conv, state_ssd, page_table, rel_bias, ada_w, ada_b, norm_w, w_in, nsa_cmp_w1, nsa_cmp_b1, nsa_cmp_w2, nsa_cmp_b2, nsa_cmp_pos, ssd_conv_w, ssd_conv_b, ssd_dt_bias, ssd_a_log, ssd_d, ssd_norm_w, w_branch_moba, w_branch_nsa, w_branch_ssd, w_out, router_w, router_b, exp_w_gate, exp_w_up, exp_w_down, shared_w_gate, shared_w_up, shared_w_down):
    dec_b = x_sample.shape[0]
    past = page_table.shape[1] * PAGE_SIZE
    xp, xs = x_prompt, x_sample
    outs = [[] for _ in range(10)]
    for l in range(DEPTH):
        p = {'ada_w': ada_w[l], 'ada_b': ada_b[l], 'norm': norm_w[l], 'w_in': w_in[l],
             'cmp': (nsa_cmp_w1[l], nsa_cmp_b1[l], nsa_cmp_w2[l], nsa_cmp_b2[l], nsa_cmp_pos[l]),
             'ssd_conv_w': ssd_conv_w[l], 'ssd_conv_b': ssd_conv_b[l], 'ssd_dt_bias': ssd_dt_bias[l],
             'ssd_a_log': ssd_a_log[l], 'ssd_d': ssd_d[l], 'ssd_norm_w': ssd_norm_w[l],
             'w_branch_moba': w_branch_moba[l], 'w_branch_nsa': w_branch_nsa[l], 'w_branch_ssd': w_branch_ssd[l],
             'w_out': w_out[l], 'router_w': router_w[l], 'router_b': router_b[l],
             'exp_w_gate': exp_w_gate[l], 'exp_w_up': exp_w_up[l], 'exp_w_down': exp_w_down[l],
             'shared_w_gate': shared_w_gate[l], 'shared_w_up': shared_w_up[l], 'shared_w_down': shared_w_down[l]}
        xp, st_p = residual_layer(xp, c_prompt, p, lambda h: prompt_mixer(h, p, rel_bias))
        moba_past = cache_moba_kv[l, page_table].reshape(dec_b, past, 2, MOBA_HEADS, HEAD_DIM)
        nsa_past = cache_nsa_kv[l, page_table].reshape(dec_b, past, 4, HEAD_DIM)
        xs, st_s = residual_layer(xs, c_sample, p, lambda h: sample_mixer(h, p, rel_bias, moba_past, nsa_past, cache_nsa_win[l], state_ssd_conv[l], state_ssd[l]))
        for i in range(5):
            outs[2 * i].append(st_p[i])
            outs[2 * i + 1].append(st_s[i])
    moba_kv_prompt = jnp.stack(outs[0])
    moba_kv_sample = jnp.stack(outs[1])
    nsa_kv_prompt = jnp.stack(outs[2])
    nsa_kv_sample = jnp.stack(outs[3])
    nsa_win_prompt = jnp.stack(outs[4])
    nsa_win_sample = jnp.stack(outs[5])
    ssd_conv_prompt = jnp.stack(outs[6])
    ssd_conv_sample = jnp.stack(outs[7])
    ssd_state_prompt = jnp.stack(outs[8])
    ssd_state_sample = jnp.stack(outs[9])
    return (xp, xs, moba_kv_prompt, moba_kv_sample, nsa_kv_prompt, nsa_kv_sample, nsa_win_prompt, nsa_win_sample, ssd_conv_prompt, ssd_conv_sample, ssd_state_prompt, ssd_state_sample)
```

```python
import functools
import math

import numpy as np
import jax
import jax.numpy as jnp
from jax import lax
from jax.experimental import pallas as pl
from jax.experimental.pallas import tpu as pltpu

F32 = jnp.float32
BF16 = jnp.bfloat16
HI = lax.Precision.HIGHEST

D_MODEL = 1024
PAGE = 128
HEAD_DIM = 64
MOBA_HEADS = 4
MOBA_BLOCK = 256
MOBA_TOPK = 3
NSA_HEADS = 4
NSA_CMP_STRIDE = 16
NSA_SEL_BLOCK = 64
NSA_TOPN = 16
NSA_WINDOW = 512
SSD_HEADS = 8
SSD_HEAD_DIM = 64
SSD_INNER = 512
SSD_GROUPS = 2
SSD_STATE = 128
SSD_CONV = 4
SSD_CHUNK = 256
SSD_CONV_DIM = 1024
N_BUCKETS = 32
MAX_DISTANCE = 128
N_EXPERTS = 64
TOP_K = 8
N_ROUTE_GROUPS = 8
TOPK_GROUPS = 4
D_EXPERT = 256
ROUTE_SCALE = 2.5
EPS = 1e-6

TM = 256
EXPERT_ROWS = 256
VMEM_LIMIT = 56 * 1024 * 1024
NEG = -1e30

_SEG = dict(qm=(0, 256), kvm=(256, 512), qn=(768, 256), nsa=(1024, 256), win=(1280, 128),
            gn=(1408, 128), z=(1536, 512), xbc=(2048, 1024), dt=(3072, 128), gbr=(3200, 3072))
PROJ_W = 6272


def _sigmoid(x):
    return 1.0 / (1.0 + jnp.exp(-x))


def _silu(x):
    return x * _sigmoid(x)


def _rms(x, w):
    return x * lax.rsqrt(jnp.mean(x * x, axis=-1, keepdims=True) + EPS) * w


def _dot(a, b):
    return jnp.dot(a.astype(BF16), b.astype(BF16), preferred_element_type=F32)


def _dot_nt(a, b, precision=None):
    if precision is None:
        a, b = a.astype(BF16), b.astype(BF16)
    return lax.dot_general(a, b, (((1,), (1,)), ((), ())), precision=precision,
                           preferred_element_type=F32)


def _cparams(sem, vmem=None):
    return pltpu.CompilerParams(dimension_semantics=sem, vmem_limit_bytes=vmem or VMEM_LIMIT)


def _mod_kernel(c_ref, w_ref, b_ref, o_ref):
    o_ref[...] = _dot(_silu(c_ref[...]), w_ref[...]) + b_ref[...]


def modulation(c_all, ada_w, ada_b, layer):
    rows, d = c_all.shape
    n = ada_w.shape[-1]
    tn = 512
    return pl.pallas_call(
        _mod_kernel,
        grid=(n // tn,),
        in_specs=[pl.BlockSpec((rows, d), lambda j: (0, 0)),
                  pl.BlockSpec((None, d, tn), lambda j: (layer, 0, j)),
                  pl.BlockSpec((None, 1, tn), lambda j: (layer, 0, j))],
        out_specs=pl.BlockSpec((rows, tn), lambda j: (0, j)),
        out_shape=jax.ShapeDtypeStruct((rows, n), F32),
        compiler_params=_cparams(("parallel",)),
        name="modulation",
    )(c_all, ada_w, ada_b.reshape(ada_b.shape[0], 1, n))


def _proj_kernel(x_ref, sh_ref, sc_ref, nw_ref, w_ref, qm_ref, kvm_ref, qn_ref, nsa_ref, win_ref,
                 gn_ref, z_ref, xbc_ref, dt_ref, gbr_ref):
    h = _rms(x_ref[...], nw_ref[...]) * (1.0 + sc_ref[...]) + sh_ref[...]
    hb = h.astype(BF16)

    def seg(name):
        o, w = _SEG[name]
        return jnp.dot(hb, w_ref[:, o:o + w], preferred_element_type=F32)

    qm_ref[...] = seg("qm")
    kvm_ref[...] = seg("kvm")
    qn_ref[...] = seg("qn")
    nsa_ref[...] = seg("nsa")
    win_ref[...] = seg("win")
    gn_ref[...] = _sigmoid(seg("gn"))
    z_ref[...] = seg("z")
    xbc_ref[...] = seg("xbc")
    dt_ref[...] = seg("dt")
    gbr_ref[...] = _sigmoid(seg("gbr"))


def _mod_spec(which, group_of_tile):
    return pl.BlockSpec((None, TM, D_MODEL), lambda i: (group_of_tile(i), 0, which))


def proj_in(x, modx, nw, w_packed, group_of_tile):
    n, d = x.shape
    names = ["qm", "kvm", "qn", "nsa", "win", "gn", "z", "xbc", "dt", "gbr"]
    return pl.pallas_call(
        _proj_kernel,
        grid=(n // TM,),
        in_specs=[pl.BlockSpec((TM, d), lambda i: (i, 0)),
                  _mod_spec(0, group_of_tile), _mod_spec(1, group_of_tile),
                  pl.BlockSpec((1, d), lambda i: (0, 0)),
                  pl.BlockSpec((d, PROJ_W), lambda i: (0, 0), pipeline_mode=pl.Buffered(1))],
        out_specs=[pl.BlockSpec((TM, _SEG[k][1]), lambda i: (i, 0)) for k in names],
        out_shape=[jax.ShapeDtypeStruct((n, _SEG[k][1]), F32) for k in names],
        compiler_params=_cparams(("parallel",)),
        name="proj_in",
    )(x, modx, modx, nw, w_packed)


CTX_PAGES = 16


def _page_ctx_kernel(k_ref, g_ref, w_ref, ks_ref, pab_ref):
    ks_ref[...] = jnp.sum(k_ref[...], axis=1)
    pab_ref[...] = _dot(g_ref[...], w_ref[...])


def page_ctx(kv_pages, nsa_pages, w_exp, page0, n_pages):
    groups = PAGE // NSA_CMP_STRIDE
    g_view = nsa_pages.reshape(nsa_pages.shape[0] * groups, NSA_CMP_STRIDE * 256)
    blk0 = page0 // CTX_PAGES
    return pl.pallas_call(
        _page_ctx_kernel,
        grid=(n_pages // CTX_PAGES,),
        in_specs=[pl.BlockSpec((CTX_PAGES, PAGE, 256), lambda i: (blk0 + i, 0, 0)),
                  pl.BlockSpec((CTX_PAGES * groups, NSA_CMP_STRIDE * 256), lambda i: (blk0 + i, 0)),
                  pl.BlockSpec((NSA_CMP_STRIDE * 256, 256), lambda i: (0, 0))],
        out_specs=[pl.BlockSpec((CTX_PAGES, 256), lambda i: (i, 0)),
                   pl.BlockSpec((CTX_PAGES * groups, 256), lambda i: (i, 0))],
        out_shape=[jax.ShapeDtypeStruct((n_pages, 256), F32),
                   jax.ShapeDtypeStruct((n_pages * groups, 256), F32)],
        compiler_params=_cparams(("parallel",)),
        name="page_ctx",
    )(kv_pages, g_view, w_exp)


def _ctx_gather_kernel(pt_ref, ks_hbm, pab_hbm, ks_out, pab_out, sem):
    b = pl.program_id(0)
    n_pages = ks_out.shape[1]
    groups = pab_out.shape[1] // n_pages

    def copies(p):
        page = pt_ref[b * n_pages + p]
        c1 = pltpu.make_async_copy(ks_hbm.at[pl.ds(page, 1)], ks_out.at[0, pl.ds(p, 1)], sem.at[0])
        c2 = pltpu.make_async_copy(pab_hbm.at[pl.ds(page * groups, groups)],
                                   pab_out.at[0, pl.ds(p * groups, groups)], sem.at[1])
        return c1, c2

    def start(p, carry):
        c1, c2 = copies(p)
        c1.start()
        c2.start()
        return carry

    def wait(p, carry):
        c1, c2 = copies(p)
        c1.wait()
        c2.wait()
        return carry

    lax.fori_loop(0, n_pages, start, 0)
    lax.fori_loop(0, n_pages, wait, 0)


def ctx_gather(page_table_flat, ksum, pab, n_seq, n_pages):
    groups = PAGE // NSA_CMP_STRIDE
    gs = pltpu.PrefetchScalarGridSpec(
        num_scalar_prefetch=1,
        grid=(n_seq,),
        in_specs=[pl.BlockSpec(memory_space=pl.ANY), pl.BlockSpec(memory_space=pl.ANY)],
        out_specs=[pl.BlockSpec((1, n_pages, 256), lambda b, pt: (b, 0, 0)),
                   pl.BlockSpec((1, n_pages * groups, 256), lambda b, pt: (b, 0, 0))],
        scratch_shapes=[pltpu.SemaphoreType.DMA((2,))],
    )
    return pl.pallas_call(
        _ctx_gather_kernel,
        grid_spec=gs,
        out_shape=[jax.ShapeDtypeStruct((n_seq, n_pages, 256), F32),
                   jax.ShapeDtypeStruct((n_seq, n_pages * groups, 256), F32)],
        compiler_params=_cparams(("arbitrary",)),
        name="ctx_gather",
    )(page_table_flat, ksum, pab)


def _gelu_tanh(x):
    return 0.5 * x * (1.0 + jnp.tanh(math.sqrt(2.0 / math.pi) * (x + 0.044715 * (x * x * x))))


def _ctx_final_kernel(ks_ref, pab_ref, pos_ref, w1_ref, b1_ref, w2_ref, b2_ref, km_ref, cmp_ref):
    n_pages = ks_ref.shape[0]
    nb = n_pages // 2
    r = lax.broadcasted_iota(jnp.int32, (nb, n_pages), 0)
    c = lax.broadcasted_iota(jnp.int32, (nb, n_pages), 1)
    pair = jnp.where((c == 2 * r) | (c == 2 * r + 1), 1.0, 0.0).astype(F32)
    km_ref[...] = jnp.dot(pair, ks_ref[...], precision=HI, preferred_element_type=F32) * (1.0 / MOBA_BLOCK)

    pab = pab_ref[...]
    ng = pab.shape[0]
    row = lax.broadcasted_iota(jnp.int32, (ng, 1), 0)
    outs = []
    for kv in range(2):
        pa = pab[:, 128 * kv:128 * kv + 64]
        pb = pab[:, 128 * kv + 64:128 * kv + 128]
        pb_next = jnp.where(row == ng - 1, 0.0, pltpu.roll(pb, ng - 1, 0))
        pos_term = _dot(pos_ref[kv], w1_ref[kv]) + b1_ref[kv]
        hid = _gelu_tanh(pa + pb_next + pos_term)
        outs.append(_dot(hid, w2_ref[kv]) + b2_ref[kv])
    cmp_ref[...] = jnp.concatenate(outs, axis=1)


def ctx_final(ksum, pab, pos, w1, b1, w2, b2, layer):
    nb, n_pages, _ = ksum.shape
    ng = pab.shape[1]
    lf = pos.shape[2] * pos.shape[3]
    pos2 = pos.reshape(pos.shape[0], 2, 1, lf)
    return pl.pallas_call(
        _ctx_final_kernel,
        grid=(nb,),
        in_specs=[pl.BlockSpec((None, n_pages, 256), lambda b: (b, 0, 0)),
                  pl.BlockSpec((None, ng, 256), lambda b: (b, 0, 0)),
                  pl.BlockSpec((None, 2, 1, lf), lambda b: (layer, 0, 0, 0)),
                  pl.BlockSpec((None, 2, lf, HEAD_DIM), lambda b: (layer, 0, 0, 0)),
                  pl.BlockSpec((None, 2, 1, HEAD_DIM), lambda b: (layer, 0, 0, 0)),
                  pl.BlockSpec((None, 2, HEAD_DIM, HEAD_DIM), lambda b: (layer, 0, 0, 0)),
                  pl.BlockSpec((None, 2, 1, HEAD_DIM), lambda b: (layer, 0, 0, 0))],
        out_specs=[pl.BlockSpec((None, n_pages // 2, 256), lambda b: (b, 0, 0)),
                   pl.BlockSpec((None, ng, 128), lambda b: (b, 0, 0))],
        out_shape=[jax.ShapeDtypeStruct((nb, n_pages // 2, 256), F32),
                   jax.ShapeDtypeStruct((nb, ng, 128), F32)],
        compiler_params=_cparams(("parallel",)),
        name="ctx_final",
    )(ksum, pab, pos2, w1, b1.reshape(b1.shape[0], 2, 1, HEAD_DIM), w2, b2.reshape(b2.shape[0], 2, 1, HEAD_DIM))


def _topk_lanes(score, k, lane_f, n_lanes):
    sel = jnp.zeros_like(score)
    for _ in range(k):
        m = jnp.max(score, axis=1, keepdims=True)
        idx = jnp.min(jnp.where(score == m, lane_f, float(n_lanes)), axis=1, keepdims=True)
        hit = lane_f == idx
        sel = jnp.where(hit & (m > -jnp.inf), 1.0, sel)
        score = jnp.where(hit, -jnp.inf, score)
    return sel


def _online(m_ref, l_ref, a_ref, idx, s, valid, v):
    s = jnp.where(valid, s, NEG)
    m_old = m_ref[idx]
    m_new = jnp.maximum(m_old, jnp.max(s, axis=1, keepdims=True))
    p = jnp.where(valid, jnp.exp(s - m_new), 0.0)
    alpha = jnp.exp(m_old - m_new)
    l_ref[idx] = alpha * l_ref[idx] + jnp.sum(p, axis=1, keepdims=True)
    a_ref[idx] = alpha * a_ref[idx] + _dot(p, v)
    m_ref[idx] = m_new


def _attn_kernel(si_ref, st_ref, sl_ref, pg_ref,
                 qm_ref, qn_ref, gn_ref, kv_ref, ns_ref, tkv_ref, tns_ref, win_ref, km_ref, cmp_ref,
                 tbm_ref, tbn_ref, cb_ref, pool_ref,
                 om_ref, on_ref,
                 qms, qns, selm, sels, m_m, l_m, a_m, m_n, l_n, a_n, oc,
                 *, q_rows, q0_base, tail_tile):
    Q = q_rows
    s_id = pl.program_id(1)
    i = si_ref[s_id]
    t = st_ref[s_id]
    q0 = q0_base + i * Q
    ob = lax.shift_right_logical(q0, 8)
    nb = km_ref.shape[0]
    nbs_p = pool_ref.shape[1]
    nc = cmp_ref.shape[0]

    qi = lax.broadcasted_iota(jnp.int32, (Q, 1), 0)
    qi4 = jnp.concatenate([qi] * NSA_HEADS, axis=0)
    lane_nb = lax.broadcasted_iota(jnp.int32, (Q, nb), 1)
    lane_bs = lax.broadcasted_iota(jnp.int32, (Q, nbs_p), 1)

    @pl.when(t == 0)
    def _init():
        qm = qm_ref[...]
        qn = qn_ref[...]
        qms[...] = qm * (HEAD_DIM ** -0.5)
        for h in range(NSA_HEADS):
            qns[h * Q:(h + 1) * Q, :] = qn[:, HEAD_DIM * h:HEAD_DIM * (h + 1)] * (HEAD_DIM ** -0.5)
        km = km_ref[...]
        lane_f = lane_nb.astype(F32)
        for h in range(MOBA_HEADS):
            sl = slice(HEAD_DIM * h, HEAD_DIM * (h + 1))
            g = _dot_nt(qm[:, sl], km[:, sl], precision=HI)
            g = jnp.where(lane_nb < ob, g, -jnp.inf)
            selm[h] = _topk_lanes(g, min(MOBA_TOPK, nb), lane_f, nb)
        cm = cmp_ref[...]
        sc = _dot_nt(qns[...], cm[:, :HEAD_DIM]) + cb_ref[...]
        epos = lax.broadcasted_iota(jnp.int32, (NSA_HEADS * Q, nc), 1) * NSA_CMP_STRIDE + (2 * NSA_CMP_STRIDE - 1)
        valid = epos <= q0 + qi4
        sc = jnp.where(valid, sc, NEG)
        mx = jnp.max(sc, axis=1, keepdims=True)
        e = jnp.where(valid, jnp.exp(sc - mx), 0.0)
        den = jnp.sum(e, axis=1, keepdims=True)
        p = e / jnp.where(den > 0, den, 1.0)
        oc[...] = _dot(p, cm[:, HEAD_DIM:])
        psum = p[0:Q] + p[Q:2 * Q] + p[2 * Q:3 * Q] + p[3 * Q:4 * Q]
        imp = jnp.dot(psum, pool_ref[...], precision=HI, preferred_element_type=F32)
        cur = lax.shift_right_logical(q0 + qi, 6)
        forced = (lane_bs == 0) | (lane_bs == cur) | (lane_bs == cur - 1)
        score = jnp.where(forced, jnp.inf, jnp.where(lane_bs <= cur, imp, -jnp.inf))
        sels[...] = _topk_lanes(score, NSA_TOPN, lane_bs.astype(F32), nbs_p)
        m_m[...] = jnp.full(m_m.shape, NEG, F32)
        l_m[...] = jnp.zeros(l_m.shape, F32)
        a_m[...] = jnp.zeros(a_m.shape, F32)
        m_n[...] = jnp.full(m_n.shape, NEG, F32)
        l_n[...] = jnp.zeros(l_n.shape, F32)
        a_n[...] = jnp.zeros(a_n.shape, F32)

    delta = q0 - t * PAGE
    kj = lax.broadcasted_iota(jnp.int32, (Q, PAGE), 1)
    d = delta + qi - kj
    causal = d >= 0
    v_idx = jnp.minimum(lax.shift_right_logical(delta, 7), 2)

    if tail_tile is None:
        kv = kv_ref[...]
        ns = ns_ref[...]
    else:
        is_tail = t == tail_tile
        kv = jnp.where(is_tail, tkv_ref[...], kv_ref[...])
        ns = jnp.where(is_tail, tns_ref[...], ns_ref[...])

    blk = lax.shift_right_logical(t, 1)
    own = blk == ob
    q_all = qms[...]
    for h in range(MOBA_HEADS):
        sl = slice(HEAD_DIM * h, HEAD_DIM * (h + 1))
        s = _dot_nt(q_all[:, sl], kv[:, sl]) + tbm_ref[h, v_idx]
        selcol = jnp.sum(jnp.where(lane_nb == blk, selm[h], 0.0), axis=1, keepdims=True)
        valid = causal & ((selcol > 0.0) | own)
        _online(m_m, l_m, a_m, h, s, valid, kv[:, 256 + HEAD_DIM * h:256 + HEAD_DIM * (h + 1)])

    qn4 = qns[...]
    s = _dot_nt(qn4, ns[:, :HEAD_DIM]) + tbn_ref[v_idx]
    sel_all = sels[...]
    lo = jnp.sum(jnp.where(lane_bs == 2 * t, sel_all, 0.0), axis=1, keepdims=True)
    hi = jnp.sum(jnp.where(lane_bs == 2 * t + 1, sel_all, 0.0), axis=1, keepdims=True)
    vmask = causal & (jnp.where(kj < NSA_SEL_BLOCK, lo, hi) > 0.0)
    _online(m_n, l_n, a_n, 0, s, jnp.concatenate([vmask] * NSA_HEADS, axis=0), ns[:, HEAD_DIM:])

    @pl.when(t >= lax.shift_right_logical(q0, 7) - NSA_WINDOW // PAGE)
    def _window():
        wv = win_ref[...]
        sw = _dot_nt(qn4, wv[:, :HEAD_DIM]) + tbn_ref[v_idx]
        wmask = causal & (d < NSA_WINDOW)
        _online(m_n, l_n, a_n, 1, sw, jnp.concatenate([wmask] * NSA_HEADS, axis=0), wv[:, HEAD_DIM:])

    @pl.when(sl_ref[s_id] == 1)
    def _finish():
        outs = []
        for h in range(MOBA_HEADS):
            l = l_m[h]
            outs.append(a_m[h] / jnp.where(l > 0, l, 1.0))
        om_ref[...] = jnp.concatenate(outs, axis=1)
        l = l_n[0]
        o_s = a_n[0] / jnp.where(l > 0, l, 1.0)
        l = l_n[1]
        o_w = a_n[1] / jnp.where(l > 0, l, 1.0)
        o_c = oc[...]
        gn = gn_ref[...]
        outs = []
        for h in range(NSA_HEADS):
            r = slice(h * Q, (h + 1) * Q)
            outs.append(gn[:, 3 * h:3 * h + 1] * o_c[r] + gn[:, 3 * h + 1:3 * h + 2] * o_s[r]
                        + gn[:, 3 * h + 2:3 * h + 3] * o_w[r])
        on_ref[...] = jnp.concatenate(outs, axis=1)


def _rel_bucket_np(dist):
    n = np.maximum(dist, 0)
    exact = N_BUCKETS // 2
    nf = np.maximum(n, 1).astype(np.float32)
    large = exact + (np.log(nf / np.float32(exact)) / np.float32(math.log(MAX_DISTANCE / exact))
                     * np.float32(N_BUCKETS - exact)).astype(np.int32)
    return np.where(n < exact, n, np.minimum(large, N_BUCKETS - 1)).astype(np.int32)


def attention_tables(rel_bias, q_rows, n_chunks, q0_base, nc):
    Q = q_rows
    qi = np.arange(Q)[:, None]
    kj = np.arange(PAGE)[None, :]
    d = np.stack([v * PAGE + qi - kj for v in range(3)])
    tb = jnp.take(rel_bias, jnp.asarray(_rel_bucket_np(d)), axis=0)
    tbm = tb[..., :MOBA_HEADS].transpose(3, 0, 1, 2)
    tbn = tb[..., MOBA_HEADS:].transpose(0, 3, 1, 2).reshape(3, NSA_HEADS * Q, PAGE)
    qpos = q0_base + np.arange(n_chunks)[:, None, None] * Q + qi[None]
    epos = (np.arange(nc) * NSA_CMP_STRIDE + 2 * NSA_CMP_STRIDE - 1)[None, None, :]
    cb = jnp.take(rel_bias, jnp.asarray(_rel_bucket_np(qpos - epos)), axis=0)[..., MOBA_HEADS:]
    cb = cb.transpose(0, 3, 1, 2).reshape(n_chunks, NSA_HEADS * Q, nc)
    return tbm, tbn, cb


def attention(qm, qn, gn, q_blk0, kv_pages, ns_pages, page_idx, tail_kv, tail_ns, win_pages, win_w0t,
              kmean, cmp, tables, *, n_seq, q_rows, n_chunks, q0_base, tiles_per_seq, n_win_tiles, n_sel_blocks):
    Q = q_rows
    tbm, tbn, cb = tables
    nb = kmean.shape[1]
    nc = cmp.shape[1]
    nbs_p = -(-n_sel_blocks // 128) * 128
    per = NSA_SEL_BLOCK // NSA_CMP_STRIDE
    pool = jnp.asarray((np.arange(nc)[:, None] // per == np.arange(nbs_p)[None, :]).astype(np.float32))
    has_tail = tail_kv is not None

    steps = []
    for i in range(n_chunks):
        t_last = (q0_base + i * Q + Q - 1) // PAGE
        for t in range(t_last + 1):
            steps.append((i, t, int(t == t_last)))
    steps = np.asarray(steps, np.int32)
    n_steps = steps.shape[0]
    tail_tile = tiles_per_seq if has_tail else None
    if not has_tail:
        tail_kv = jnp.zeros((1, PAGE, 512), F32)
        tail_ns = jnp.zeros((1, PAGE, 256), F32)

    def q_map(b, s, si, st, sl, pg):
        return (q_blk0 + b * n_chunks + si[s], 0)

    def o_map(b, s, si, st, sl, pg):
        return (b * n_chunks + si[s], 0)

    def page_of(b, s, st, pg):
        return pg[b * tiles_per_seq + jnp.minimum(st[s], tiles_per_seq - 1)]

    def tail_map(b):
        return b if has_tail else 0

    gs = pltpu.PrefetchScalarGridSpec(
        num_scalar_prefetch=4,
        grid=(n_seq, n_steps),
        in_specs=[
            pl.BlockSpec((Q, 256), q_map),
            pl.BlockSpec((Q, 256), q_map),
            pl.BlockSpec((Q, 128), q_map),
            pl.BlockSpec((None, PAGE, 512), lambda b, s, si, st, sl, pg: (page_of(b, s, st, pg), 0, 0)),
            pl.BlockSpec((None, PAGE, 128), lambda b, s, si, st, sl, pg: (page_of(b, s, st, pg), 0, 1)),
            pl.BlockSpec((None, PAGE, 512), lambda b, s, si, st, sl, pg: (tail_map(b), 0, 0)),
            pl.BlockSpec((None, PAGE, 128), lambda b, s, si, st, sl, pg: (tail_map(b), 0, 1)),
            pl.BlockSpec((None, PAGE, 128),
                         lambda b, s, si, st, sl, pg: (b * n_win_tiles + jnp.clip(st[s] - win_w0t, 0, n_win_tiles - 1), 0, 0)),
            pl.BlockSpec((None, nb, 256), lambda b, s, si, st, sl, pg: (b, 0, 0)),
            pl.BlockSpec((None, nc, 128), lambda b, s, si, st, sl, pg: (b, 0, 0)),
            pl.BlockSpec((MOBA_HEADS, 3, Q, PAGE), lambda b, s, si, st, sl, pg: (0, 0, 0, 0)),
            pl.BlockSpec((3, NSA_HEADS * Q, PAGE), lambda b, s, si, st, sl, pg: (0, 0, 0)),
            pl.BlockSpec((None, NSA_HEADS * Q, nc), lambda b, s, si, st, sl, pg: (si[s], 0, 0)),
            pl.BlockSpec((nc, nbs_p), lambda b, s, si, st, sl, pg: (0, 0)),
        ],
        out_specs=[pl.BlockSpec((Q, 256), o_map), pl.BlockSpec((Q, 256), o_map)],
        scratch_shapes=[
            pltpu.VMEM((Q, 256), F32),
            pltpu.VMEM((NSA_HEADS * Q, HEAD_DIM), F32),
            pltpu.VMEM((MOBA_HEADS, Q, nb), F32),
            pltpu.VMEM((Q, nbs_p), F32),
            pltpu.VMEM((MOBA_HEADS, Q, 1), F32),
            pltpu.VMEM((MOBA_HEADS, Q, 1), F32),
            pltpu.VMEM((MOBA_HEADS, Q, HEAD_DIM), F32),
            pltpu.VMEM((2, NSA_HEADS * Q, 1), F32),
            pltpu.VMEM((2, NSA_HEADS * Q, 1), F32),
            pltpu.VMEM((2, NSA_HEADS * Q, HEAD_DIM), F32),
            pltpu.VMEM((NSA_HEADS * Q, HEAD_DIM), F32),
        ],
    )
    n_tok = n_seq * n_chunks * Q
    return pl.pallas_call(
        functools.partial(_attn_kernel, q_rows=Q, q0_base=q0_base, tail_tile=tail_tile),
        grid_spec=gs,
        out_shape=[jax.ShapeDtypeStruct((n_tok, 256), F32), jax.ShapeDtypeStruct((n_tok, 256), F32)],
        compiler_params=_cparams(("parallel", "arbitrary")),
        name="attention",
    )(jnp.asarray(steps[:, 0]), jnp.asarray(steps[:, 1]), jnp.asarray(steps[:, 2]), page_idx,
      qm, qn, gn, kv_pages, ns_pages, tail_kv, tail_ns, win_pages, kmean, cmp, tbm, tbn, cb, pool)


def _softplus(x):
    return jnp.maximum(x, 0.0) + jnp.log(1.0 + jnp.exp(-jnp.abs(x)))


def _ssd_kernel(xbc_ref, z_ref, dt_ref, dtt_ref, cprev_ref, st0_ref, cw_ref, cb_ref, dtb_ref, dtbt_ref,
                al_ref, alt_ref, dd_ref, nw_ref, y_ref, st_ref, xp, *, valid_len):
    c = pl.program_id(1)
    cl = xbc_ref.shape[0]
    gn = SSD_GROUPS * SSD_STATE

    @pl.when(c == 0)
    def _first():
        xp[0:8, :] = cprev_ref[...]
        st_ref[...] = st0_ref[...]

    xp[8:8 + cl, :] = xbc_ref[...]
    conv = cb_ref[...] + cw_ref[0:1, :] * xp[5:5 + cl, :]
    for k in range(1, SSD_CONV):
        conv = conv + cw_ref[k:k + 1, :] * xp[5 + k:5 + k + cl, :]
    xp[0:8, :] = xp[cl:cl + 8, :]
    act = _silu(conv)
    xs = act[:, :SSD_INNER]
    bm = act[:, SSD_INNER:SSD_INNER + gn]
    cm = act[:, SSD_INNER + gn:]

    row = lax.broadcasted_iota(jnp.int32, (cl, cl), 0)
    col = lax.broadcasted_iota(jnp.int32, (cl, cl), 1)
    tri = row >= col
    pos_r = c * cl + lax.broadcasted_iota(jnp.int32, (cl, 1), 0)
    pos_c = c * cl + lax.broadcasted_iota(jnp.int32, (1, cl), 1)
    dt = jnp.where(pos_r < valid_len, _softplus(dt_ref[...] + dtb_ref[...]), 0.0)
    dtt = jnp.where(pos_c < valid_len, _softplus(dtt_ref[...] + dtbt_ref[...]), 0.0)
    da = dt * (-jnp.exp(al_ref[...]))
    dat = dtt * (-jnp.exp(alt_ref[...]))
    acum = jnp.dot(jnp.where(tri, 1.0, 0.0), da, precision=HI, preferred_element_type=F32)
    acumt = jnp.dot(dat, jnp.where(row <= col, 1.0, 0.0), precision=HI, preferred_element_type=F32)

    cbs = [_dot_nt(cm[:, SSD_STATE * g:SSD_STATE * (g + 1)], bm[:, SSD_STATE * g:SSD_STATE * (g + 1)])
           for g in range(SSD_GROUPS)]
    ys = []
    for h in range(SSD_HEADS):
        g = h // (SSD_HEADS // SSD_GROUPS)
        bg = bm[:, SSD_STATE * g:SSD_STATE * (g + 1)]
        cg = cm[:, SSD_STATE * g:SSD_STATE * (g + 1)]
        a_col = acum[:, h:h + 1]
        a_row = acumt[h:h + 1, :]
        a_last = acumt[h:h + 1, cl - 1:cl]
        lmat = jnp.exp(jnp.where(tri, a_col - a_row, -jnp.inf))
        xh = xs[:, SSD_HEAD_DIM * h:SSD_HEAD_DIM * (h + 1)]
        xdt = xh * dt[:, h:h + 1]
        y_diag = _dot(cbs[g] * lmat, xdt)
        prev = st_ref[h]
        y_off = jnp.exp(a_col) * _dot_nt(cg, prev)
        decay = jnp.exp(a_last - a_col)
        upd = lax.dot_general(xdt.astype(BF16), (bg * decay).astype(BF16), (((0,), (0,)), ((), ())),
                              preferred_element_type=F32)
        st_ref[h] = prev * jnp.exp(a_last) + upd
        ys.append(y_diag + y_off + dd_ref[:, h:h + 1] * xh)
    y = jnp.concatenate(ys, axis=1) * _silu(z_ref[...])
    half = SSD_INNER // SSD_GROUPS
    outs = []
    for g in range(SSD_GROUPS):
        yg = y[:, half * g:half * (g + 1)]
        outs.append(yg * lax.rsqrt(jnp.mean(yg * yg, axis=-1, keepdims=True) + EPS))
    y_ref[...] = jnp.concatenate(outs, axis=1) * nw_ref[...]


def ssd(xbc, z, dt, dt_t, conv_prev8, state0, cw, cb, dtb, alog, dd, nw, *, n_seq, seq_rows, chunk, valid_len):
    n_ch = seq_rows // chunk

    def pad128(v):
        return jnp.pad(v.reshape(1, -1), ((0, 0), (0, 128 - v.shape[-1])))

    tok = lambda w: pl.BlockSpec((chunk, w), lambda b, c: (b * n_ch + c, 0))
    full = lambda shp: pl.BlockSpec(shp, lambda b, c: (0,) * len(shp))
    return pl.pallas_call(
        functools.partial(_ssd_kernel, valid_len=valid_len),
        grid=(n_seq, n_ch),
        in_specs=[tok(SSD_CONV_DIM), tok(SSD_INNER), tok(128),
                  pl.BlockSpec((None, SSD_HEADS, chunk), lambda b, c: (b, 0, c)),
                  pl.BlockSpec((None, 8, SSD_CONV_DIM), lambda b, c: (b, 0, 0)),
                  pl.BlockSpec((None, SSD_HEADS, SSD_HEAD_DIM, SSD_STATE), lambda b, c: (b, 0, 0, 0)),
                  full((SSD_CONV, SSD_CONV_DIM)), full((1, SSD_CONV_DIM)), full((1, 128)), full((SSD_HEADS, 1)),
                  full((1, 128)), full((SSD_HEADS, 1)), full((1, 128)), full((1, SSD_INNER))],
        out_specs=[tok(SSD_INNER),
                   pl.BlockSpec((None, SSD_HEADS, SSD_HEAD_DIM, SSD_STATE), lambda b, c: (b, 0, 0, 0))],
        out_shape=[jax.ShapeDtypeStruct((n_seq * seq_rows, SSD_INNER), F32),
                   jax.ShapeDtypeStruct((n_seq, SSD_HEADS, SSD_HEAD_DIM, SSD_STATE), F32)],
        scratch_shapes=[pltpu.VMEM((chunk + 8, SSD_CONV_DIM), F32)],
        compiler_params=_cparams(("parallel", "arbitrary")),
        name="ssd",
    )(xbc, z, dt, dt_t, conv_prev8, state0, cw, cb.reshape(1, -1), pad128(dtb), dtb.reshape(-1, 1),
      pad128(alog), alog.reshape(-1, 1), pad128(dd), nw.reshape(1, -1))


def _mix_kernel(om_ref, on_ref, ys_ref, gbr_ref, x_ref, g1_ref, sc2_ref, sh2_ref, nw1_ref, nw2_ref,
                wbm_ref, wbn_ref, wbs_ref, wo_ref, rwt_ref, x1_ref, h2_ref, lg_ref):
    d = D_MODEL
    ya = _dot(om_ref[...], wbm_ref[...])
    yb = _dot(on_ref[...], wbn_ref[...])
    yc = _dot(ys_ref[...], wbs_ref[...])
    merged = gbr_ref[:, 0:d] * ya + gbr_ref[:, d:2 * d] * yb + gbr_ref[:, 2 * d:3 * d] * yc
    m = _dot(merged, wo_ref[...])
    x1 = x_ref[...] + g1_ref[...] * _rms(m, nw1_ref[...])
    h2 = _rms(x1, nw2_ref[...]) * (1.0 + sc2_ref[...]) + sh2_ref[...]
    x1_ref[...] = x1
    h2_ref[...] = h2
    lg_ref[...] = _dot_nt(rwt_ref[...], h2, precision=HI)


def mix_out(om, on, ys, gbr, x, modx, nw1, nw2, wbm, wbn, wbs, wo, rwt, group_of_tile):
    n, d = x.shape
    tok = lambda w: pl.BlockSpec((TM, w), lambda i: (i, 0))
    full = lambda a: pl.BlockSpec(a.shape, lambda i: (0,) * a.ndim, pipeline_mode=pl.Buffered(1))
    return pl.pallas_call(
        _mix_kernel,
        grid=(n // TM,),
        in_specs=[tok(256), tok(256), tok(SSD_INNER), tok(3 * d), tok(d),
                  _mod_spec(2, group_of_tile), _mod_spec(4, group_of_tile), _mod_spec(3, group_of_tile),
                  full(nw1), full(nw2), full(wbm), full(wbn), full(wbs), full(wo), full(rwt)],
        out_specs=[tok(d), tok(d), pl.BlockSpec((N_EXPERTS, TM), lambda i: (0, i))],
        out_shape=[jax.ShapeDtypeStruct((n, d), F32), jax.ShapeDtypeStruct((n, d), F32),
                   jax.ShapeDtypeStruct((N_EXPERTS, n), F32)],
        compiler_params=_cparams(("parallel",)),
        name="mix_out",
    )(om, on, ys, gbr, x, modx, modx, modx, nw1, nw2, wbm, wbn, wbs, wo, rwt)


def _router_kernel(lg_ref, rb_ref, eidx_ref, w8_ref, pos_ref, cnt_ref, carry):
    i = pl.program_id(0)
    tm = lg_ref.shape[1]
    per = N_EXPERTS // N_ROUTE_GROUPS

    @pl.when(i == 0)
    def _zero():
        carry[...] = jnp.zeros(carry.shape, F32)

    s = _sigmoid(lg_ref[...])
    sc = s + rb_ref[...]
    sub = lax.broadcasted_iota(jnp.int32, (per, tm), 0).astype(F32)
    gs_rows = []
    for g in range(N_ROUTE_GROUPS):
        x = sc[per * g:per * (g + 1), :]
        m1 = jnp.max(x, axis=0, keepdims=True)
        i1 = jnp.min(jnp.where(x == m1, sub, float(per)), axis=0, keepdims=True)
        m2 = jnp.max(jnp.where(sub == i1, -jnp.inf, x), axis=0, keepdims=True)
        gs_rows.append(m1 + m2)
    gs = jnp.concatenate(gs_rows, axis=0)
    gsel = jnp.zeros_like(gs)
    for _ in range(TOPK_GROUPS):
        m = jnp.max(gs, axis=0, keepdims=True)
        ix = jnp.min(jnp.where(gs == m, sub, float(N_ROUTE_GROUPS)), axis=0, keepdims=True)
        hit = sub == ix
        gsel = jnp.where(hit, 1.0, gsel)
        gs = jnp.where(hit, -jnp.inf, gs)
    emask = jnp.concatenate([jnp.broadcast_to(gsel[g:g + 1, :], (per, tm)) for g in range(N_ROUTE_GROUPS)], axis=0)
    msc = jnp.where(emask > 0.0, sc, -jnp.inf)
    e_io = lax.broadcasted_iota(jnp.int32, (N_EXPERTS, tm), 0).astype(F32)
    sel = jnp.zeros_like(sc)
    idxs = []
    for _ in range(TOP_K):
        m = jnp.max(msc, axis=0, keepdims=True)
        ix = jnp.min(jnp.where(msc == m, e_io, float(N_EXPERTS)), axis=0, keepdims=True)
        hit = e_io == ix
        sel = jnp.where(hit, 1.0, sel)
        msc = jnp.where(hit, -jnp.inf, msc)
        idxs.append(ix)
    w = s * sel
    wn = w / jnp.sum(w, axis=0, keepdims=True) * ROUTE_SCALE
    r = lax.broadcasted_iota(jnp.int32, (tm, tm), 0)
    cidx = lax.broadcasted_iota(jnp.int32, (tm, tm), 1)
    upper = jnp.where(r <= cidx, 1.0, 0.0)
    cum = _dot(sel, upper)
    rank = cum - sel + carry[:, 0:1]
    eidx_rows, w_rows, p_rows = [], [], []
    for k in range(TOP_K):
        hit = e_io == idxs[k]
        eidx_rows.append(idxs[k])
        w_rows.append(jnp.sum(jnp.where(hit, wn, 0.0), axis=0, keepdims=True))
        p_rows.append(jnp.sum(jnp.where(hit, rank, 0.0), axis=0, keepdims=True))
    eidx_ref[...] = jnp.concatenate(eidx_rows, axis=0).astype(jnp.int32)
    w8_ref[...] = jnp.concatenate(w_rows, axis=0)
    pos_ref[...] = jnp.concatenate(p_rows, axis=0).astype(jnp.int32)
    carry[...] = carry[...] + jnp.sum(sel, axis=1, keepdims=True)
    cnt_ref[...] = carry[...]


def router(logits_t, router_b):
    ne, n = logits_t.shape
    return pl.pallas_call(
        _router_kernel,
        grid=(n // TM,),
        in_specs=[pl.BlockSpec((ne, TM), lambda i: (0, i)), pl.BlockSpec((ne, 1), lambda i: (0, 0))],
        out_specs=[pl.BlockSpec((TOP_K, TM), lambda i: (0, i)), pl.BlockSpec((TOP_K, TM), lambda i: (0, i)),
                   pl.BlockSpec((TOP_K, TM), lambda i: (0, i)), pl.BlockSpec((ne, 128), lambda i: (0, 0))],
        out_shape=[jax.ShapeDtypeStruct((TOP_K, n), jnp.int32), jax.ShapeDtypeStruct((TOP_K, n), F32),
                   jax.ShapeDtypeStruct((TOP_K, n), jnp.int32), jax.ShapeDtypeStruct((ne, 128), F32)],
        scratch_shapes=[pltpu.VMEM((ne, 128), F32)],
        compiler_params=_cparams(("arbitrary",)),
        name="router",
    )(logits_t, router_b.reshape(ne, 1))


def _dispatch_kernel(dest_ref, h_ref, init_ref, rows_ref, sem):
    del init_ref
    n_pairs = dest_ref.shape[1]

    def copy(j):
        tok = lax.shift_right_logical(j, 3)
        return pltpu.make_async_copy(h_ref.at[pl.ds(tok, 1)], rows_ref.at[pl.ds(dest_ref[0, j], 1)], sem)

    def start(j, carry):
        copy(j).start()
        return carry

    def wait(j, carry):
        copy(j).wait()
        return carry

    lax.fori_loop(0, n_pairs, start, 0)
    lax.fori_loop(0, n_pairs, wait, 0)


def dispatch(h2, dest, n_rows):
    n, d = h2.shape
    n_tiles = n // TM
    rows0 = jnp.zeros((n_rows, d), F32)
    return pl.pallas_call(
        _dispatch_kernel,
        grid=(n_tiles,),
        in_specs=[pl.BlockSpec((None, 1, TM * TOP_K), lambda i: (i, 0, 0), memory_space=pltpu.SMEM),
                  pl.BlockSpec((TM, d), lambda i: (i, 0)),
                  pl.BlockSpec(memory_space=pl.ANY)],
        out_specs=pl.BlockSpec(memory_space=pl.ANY),
        out_shape=jax.ShapeDtypeStruct((n_rows, d), F32),
        scratch_shapes=[pltpu.SemaphoreType.DMA(())],
        input_output_aliases={2: 0},
        compiler_params=_cparams(("arbitrary",)),
        name="dispatch",
    )(dest.reshape(n_tiles, 1, TM * TOP_K), h2, rows0)


def _expert_kernel(be_ref, nu_ref, x_ref, wg_ref, wu_ref, wd_ref, y_ref):
    @pl.when(pl.program_id(0) < nu_ref[0])
    def _():
        x = x_ref[...].astype(BF16)
        g = jnp.dot(x, wg_ref[...].astype(BF16), preferred_element_type=F32)
        u = jnp.dot(x, wu_ref[...].astype(BF16), preferred_element_type=F32)
        y_ref[...] = _dot(_silu(g) * u, wd_ref[...])

    @pl.when(pl.program_id(0) >= nu_ref[0])
    def _():
        y_ref[...] = jnp.zeros(y_ref.shape, F32)


def experts(x_rows, blk_e, n_used, wg, wu, wd, layer):
    n_rows, d = x_rows.shape
    n_blk = n_rows // EXPERT_ROWS
    ne = wg.shape[1]
    de = wg.shape[-1]
    wg2 = wg.reshape(-1, d, de)
    wu2 = wu.reshape(-1, d, de)
    wd2 = wd.reshape(-1, de, d)
    gs = pltpu.PrefetchScalarGridSpec(
        num_scalar_prefetch=2,
        grid=(n_blk,),
        in_specs=[pl.BlockSpec((EXPERT_ROWS, d), lambda i, be, nu: (i, 0)),
                  pl.BlockSpec((None, d, de), lambda i, be, nu: (layer * ne + be[i], 0, 0)),
                  pl.BlockSpec((None, d, de), lambda i, be, nu: (layer * ne + be[i], 0, 0)),
                  pl.BlockSpec((None, de, d), lambda i, be, nu: (layer * ne + be[i], 0, 0))],
        out_specs=pl.BlockSpec((EXPERT_ROWS, d), lambda i, be, nu: (i, 0)),
    )
    return pl.pallas_call(
        _expert_kernel,
        grid_spec=gs,
        out_shape=jax.ShapeDtypeStruct((n_rows, d), F32),
        compiler_params=_cparams(("arbitrary",)),
        name="experts",
    )(blk_e, n_used, x_rows, wg2, wu2, wd2)


def _combine_kernel(dest_ref, w8_ref, x1_ref, h2_ref, g2_ref, nw_ref, wsg_ref, wsu_ref, wsd_ref, yrows_ref,
                    out_ref, buf, sem):
    n_pairs = dest_ref.shape[1]

    def copy(j):
        tok = lax.shift_right_logical(j, 3)
        k = jnp.bitwise_and(j, TOP_K - 1)
        return pltpu.make_async_copy(yrows_ref.at[pl.ds(dest_ref[0, j], 1)], buf.at[k, pl.ds(tok, 1)], sem)

    def start(j, carry):
        copy(j).start()
        return carry

    def wait(j, carry):
        copy(j).wait()
        return carry

    lax.fori_loop(0, n_pairs, start, 0)
    h2 = h2_ref[...]
    shared = _dot(_silu(_dot(h2, wsg_ref[...])) * _dot(h2, wsu_ref[...]), wsd_ref[...])
    lax.fori_loop(0, n_pairs, wait, 0)
    w8 = w8_ref[...]
    routed = w8[:, 0:1] * buf[0]
    for k in range(1, TOP_K):
        routed = routed + w8[:, k:k + 1] * buf[k]
    out_ref[...] = x1_ref[...] + g2_ref[...] * _rms(routed + shared, nw_ref[...])


def combine(dest, w8, x1, h2, modx, nw3, wsg, wsu, wsd, y_rows, group_of_tile):
    n, d = x1.shape
    n_tiles = n // TM
    tok = lambda w: pl.BlockSpec((TM, w), lambda i: (i, 0))
    full = lambda a: pl.BlockSpec(a.shape, lambda i: (0,) * a.ndim)
    return pl.pallas_call(
        _combine_kernel,
        grid=(n_tiles,),
        in_specs=[pl.BlockSpec((None, 1, TM * TOP_K), lambda i: (i, 0, 0), memory_space=pltpu.SMEM),
                  tok(TOP_K), tok(d), tok(d), _mod_spec(5, group_of_tile),
                  full(nw3), full(wsg), full(wsu), full(wsd),
                  pl.BlockSpec(memory_space=pl.ANY)],
        out_specs=tok(d),
        out_shape=jax.ShapeDtypeStruct((n, d), F32),
        scratch_shapes=[pltpu.VMEM((TOP_K, TM, d), F32), pltpu.SemaphoreType.DMA(())],
        compiler_params=_cparams(("arbitrary",)),
        name="combine",
    )(dest.reshape(n_tiles, 1, TM * TOP_K), w8, x1, h2, modx, nw3, wsg, wsu, wsd, y_rows)


def _pack_w_in(w):
    d = w.shape[0]
    z = lambda n: jnp.zeros((d, n), w.dtype)
    parts = [w[:, 0:1408], w[:, 1408:1420], z(116), w[:, 1420:1932], w[:, 1932:2956], w[:, 2956:2964], z(120),
             w[:, 2964:6036]]
    return jnp.concatenate(parts, axis=1).astype(BF16)


def _expand_w1(w1):
    half = NSA_CMP_STRIDE * HEAD_DIM
    w = jnp.zeros((NSA_CMP_STRIDE, 4, HEAD_DIM, 256), w1.dtype)
    for kv in range(2):
        w = w.at[:, kv, :, 128 * kv:128 * kv + 64].set(w1[kv, :half].reshape(NSA_CMP_STRIDE, HEAD_DIM, HEAD_DIM))
        w = w.at[:, kv, :, 128 * kv + 64:128 * kv + 128].set(w1[kv, half:].reshape(NSA_CMP_STRIDE, HEAD_DIM, HEAD_DIM))
    return w.reshape(NSA_CMP_STRIDE * 256, 256).astype(BF16)


def kernel(x_prompt, x_sample, c_prompt, c_sample, cache_moba_kv, cache_nsa_kv, cache_nsa_win, state_ssd_conv, state_ssd, page_table, rel_bias, ada_w, ada_b, norm_w, w_in, nsa_cmp_w1, nsa_cmp_b1, nsa_cmp_w2, nsa_cmp_b2, nsa_cmp_pos, ssd_conv_w, ssd_conv_b, ssd_dt_bias, ssd_a_log, ssd_d, ssd_norm_w, w_branch_moba, w_branch_nsa, w_branch_ssd, w_out, router_w, router_b, exp_w_gate, exp_w_up, exp_w_down, shared_w_gate, shared_w_up, shared_w_down):
    depth = w_in.shape[0]
    bp, lp, d = x_prompt.shape
    bs, ls, _ = x_sample.shape
    n_p = bp * lp
    n_s = bs * ls
    n = n_p + n_s
    assert n_s == TM and lp % TM == 0 and lp % SSD_CHUNK == 0
    n_pages = page_table.shape[1]
    past = n_pages * PAGE
    n_pool = cache_moba_kv.shape[1]
    tiles_p = lp // TM
    n_ptiles = n_p // TM

    def group_of_tile(i):
        return jnp.where(i < n_ptiles, i // tiles_p, bp)

    x = jnp.concatenate([x_prompt.reshape(n_p, d), x_sample.reshape(n_s, d)], axis=0)
    c_all = jnp.concatenate([c_prompt, c_sample, jnp.zeros((4, d), F32)], axis=0)
    kv_cache = cache_moba_kv.reshape(depth * n_pool, PAGE, 512)
    ns_cache = cache_nsa_kv.reshape(depth * n_pool, PAGE, 256)
    pt_flat = page_table.reshape(-1).astype(jnp.int32)
    prompt_pages = jnp.arange(bp * (lp // PAGE), dtype=jnp.int32)

    q_chunk = 128
    tables_p = attention_tables(rel_bias, q_chunk, lp // q_chunk, 0, lp // NSA_CMP_STRIDE)
    tables_s = attention_tables(rel_bias, ls, 1, past, past // NSA_CMP_STRIDE)

    n_blk = -(-(n * TOP_K + N_EXPERTS * (EXPERT_ROWS - 1)) // EXPERT_ROWS)
    n_rows = n_blk * EXPERT_ROWS

    sample_pad = 128
    outs = [[] for _ in range(10)]
    for l in range(depth):
        mod = modulation(c_all, ada_w, ada_b, l)
        modx = jnp.concatenate([jnp.broadcast_to(mod[:bp, None, :], (bp, TM, 6 * d)),
                                jnp.repeat(mod[bp:bp + bs], ls, axis=0)[None]], axis=0)
        nw = norm_w[l]
        qm, kvm, qn, nsa, win, gn, z, xbc, dt, gbr = proj_in(x, modx, nw[0:1], _pack_w_in(w_in[l]), group_of_tile)
        w_exp = _expand_w1(nsa_cmp_w1[l])

        kvm_pages = kvm.reshape(n // PAGE, PAGE, 512)
        nsa_pages = nsa.reshape(n // PAGE, PAGE, 256)
        win_pages = win.reshape(n // PAGE, PAGE, 128)
        ks_p, pab_p = page_ctx(kvm_pages, nsa_pages, w_exp, 0, n_p // PAGE)
        km_p, cmp_p = ctx_final(ks_p.reshape(bp, lp // PAGE, 256), pab_p.reshape(bp, lp // NSA_CMP_STRIDE, 256),
                                nsa_cmp_pos, nsa_cmp_w1, nsa_cmp_b1, nsa_cmp_w2, nsa_cmp_b2, l)
        om_p, on_p = attention(qm, qn, gn, 0, kvm_pages, nsa_pages, prompt_pages, None, None,
                               win_pages, 0, km_p, cmp_p, tables_p,
                               n_seq=bp, q_rows=q_chunk, n_chunks=lp // q_chunk, q0_base=0,
                               tiles_per_seq=lp // PAGE, n_win_tiles=lp // PAGE, n_sel_blocks=lp // NSA_SEL_BLOCK)

        ks_c, pab_c = page_ctx(kv_cache, ns_cache, w_exp, l * n_pool, n_pool)
        ks_s, pab_s = ctx_gather(pt_flat, ks_c, pab_c, bs, n_pages)
        km_s, cmp_s = ctx_final(ks_s, pab_s, nsa_cmp_pos, nsa_cmp_w1, nsa_cmp_b1, nsa_cmp_w2, nsa_cmp_b2, l)
        pad_rows = lambda a: jnp.pad(a[n_p:].reshape(bs, ls, -1), ((0, 0), (0, PAGE - ls), (0, 0)))
        tail_kv = pad_rows(kvm)
        tail_ns = pad_rows(nsa)
        win_s = jnp.concatenate([cache_nsa_win[l].reshape(bs, -1, 128), pad_rows(win)], axis=1)
        win_s = win_s.reshape(bs * (win_s.shape[1] // PAGE), PAGE, 128)
        lf_s = -(-(past + ls) // MOBA_BLOCK) * MOBA_BLOCK
        om_s, on_s = attention(qm, qn, gn, n_p // ls, kv_cache, ns_cache, pt_flat + l * n_pool, tail_kv, tail_ns,
                               win_s, (past - NSA_WINDOW) // PAGE, km_s, cmp_s, tables_s,
                               n_seq=bs, q_rows=ls, n_chunks=1, q0_base=past,
                               tiles_per_seq=n_pages, n_win_tiles=win_s.shape[0] // bs,
                               n_sel_blocks=lf_s // NSA_SEL_BLOCK)
        om = jnp.concatenate([om_p, om_s], axis=0)
        on = jnp.concatenate([on_p, on_s], axis=0)

        dt_t = dt[:, :SSD_HEADS].T
        ssd_par = (ssd_conv_w[l], ssd_conv_b[l], ssd_dt_bias[l], ssd_a_log[l], ssd_d[l], ssd_norm_w[l])
        y_p, st_p = ssd(xbc, z, dt, dt_t[:, :n_p].reshape(SSD_HEADS, bp, lp).transpose(1, 0, 2),
                        jnp.zeros((bp, 8, SSD_CONV_DIM), F32),
                        jnp.zeros((bp, SSD_HEADS, SSD_HEAD_DIM, SSD_STATE), F32), *ssd_par,
                        n_seq=bp, seq_rows=lp, chunk=SSD_CHUNK, valid_len=lp)
        pad_s = lambda a: jnp.pad(a[n_p:].reshape(bs, ls, -1), ((0, 0), (0, sample_pad - ls), (0, 0))).reshape(bs * sample_pad, -1)
        dt_t_s = jnp.pad(dt_t[:, n_p:].reshape(SSD_HEADS, bs, ls).transpose(1, 0, 2), ((0, 0), (0, 0), (0, sample_pad - ls)))
        conv_prev = jnp.pad(state_ssd_conv[l], ((0, 0), (8 - (SSD_CONV - 1), 0), (0, 0)))
        y_s, st_s = ssd(pad_s(xbc), pad_s(z), pad_s(dt), dt_t_s, conv_prev, state_ssd[l], *ssd_par,
                        n_seq=bs, seq_rows=sample_pad, chunk=sample_pad, valid_len=ls)
        ys = jnp.concatenate([y_p[:n_p], y_s.reshape(bs, sample_pad, -1)[:, :ls].reshape(n_s, -1)], axis=0)

        x1, h2, logits_t = mix_out(om, on, ys, gbr, x, modx, nw[1:2], nw[2:3],
                                   w_branch_moba[l].astype(BF16), w_branch_nsa[l].astype(BF16),
                                   w_branch_ssd[l].astype(BF16), w_out[l].astype(BF16), router_w[l].T, group_of_tile)
        eidx, w8, pos8, cnt = router(logits_t, router_b[l])
        cnt = cnt[:, 0].astype(jnp.int32)
        padded = (cnt + EXPERT_ROWS - 1) // EXPERT_ROWS * EXPERT_ROWS
        ends = jnp.cumsum(padded)
        off = ends - padded
        dest = (off[eidx] + pos8).T.reshape(-1)
        blk_e = jnp.minimum(jnp.searchsorted(ends, jnp.arange(n_blk, dtype=jnp.int32) * EXPERT_ROWS, side="right"),
                            N_EXPERTS - 1).astype(jnp.int32)
        n_used = (ends[-1] // EXPERT_ROWS).astype(jnp.int32).reshape(1)
        x_rows = dispatch(h2, dest, n_rows)
        y_rows = experts(x_rows, blk_e, n_used, exp_w_gate, exp_w_up, exp_w_down, l)
        x = combine(dest, w8.T, x1, h2, modx, nw[3:4], shared_w_gate[l].astype(BF16), shared_w_up[l].astype(BF16),
                    shared_w_down[l].astype(BF16), y_rows, group_of_tile)

        keep = min(NSA_WINDOW, lp)
        outs[0].append(kvm[:n_p].reshape(bp, lp, 2, MOBA_HEADS, HEAD_DIM))
        outs[1].append(kvm[n_p:].reshape(bs, ls, 2, MOBA_HEADS, HEAD_DIM))
        outs[2].append(nsa[:n_p].reshape(bp, lp, 4, HEAD_DIM))
        outs[3].append(nsa[n_p:].reshape(bs, ls, 4, HEAD_DIM))
        outs[4].append(win[:n_p].reshape(bp, lp, 2, HEAD_DIM)[:, lp - keep:])
        outs[5].append(win[n_p:].reshape(bs, ls, 2, HEAD_DIM))
        outs[6].append(xbc[:n_p].reshape(bp, lp, -1)[:, lp - (SSD_CONV - 1):])
        outs[7].append(xbc[n_p:].reshape(bs, ls, -1)[:, ls - (SSD_CONV - 1):])
        outs[8].append(st_p)
        outs[9].append(st_s)

    y_prompt = x[:n_p].reshape(bp, lp, d)
    y_sample = x[n_p:].reshape(bs, ls, d)
    return (y_prompt, y_sample) + tuple(jnp.stack(o) for o in outs)
```

```python
import functools
import math

import numpy as np
import jax
import jax.numpy as jnp
from jax import lax
from jax.experimental import pallas as pl
from jax.experimental.pallas import tpu as pltpu

F32 = jnp.float32
BF16 = jnp.bfloat16
HI = lax.Precision.HIGHEST

D_MODEL = 1024
PAGE = 128
HEAD_DIM = 64
MOBA_HEADS = 4
MOBA_BLOCK = 256
MOBA_TOPK = 3
NSA_HEADS = 4
NSA_CMP_STRIDE = 16
NSA_SEL_BLOCK = 64
NSA_TOPN = 16
NSA_WINDOW = 512
SSD_HEADS = 8
SSD_HEAD_DIM = 64
SSD_INNER = 512
SSD_GROUPS = 2
SSD_STATE = 128
SSD_CONV = 4
SSD_CHUNK = 256
SSD_CONV_DIM = 1024
N_BUCKETS = 32
MAX_DISTANCE = 128
N_EXPERTS = 64
TOP_K = 8
N_ROUTE_GROUPS = 8
TOPK_GROUPS = 4
D_EXPERT = 256
ROUTE_SCALE = 2.5
EPS = 1e-6

TM = 256
ROW_TILES = D_MODEL // 128
EXPERT_ROWS = 256
VMEM_LIMIT = 56 * 1024 * 1024
NEG = -1e30

_SEG = dict(qm=(0, 256), kvm=(256, 512), qn=(768, 256), nsa=(1024, 256), win=(1280, 128),
            gn=(1408, 128), z=(1536, 512), xbc=(2048, 1024), dt=(3072, 128), gbr=(3200, 3072))
PROJ_W = 6272


def _sigmoid(x):
    return 1.0 / (1.0 + jnp.exp(-x))


def _silu(x):
    return x * _sigmoid(x)


def _rms(x, w):
    return x * lax.rsqrt(jnp.mean(x * x, axis=-1, keepdims=True) + EPS) * w


def _dot(a, b):
    return jnp.dot(a.astype(BF16), b.astype(BF16), preferred_element_type=F32)


def _dot_nt(a, b, precision=None):
    if precision is None:
        a, b = a.astype(BF16), b.astype(BF16)
    return lax.dot_general(a, b, (((1,), (1,)), ((), ())), precision=precision,
                           preferred_element_type=F32)


def _cparams(sem, vmem=None):
    return pltpu.CompilerParams(dimension_semantics=sem, vmem_limit_bytes=vmem or VMEM_LIMIT)


def _mod_kernel(c_ref, w_ref, b_ref, o_ref):
    o_ref[...] = _dot(_silu(c_ref[...]), w_ref[...]) + b_ref[...]


def modulation(c_all, ada_w, ada_b, layer):
    rows, d = c_all.shape
    n = ada_w.shape[-1]
    tn = 512
    return pl.pallas_call(
        _mod_kernel,
        grid=(n // tn,),
        in_specs=[pl.BlockSpec((rows, d), lambda j: (0, 0)),
                  pl.BlockSpec((None, d, tn), lambda j: (layer, 0, j)),
                  pl.BlockSpec((None, 1, tn), lambda j: (layer, 0, j))],
        out_specs=pl.BlockSpec((rows, tn), lambda j: (0, j)),
        out_shape=jax.ShapeDtypeStruct((rows, n), F32),
        compiler_params=_cparams(("parallel",)),
        name="modulation",
    )(c_all, ada_w, ada_b.reshape(ada_b.shape[0], 1, n))


def _proj_kernel(x_ref, sh_ref, sc_ref, nw_ref, w_ref, qm_ref, kvm_ref, qn_ref, nsa_ref, win_ref,
                 gn_ref, z_ref, xbc_ref, dt_ref, gbr_ref):
    h = _rms(x_ref[...], nw_ref[...]) * (1.0 + sc_ref[...]) + sh_ref[...]
    hb = h.astype(BF16)

    def seg(name):
        o, w = _SEG[name]
        return jnp.dot(hb, w_ref[:, o:o + w], preferred_element_type=F32)

    qm_ref[...] = seg("qm")
    kvm_ref[...] = seg("kvm")
    qn_ref[...] = seg("qn")
    nsa_ref[...] = seg("nsa")
    win_ref[...] = seg("win")
    gn_ref[...] = _sigmoid(seg("gn"))
    z_ref[...] = seg("z")
    xbc_ref[...] = seg("xbc")
    dt_ref[...] = seg("dt")
    gbr_ref[...] = _sigmoid(seg("gbr"))


def _mod_spec(which, group_of_tile):
    return pl.BlockSpec((None, TM, D_MODEL), lambda i: (group_of_tile(i), 0, which))


def proj_in(x, modx, nw, w_packed, group_of_tile):
    n, d = x.shape
    names = ["qm", "kvm", "qn", "nsa", "win", "gn", "z", "xbc", "dt", "gbr"]
    return pl.pallas_call(
        _proj_kernel,
        grid=(n // TM,),
        in_specs=[pl.BlockSpec((TM, d), lambda i: (i, 0)),
                  _mod_spec(0, group_of_tile), _mod_spec(1, group_of_tile),
                  pl.BlockSpec((1, d), lambda i: (0, 0)),
                  pl.BlockSpec((d, PROJ_W), lambda i: (0, 0), pipeline_mode=pl.Buffered(1))],
        out_specs=[pl.BlockSpec((TM, _SEG[k][1]), lambda i: (i, 0)) for k in names],
        out_shape=[jax.ShapeDtypeStruct((n, _SEG[k][1]), F32) for k in names],
        compiler_params=_cparams(("parallel",)),
        name="proj_in",
    )(x, modx, modx, nw, w_packed)


CTX_PAGES = 16


def _page_ctx_kernel(k_ref, g_ref, w_ref, ks_ref, pab_ref):
    ks_ref[...] = jnp.sum(k_ref[...], axis=1)
    pab_ref[...] = _dot(g_ref[...], w_ref[...])


def page_ctx(kv_pages, nsa_pages, w_exp, page0, n_pages):
    groups = PAGE // NSA_CMP_STRIDE
    g_view = nsa_pages.reshape(nsa_pages.shape[0] * groups, NSA_CMP_STRIDE * 256)
    blk0 = page0 // CTX_PAGES
    return pl.pallas_call(
        _page_ctx_kernel,
        grid=(n_pages // CTX_PAGES,),
        in_specs=[pl.BlockSpec((CTX_PAGES, PAGE, 256), lambda i: (blk0 + i, 0, 0)),
                  pl.BlockSpec((CTX_PAGES * groups, NSA_CMP_STRIDE * 256), lambda i: (blk0 + i, 0)),
                  pl.BlockSpec((NSA_CMP_STRIDE * 256, 256), lambda i: (0, 0))],
        out_specs=[pl.BlockSpec((CTX_PAGES, 256), lambda i: (i, 0)),
                   pl.BlockSpec((CTX_PAGES * groups, 256), lambda i: (i, 0))],
        out_shape=[jax.ShapeDtypeStruct((n_pages, 256), F32),
                   jax.ShapeDtypeStruct((n_pages * groups, 256), F32)],
        compiler_params=_cparams(("parallel",)),
        name="page_ctx",
    )(kv_pages, g_view, w_exp)


def _page_ctx_cache_kernel(kv_ref, ns_ref, w_ref, ks_ref, pab_ref):
    rows_kv = 2 * MOBA_HEADS
    for p in range(CTX_PAGES):
        page = kv_ref[p * PAGE * rows_kv:(p + 1) * PAGE * rows_kv, :]
        ks_ref[p] = jnp.sum(page.reshape(PAGE, rows_kv, HEAD_DIM), axis=0)
    groups = CTX_PAGES * PAGE // NSA_CMP_STRIDE
    acc = jnp.zeros((groups, 256), F32)
    for r in range(NSA_CMP_STRIDE):
        kc = ns_ref[pl.ds(4 * r, groups, stride=4 * NSA_CMP_STRIDE), :]
        vc = ns_ref[pl.ds(4 * r + 1, groups, stride=4 * NSA_CMP_STRIDE), :]
        acc = acc + _dot(jnp.concatenate([kc, vc], axis=1), w_ref[128 * r:128 * (r + 1), :])
    pab_ref[...] = acc


def page_ctx_cache(kv_rows, ns_rows, w_cmp, page0, n_pages):
    groups = PAGE // NSA_CMP_STRIDE
    blk0 = page0 // CTX_PAGES
    rows_kv = 2 * MOBA_HEADS
    return pl.pallas_call(
        _page_ctx_cache_kernel,
        grid=(n_pages // CTX_PAGES,),
        in_specs=[pl.BlockSpec((CTX_PAGES * PAGE * rows_kv, HEAD_DIM), lambda i: (blk0 + i, 0)),
                  pl.BlockSpec((CTX_PAGES * PAGE * 4, HEAD_DIM), lambda i: (blk0 + i, 0)),
                  pl.BlockSpec(w_cmp.shape, lambda i: (0, 0))],
        out_specs=[pl.BlockSpec((CTX_PAGES, rows_kv, HEAD_DIM), lambda i: (i, 0, 0)),
                   pl.BlockSpec((CTX_PAGES * groups, 256), lambda i: (i, 0))],
        out_shape=[jax.ShapeDtypeStruct((n_pages, rows_kv, HEAD_DIM), F32),
                   jax.ShapeDtypeStruct((n_pages * groups, 256), F32)],
        compiler_params=_cparams(("parallel",)),
        name="page_ctx_cache",
    )(kv_rows, ns_rows, w_cmp)


def _ctx_gather_kernel(pt_ref, ks_hbm, pab_hbm, ks_out, pab_out, sem):
    b = pl.program_id(0)
    n_pages = ks_out.shape[1]
    groups = pab_out.shape[1] // n_pages

    def copies(p):
        page = pt_ref[b * n_pages + p]
        c1 = pltpu.make_async_copy(ks_hbm.at[pl.ds(page, 1)], ks_out.at[0, pl.ds(p, 1)], sem.at[0])
        c2 = pltpu.make_async_copy(pab_hbm.at[pl.ds(page * groups, groups)],
                                   pab_out.at[0, pl.ds(p * groups, groups)], sem.at[1])
        return c1, c2

    def start(p, carry):
        c1, c2 = copies(p)
        c1.start()
        c2.start()
        return carry

    def wait(p, carry):
        c1, c2 = copies(p)
        c1.wait()
        c2.wait()
        return carry

    lax.fori_loop(0, n_pages, start, 0)
    lax.fori_loop(0, n_pages, wait, 0)


def ctx_gather(page_table_flat, ksum, pab, n_seq, n_pages):
    groups = PAGE // NSA_CMP_STRIDE
    gs = pltpu.PrefetchScalarGridSpec(
        num_scalar_prefetch=1,
        grid=(n_seq,),
        in_specs=[pl.BlockSpec(memory_space=pl.ANY), pl.BlockSpec(memory_space=pl.ANY)],
        out_specs=[pl.BlockSpec((1, n_pages) + ksum.shape[1:], lambda b, pt: (b, 0, 0, 0)),
                   pl.BlockSpec((1, n_pages * groups, 256), lambda b, pt: (b, 0, 0))],
        scratch_shapes=[pltpu.SemaphoreType.DMA((2,))],
    )
    return pl.pallas_call(
        _ctx_gather_kernel,
        grid_spec=gs,
        out_shape=[jax.ShapeDtypeStruct((n_seq, n_pages) + ksum.shape[1:], F32),
                   jax.ShapeDtypeStruct((n_seq, n_pages * groups, 256), F32)],
        compiler_params=_cparams(("arbitrary",)),
        name="ctx_gather",
    )(page_table_flat, ksum, pab)


def _gelu_tanh(x):
    return 0.5 * x * (1.0 + jnp.tanh(math.sqrt(2.0 / math.pi) * (x + 0.044715 * (x * x * x))))


def _ctx_final_kernel(ks_ref, pab_ref, pos_ref, w1_ref, b1_ref, w2_ref, b2_ref, km_ref, cmp_ref, *, head_rows):
    nb = km_ref.shape[0]
    n_pages = 2 * nb
    r = lax.broadcasted_iota(jnp.int32, (nb, n_pages), 0)
    c = lax.broadcasted_iota(jnp.int32, (nb, n_pages), 1)
    pair = jnp.where((c == 2 * r) | (c == 2 * r + 1), 1.0, 0.0).astype(F32)
    if head_rows:
        ks = jnp.concatenate([ks_ref[pl.ds(h, n_pages, stride=2 * MOBA_HEADS), :] for h in range(MOBA_HEADS)], axis=1)
    else:
        ks = ks_ref[...]
    km_ref[...] = jnp.dot(pair, ks, precision=HI, preferred_element_type=F32) * (1.0 / MOBA_BLOCK)

    pab = pab_ref[...]
    ng = pab.shape[0]
    row = lax.broadcasted_iota(jnp.int32, (ng, 1), 0)
    outs = []
    for kv in range(2):
        pa = pab[:, 128 * kv:128 * kv + 64]
        pb = pab[:, 128 * kv + 64:128 * kv + 128]
        pb_next = jnp.where(row == ng - 1, 0.0, pltpu.roll(pb, ng - 1, 0))
        pos_term = _dot(pos_ref[kv], w1_ref[kv]) + b1_ref[kv]
        hid = _gelu_tanh(pa + pb_next + pos_term)
        outs.append(_dot(hid, w2_ref[kv]) + b2_ref[kv])
    cmp_ref[...] = jnp.concatenate(outs, axis=1)


def ctx_final(ksum, pab, pos, w1, b1, w2, b2, layer, n_pages):
    nb = ksum.shape[0]
    head_rows = ksum.shape[-1] == HEAD_DIM
    ng = pab.shape[1]
    lf = pos.shape[2] * pos.shape[3]
    pos2 = pos.reshape(pos.shape[0], 2, 1, lf)
    return pl.pallas_call(
        functools.partial(_ctx_final_kernel, head_rows=head_rows),
        grid=(nb,),
        in_specs=[pl.BlockSpec((None,) + ksum.shape[1:], lambda b: (b, 0, 0)),
                  pl.BlockSpec((None, ng, 256), lambda b: (b, 0, 0)),
                  pl.BlockSpec((None, 2, 1, lf), lambda b: (layer, 0, 0, 0)),
                  pl.BlockSpec((None, 2, lf, HEAD_DIM), lambda b: (layer, 0, 0, 0)),
                  pl.BlockSpec((None, 2, 1, HEAD_DIM), lambda b: (layer, 0, 0, 0)),
                  pl.BlockSpec((None, 2, HEAD_DIM, HEAD_DIM), lambda b: (layer, 0, 0, 0)),
                  pl.BlockSpec((None, 2, 1, HEAD_DIM), lambda b: (layer, 0, 0, 0))],
        out_specs=[pl.BlockSpec((None, n_pages // 2, 256), lambda b: (b, 0, 0)),
                   pl.BlockSpec((None, ng, 128), lambda b: (b, 0, 0))],
        out_shape=[jax.ShapeDtypeStruct((nb, n_pages // 2, 256), F32),
                   jax.ShapeDtypeStruct((nb, ng, 128), F32)],
        compiler_params=_cparams(("parallel",)),
        name="ctx_final",
    )(ksum, pab, pos2, w1, b1.reshape(b1.shape[0], 2, 1, HEAD_DIM), w2, b2.reshape(b2.shape[0], 2, 1, HEAD_DIM))


def _topk_lanes(score, k, lane_f, n_lanes):
    sel = jnp.zeros_like(score)
    for _ in range(k):
        m = jnp.max(score, axis=1, keepdims=True)
        idx = jnp.min(jnp.where(score == m, lane_f, float(n_lanes)), axis=1, keepdims=True)
        hit = lane_f == idx
        sel = jnp.where(hit & (m > -jnp.inf), 1.0, sel)
        score = jnp.where(hit, -jnp.inf, score)
    return sel


def _online(m_ref, l_ref, a_ref, idx, s, valid, v):
    s = jnp.where(valid, s, NEG)
    m_old = m_ref[idx]
    m_new = jnp.maximum(m_old, jnp.max(s, axis=1, keepdims=True))
    p = jnp.where(valid, jnp.exp(s - m_new), 0.0)
    alpha = jnp.exp(m_old - m_new)
    l_ref[idx] = alpha * l_ref[idx] + jnp.sum(p, axis=1, keepdims=True)
    a_ref[idx] = alpha * a_ref[idx] + _dot(p, v)
    m_ref[idx] = m_new


def _attn_kernel(si_ref, st_ref, sl_ref, pg_ref,
                 qm_ref, qn_ref, gn_ref, kv_ref, ns_ref, tkv_ref, tns_ref, win_ref, km_ref, cmp_ref,
                 tbm_ref, tbn_ref, cb_ref, pool_ref,
                 om_ref, on_ref,
                 qms, qns, selm, sels, m_m, l_m, a_m, m_n, l_n, a_n, oc,
                 *, q_rows, q0_base, tail_tile):
    Q = q_rows
    s_id = pl.program_id(1)
    i = si_ref[s_id]
    t = st_ref[s_id]
    q0 = q0_base + i * Q
    ob = lax.shift_right_logical(q0, 8)
    nb = km_ref.shape[0]
    nbs_p = pool_ref.shape[1]
    nc = cmp_ref.shape[0]

    qi = lax.broadcasted_iota(jnp.int32, (Q, 1), 0)
    qi4 = jnp.concatenate([qi] * NSA_HEADS, axis=0)
    lane_nb = lax.broadcasted_iota(jnp.int32, (Q, nb), 1)
    lane_bs = lax.broadcasted_iota(jnp.int32, (Q, nbs_p), 1)

    @pl.when(t == 0)
    def _init():
        qm = qm_ref[...]
        qn = qn_ref[...]
        qms[...] = qm * (HEAD_DIM ** -0.5)
        for h in range(NSA_HEADS):
            qns[h * Q:(h + 1) * Q, :] = qn[:, HEAD_DIM * h:HEAD_DIM * (h + 1)] * (HEAD_DIM ** -0.5)
        km = km_ref[...]
        lane_f = lane_nb.astype(F32)
        for h in range(MOBA_HEADS):
            sl = slice(HEAD_DIM * h, HEAD_DIM * (h + 1))
            g = _dot_nt(qm[:, sl], km[:, sl], precision=HI)
            g = jnp.where(lane_nb < ob, g, -jnp.inf)
            selm[h] = _topk_lanes(g, min(MOBA_TOPK, nb), lane_f, nb)
        cm = cmp_ref[...]
        sc = _dot_nt(qns[...], cm[:, :HEAD_DIM]) + cb_ref[...]
        epos = lax.broadcasted_iota(jnp.int32, (NSA_HEADS * Q, nc), 1) * NSA_CMP_STRIDE + (2 * NSA_CMP_STRIDE - 1)
        valid = epos <= q0 + qi4
        sc = jnp.where(valid, sc, NEG)
        mx = jnp.max(sc, axis=1, keepdims=True)
        e = jnp.where(valid, jnp.exp(sc - mx), 0.0)
        den = jnp.sum(e, axis=1, keepdims=True)
        p = e / jnp.where(den > 0, den, 1.0)
        oc[...] = _dot(p, cm[:, HEAD_DIM:])
        psum = p[0:Q] + p[Q:2 * Q] + p[2 * Q:3 * Q] + p[3 * Q:4 * Q]
        imp = jnp.dot(psum, pool_ref[...], precision=HI, preferred_element_type=F32)
        cur = lax.shift_right_logical(q0 + qi, 6)
        forced = (lane_bs == 0) | (lane_bs == cur) | (lane_bs == cur - 1)
        score = jnp.where(forced, jnp.inf, jnp.where(lane_bs <= cur, imp, -jnp.inf))
        sels[...] = _topk_lanes(score, NSA_TOPN, lane_bs.astype(F32), nbs_p)
        m_m[...] = jnp.full(m_m.shape, NEG, F32)
        l_m[...] = jnp.zeros(l_m.shape, F32)
        a_m[...] = jnp.zeros(a_m.shape, F32)
        m_n[...] = jnp.full(m_n.shape, NEG, F32)
        l_n[...] = jnp.zeros(l_n.shape, F32)
        a_n[...] = jnp.zeros(a_n.shape, F32)

    delta = q0 - t * PAGE
    kj = lax.broadcasted_iota(jnp.int32, (Q, PAGE), 1)
    d = delta + qi - kj
    causal = d >= 0
    v_idx = jnp.minimum(lax.shift_right_logical(delta, 7), 2)

    if tail_tile is None:
        kv = kv_ref[...]
        ns = ns_ref[...]
    else:
        is_tail = t == tail_tile
        kv = jnp.where(is_tail, tkv_ref[...], kv_ref[...])
        ns = jnp.where(is_tail, tns_ref[...], ns_ref[...])

    blk = lax.shift_right_logical(t, 1)
    own = blk == ob
    q_all = qms[...]
    for h in range(MOBA_HEADS):
        sl = slice(HEAD_DIM * h, HEAD_DIM * (h + 1))
        s = _dot_nt(q_all[:, sl], kv[:, sl]) + tbm_ref[h, v_idx]
        selcol = jnp.sum(jnp.where(lane_nb == blk, selm[h], 0.0), axis=1, keepdims=True)
        valid = causal & ((selcol > 0.0) | own)
        _online(m_m, l_m, a_m, h, s, valid, kv[:, 256 + HEAD_DIM * h:256 + HEAD_DIM * (h + 1)])

    qn4 = qns[...]
    s = _dot_nt(qn4, ns[:, :HEAD_DIM]) + tbn_ref[v_idx]
    sel_all = sels[...]
    lo = jnp.sum(jnp.where(lane_bs == 2 * t, sel_all, 0.0), axis=1, keepdims=True)
    hi = jnp.sum(jnp.where(lane_bs == 2 * t + 1, sel_all, 0.0), axis=1, keepdims=True)
    vmask = causal & (jnp.where(kj < NSA_SEL_BLOCK, lo, hi) > 0.0)
    _online(m_n, l_n, a_n, 0, s, jnp.concatenate([vmask] * NSA_HEADS, axis=0), ns[:, HEAD_DIM:])

    @pl.when(t >= lax.shift_right_logical(q0, 7) - NSA_WINDOW // PAGE)
    def _window():
        wv = win_ref[...]
        sw = _dot_nt(qn4, wv[:, :HEAD_DIM]) + tbn_ref[v_idx]
        wmask = causal & (d < NSA_WINDOW)
        _online(m_n, l_n, a_n, 1, sw, jnp.concatenate([wmask] * NSA_HEADS, axis=0), wv[:, HEAD_DIM:])

    @pl.when(sl_ref[s_id] == 1)
    def _finish():
        outs = []
        for h in range(MOBA_HEADS):
            l = l_m[h]
            outs.append(a_m[h] / jnp.where(l > 0, l, 1.0))
        om_ref[...] = jnp.concatenate(outs, axis=1)
        l = l_n[0]
        o_s = a_n[0] / jnp.where(l > 0, l, 1.0)
        l = l_n[1]
        o_w = a_n[1] / jnp.where(l > 0, l, 1.0)
        o_c = oc[...]
        gn = gn_ref[...]
        outs = []
        for h in range(NSA_HEADS):
            r = slice(h * Q, (h + 1) * Q)
            outs.append(gn[:, 3 * h:3 * h + 1] * o_c[r] + gn[:, 3 * h + 1:3 * h + 2] * o_s[r]
                        + gn[:, 3 * h + 2:3 * h + 3] * o_w[r])
        on_ref[...] = jnp.concatenate(outs, axis=1)


def _rel_bucket_np(dist):
    n = np.maximum(dist, 0)
    exact = N_BUCKETS // 2
    nf = np.maximum(n, 1).astype(np.float32)
    large = exact + (np.log(nf / np.float32(exact)) / np.float32(math.log(MAX_DISTANCE / exact))
                     * np.float32(N_BUCKETS - exact)).astype(np.int32)
    return np.where(n < exact, n, np.minimum(large, N_BUCKETS - 1)).astype(np.int32)


def attention_tables(rel_bias, q_rows, n_chunks, q0_base, nc):
    Q = q_rows
    qi = np.arange(Q)[:, None]
    kj = np.arange(PAGE)[None, :]
    d = np.stack([v * PAGE + qi - kj for v in range(3)])
    tb = jnp.take(rel_bias, jnp.asarray(_rel_bucket_np(d)), axis=0)
    tbm = tb[..., :MOBA_HEADS].transpose(3, 0, 1, 2)
    tbn = tb[..., MOBA_HEADS:].transpose(0, 3, 1, 2).reshape(3, NSA_HEADS * Q, PAGE)
    qpos = q0_base + np.arange(n_chunks)[:, None, None] * Q + qi[None]
    epos = (np.arange(nc) * NSA_CMP_STRIDE + 2 * NSA_CMP_STRIDE - 1)[None, None, :]
    cb = jnp.take(rel_bias, jnp.asarray(_rel_bucket_np(qpos - epos)), axis=0)[..., MOBA_HEADS:]
    cb = cb.transpose(0, 3, 1, 2).reshape(n_chunks, NSA_HEADS * Q, nc)
    return tbm, tbn, cb


def attention(qm, qn, gn, q_blk0, kv_pages, ns_pages, page_idx, tail_kv, tail_ns, win_pages, win_w0t,
              kmean, cmp, tables, *, n_seq, q_rows, n_chunks, q0_base, tiles_per_seq, n_win_tiles, n_sel_blocks):
    Q = q_rows
    tbm, tbn, cb = tables
    nb = kmean.shape[1]
    nc = cmp.shape[1]
    nbs_p = -(-n_sel_blocks // 128) * 128
    per = NSA_SEL_BLOCK // NSA_CMP_STRIDE
    pool = jnp.asarray((np.arange(nc)[:, None] // per == np.arange(nbs_p)[None, :]).astype(np.float32))
    has_tail = tail_kv is not None

    steps = []
    for i in range(n_chunks):
        t_last = (q0_base + i * Q + Q - 1) // PAGE
        for t in range(t_last + 1):
            steps.append((i, t, int(t == t_last)))
    steps = np.asarray(steps, np.int32)
    n_steps = steps.shape[0]
    tail_tile = tiles_per_seq if has_tail else None
    if not has_tail:
        tail_kv = jnp.zeros((1, PAGE, 512), F32)
        tail_ns = jnp.zeros((1, PAGE, 256), F32)

    def q_map(b, s, si, st, sl, pg):
        return (q_blk0 + b * n_chunks + si[s], 0)

    def o_map(b, s, si, st, sl, pg):
        return (b * n_chunks + si[s], 0)

    def page_of(b, s, st, pg):
        return pg[b * tiles_per_seq + jnp.minimum(st[s], tiles_per_seq - 1)]

    def tail_map(b):
        return b if has_tail else 0

    gs = pltpu.PrefetchScalarGridSpec(
        num_scalar_prefetch=4,
        grid=(n_seq, n_steps),
        in_specs=[
            pl.BlockSpec((Q, 256), q_map),
            pl.BlockSpec((Q, 256), q_map),
            pl.BlockSpec((Q, 128), q_map),
            pl.BlockSpec((None, PAGE, 512), lambda b, s, si, st, sl, pg: (page_of(b, s, st, pg), 0, 0)),
            pl.BlockSpec((None, PAGE, 128), lambda b, s, si, st, sl, pg: (page_of(b, s, st, pg), 0, 1)),
            pl.BlockSpec((None, PAGE, 512), lambda b, s, si, st, sl, pg: (tail_map(b), 0, 0)),
            pl.BlockSpec((None, PAGE, 128), lambda b, s, si, st, sl, pg: (tail_map(b), 0, 1)),
            pl.BlockSpec((None, PAGE, 128),
                         lambda b, s, si, st, sl, pg: (b * n_win_tiles + jnp.clip(st[s] - win_w0t, 0, n_win_tiles - 1), 0, 0)),
            pl.BlockSpec((None, nb, 256), lambda b, s, si, st, sl, pg: (b, 0, 0)),
            pl.BlockSpec((None, nc, 128), lambda b, s, si, st, sl, pg: (b, 0, 0)),
            pl.BlockSpec((MOBA_HEADS, 3, Q, PAGE), lambda b, s, si, st, sl, pg: (0, 0, 0, 0)),
            pl.BlockSpec((3, NSA_HEADS * Q, PAGE), lambda b, s, si, st, sl, pg: (0, 0, 0)),
            pl.BlockSpec((None, NSA_HEADS * Q, nc), lambda b, s, si, st, sl, pg: (si[s], 0, 0)),
            pl.BlockSpec((nc, nbs_p), lambda b, s, si, st, sl, pg: (0, 0)),
        ],
        out_specs=[pl.BlockSpec((Q, 256), o_map), pl.BlockSpec((Q, 256), o_map)],
        scratch_shapes=[
            pltpu.VMEM((Q, 256), F32),
            pltpu.VMEM((NSA_HEADS * Q, HEAD_DIM), F32),
            pltpu.VMEM((MOBA_HEADS, Q, nb), F32),
            pltpu.VMEM((Q, nbs_p), F32),
            pltpu.VMEM((MOBA_HEADS, Q, 1), F32),
            pltpu.VMEM((MOBA_HEADS, Q, 1), F32),
            pltpu.VMEM((MOBA_HEADS, Q, HEAD_DIM), F32),
            pltpu.VMEM((2, NSA_HEADS * Q, 1), F32),
            pltpu.VMEM((2, NSA_HEADS * Q, 1), F32),
            pltpu.VMEM((2, NSA_HEADS * Q, HEAD_DIM), F32),
            pltpu.VMEM((NSA_HEADS * Q, HEAD_DIM), F32),
        ],
    )
    n_tok = n_seq * n_chunks * Q
    return pl.pallas_call(
        functools.partial(_attn_kernel, q_rows=Q, q0_base=q0_base, tail_tile=tail_tile),
        grid_spec=gs,
        out_shape=[jax.ShapeDtypeStruct((n_tok, 256), F32), jax.ShapeDtypeStruct((n_tok, 256), F32)],
        compiler_params=_cparams(("parallel", "arbitrary")),
        name="attention",
    )(jnp.asarray(steps[:, 0]), jnp.asarray(steps[:, 1]), jnp.asarray(steps[:, 2]), page_idx,
      qm, qn, gn, kv_pages, ns_pages, tail_kv, tail_ns, win_pages, kmean, cmp, tbm, tbn, cb, pool)


WIN_TILES = NSA_WINDOW // PAGE + 1


def _bucket_thresholds():
    b = _rel_bucket_np(np.arange(4 * MAX_DISTANCE))
    return [int(np.argmax(b >= k)) for k in range(N_BUCKETS)]


def _softmax_update(m_ref, l_ref, a_ref, idx, s, valid, vs):
    s = jnp.where(valid, s, NEG)
    m_old = m_ref[idx]
    m_new = jnp.maximum(m_old, jnp.max(s, axis=1, keepdims=True))
    p = jnp.where(valid, jnp.exp(s - m_new), 0.0)
    alpha = jnp.exp(m_old - m_new)
    l_ref[idx] = alpha * l_ref[idx] + jnp.sum(p, axis=1, keepdims=True)
    pv = _dot(p[:, 0:PAGE], vs[0])
    for j in range(1, len(vs)):
        pv = pv + _dot(p[:, PAGE * j:PAGE * (j + 1)], vs[j])
    a_ref[idx] = alpha * a_ref[idx] + pv
    m_ref[idx] = m_new


def _attn_slots_kernel(si_ref, st_ref, sl_ref, pg_ref, *refs, q_rows, q0_base, n_slots, cache_rows, tail_tile):
    Q, G = q_rows, n_slots
    rb_ref, qm_ref, qn_ref, gn_ref = refs[0:4]
    kv_refs = refs[4:4 + G]
    ns_refs = refs[4 + G:4 + 2 * G]
    tkv_ref, tns_ref = refs[4 + 2 * G:6 + 2 * G]
    win_refs = refs[6 + 2 * G:6 + 2 * G + WIN_TILES]
    base = 6 + 2 * G + WIN_TILES
    km_ref, cmp_ref, tbm_ref, tbn_ref, pool_ref, om_ref, on_ref = refs[base:base + 7]
    qms, qns, selm, sels, m_m, l_m, a_m, m_n, l_n, a_n, oc, ow = refs[base + 7:]

    s_id = pl.program_id(1)
    i = si_ref[s_id]
    t0 = st_ref[s_id]
    q0 = q0_base + i * Q
    ob = lax.shift_right_logical(q0, 8)
    nb = km_ref.shape[0]
    nbs_p = pool_ref.shape[1]
    nc = cmp_ref.shape[0]
    R4 = NSA_HEADS * Q

    qi = lax.broadcasted_iota(jnp.int32, (Q, 1), 0)
    qi4 = jnp.concatenate([qi] * NSA_HEADS, axis=0)
    kj = lax.broadcasted_iota(jnp.int32, (Q, PAGE), 1)
    lane_nb = lax.broadcasted_iota(jnp.int32, (Q, nb), 1)
    lane_bs = lax.broadcasted_iota(jnp.int32, (Q, nbs_p), 1)
    rep4 = lambda x: jnp.concatenate([x] * NSA_HEADS, axis=0)

    @pl.when(t0 == 0)
    def _init():
        qm = qm_ref[...]
        qn = qn_ref[...]
        qms[...] = qm * (HEAD_DIM ** -0.5)
        for h in range(NSA_HEADS):
            qns[h * Q:(h + 1) * Q, :] = qn[:, HEAD_DIM * h:HEAD_DIM * (h + 1)] * (HEAD_DIM ** -0.5)
        qn4 = qns[...]
        km = km_ref[...]
        lane_f = lane_nb.astype(F32)
        for h in range(MOBA_HEADS):
            sl = slice(HEAD_DIM * h, HEAD_DIM * (h + 1))
            g = _dot_nt(qm[:, sl], km[:, sl], precision=HI)
            g = jnp.where(lane_nb < ob, g, -jnp.inf)
            selm[h] = _topk_lanes(g, min(MOBA_TOPK, nb), lane_f, nb)
        cm = cmp_ref[...]
        epos = lax.broadcasted_iota(jnp.int32, (Q, nc), 1) * NSA_CMP_STRIDE + (2 * NSA_CMP_STRIDE - 1)
        dc = q0 + qi - epos
        thr = _bucket_thresholds()
        bias = [jnp.full((Q, nc), rb_ref[N_BUCKETS - 1, MOBA_HEADS + h], F32) for h in range(NSA_HEADS)]
        for k in range(N_BUCKETS - 2, -1, -1):
            below = dc < thr[k + 1]
            bias = [jnp.where(below, rb_ref[k, MOBA_HEADS + h], bias[h]) for h in range(NSA_HEADS)]
        sc = _dot_nt(qn4, cm[:, :HEAD_DIM]) + jnp.concatenate(bias, axis=0)
        valid = rep4(dc >= 0)
        sc = jnp.where(valid, sc, NEG)
        mx = jnp.max(sc, axis=1, keepdims=True)
        e = jnp.where(valid, jnp.exp(sc - mx), 0.0)
        den = jnp.sum(e, axis=1, keepdims=True)
        p = e / jnp.where(den > 0, den, 1.0)
        oc[...] = _dot(p, cm[:, HEAD_DIM:])
        psum = p[0:Q] + p[Q:2 * Q] + p[2 * Q:3 * Q] + p[3 * Q:4 * Q]
        imp = jnp.dot(psum, pool_ref[...], precision=HI, preferred_element_type=F32)
        cur = lax.shift_right_logical(q0 + qi, 6)
        forced = (lane_bs == 0) | (lane_bs == cur) | (lane_bs == cur - 1)
        score = jnp.where(forced, jnp.inf, jnp.where(lane_bs <= cur, imp, -jnp.inf))
        sels[...] = _topk_lanes(score, NSA_TOPN, lane_bs.astype(F32), nbs_p)
        tw0 = lax.shift_right_logical(q0, 7) - (WIN_TILES - 1)
        s_parts, v_parts, m_parts = [], [], []
        for w in range(WIN_TILES):
            wv = win_refs[w][...]
            back = WIN_TILES - 1 - w
            dw = back * PAGE + qi - kj
            s_parts.append(_dot_nt(qn4, wv[:, :HEAD_DIM]) + tbn_ref[min(back, 2)])
            m_parts.append((dw >= 0) & (dw < NSA_WINDOW) & (tw0 + w >= 0))
            v_parts.append(wv[:, HEAD_DIM:])
        sw = jnp.concatenate(s_parts, axis=1)
        wvalid = rep4(jnp.concatenate(m_parts, axis=1))
        sw = jnp.where(wvalid, sw, NEG)
        mx = jnp.max(sw, axis=1, keepdims=True)
        e = jnp.where(wvalid, jnp.exp(sw - mx), 0.0)
        den = jnp.sum(e, axis=1, keepdims=True)
        pw = e / jnp.where(den > 0, den, 1.0)
        o_w = _dot(pw[:, 0:PAGE], v_parts[0])
        for w in range(1, WIN_TILES):
            o_w = o_w + _dot(pw[:, PAGE * w:PAGE * (w + 1)], v_parts[w])
        ow[...] = o_w
        m_m[...] = jnp.full(m_m.shape, NEG, F32)
        l_m[...] = jnp.zeros(l_m.shape, F32)
        a_m[...] = jnp.zeros(a_m.shape, F32)
        m_n[...] = jnp.full(m_n.shape, NEG, F32)
        l_n[...] = jnp.zeros(l_n.shape, F32)
        a_n[...] = jnp.zeros(a_n.shape, F32)

    is_tail = (t0 == tail_tile) if tail_tile is not None else None

    def pick(j, cache_fn, tail_fn):
        x = cache_fn(kv_refs[j], ns_refs[j])
        if is_tail is not None and j == 0:
            x = jnp.where(is_tail, tail_fn(tkv_ref, tns_ref), x)
        return x

    if cache_rows:
        rows_kv = 2 * MOBA_HEADS
        k_of = lambda j, h: pick(j, *[lambda kv, ns: kv[pl.ds(h, PAGE, stride=rows_kv), :]] * 2)
        v_of = lambda j, h: pick(j, *[lambda kv, ns: kv[pl.ds(MOBA_HEADS + h, PAGE, stride=rows_kv), :]] * 2)
        ksel_of = lambda j: pick(j, *[lambda kv, ns: ns[pl.ds(2, PAGE, stride=4), :]] * 2)
        vsel_of = lambda j: pick(j, *[lambda kv, ns: ns[pl.ds(3, PAGE, stride=4), :]] * 2)
    else:
        k_of = lambda j, h: kv_refs[j][:, HEAD_DIM * h:HEAD_DIM * (h + 1)]
        v_of = lambda j, h: kv_refs[j][:, 256 + HEAD_DIM * h:256 + HEAD_DIM * (h + 1)]
        ksel_of = lambda j: ns_refs[j][:, 0:HEAD_DIM]
        vsel_of = lambda j: ns_refs[j][:, HEAD_DIM:2 * HEAD_DIM]

    causal, v_idx, blk = [], [], []
    for j in range(G):
        delta = q0 - (t0 + j) * PAGE
        causal.append(delta + qi - kj >= 0)
        v_idx.append(jnp.clip(lax.shift_right_arithmetic(delta, 7), 0, 2))
        blk.append(lax.shift_right_logical(t0 + j, 1))

    q_all = qms[...]
    for h in range(MOBA_HEADS):
        sl = slice(HEAD_DIM * h, HEAD_DIM * (h + 1))
        sel_h = selm[h]
        s_parts, m_parts = [], []
        for j in range(G):
            s_parts.append(_dot_nt(q_all[:, sl], k_of(j, h)) + tbm_ref[h, v_idx[j]])
            selcol = jnp.sum(jnp.where(lane_nb == blk[j], sel_h, 0.0), axis=1, keepdims=True)
            m_parts.append(causal[j] & ((selcol > 0.0) | (blk[j] == ob)))
        _softmax_update(m_m, l_m, a_m, h, jnp.concatenate(s_parts, axis=1), jnp.concatenate(m_parts, axis=1),
                        [v_of(j, h) for j in range(G)])

    qn4 = qns[...]
    sel_all = sels[...]
    s_parts, m_parts = [], []
    for j in range(G):
        s_parts.append(_dot_nt(qn4, ksel_of(j)) + tbn_ref[v_idx[j]])
        lo = jnp.sum(jnp.where(lane_bs == 2 * (t0 + j), sel_all, 0.0), axis=1, keepdims=True)
        hi = jnp.sum(jnp.where(lane_bs == 2 * (t0 + j) + 1, sel_all, 0.0), axis=1, keepdims=True)
        m_parts.append(causal[j] & (jnp.where(kj < NSA_SEL_BLOCK, lo, hi) > 0.0))
    _softmax_update(m_n, l_n, a_n, 0, jnp.concatenate(s_parts, axis=1), rep4(jnp.concatenate(m_parts, axis=1)),
                    [vsel_of(j) for j in range(G)])

    @pl.when(sl_ref[s_id] == 1)
    def _finish():
        outs = []
        for h in range(MOBA_HEADS):
            l = l_m[h]
            outs.append(a_m[h] / jnp.where(l > 0, l, 1.0))
        om_ref[...] = jnp.concatenate(outs, axis=1)
        l = l_n[0]
        o_s = a_n[0] / jnp.where(l > 0, l, 1.0)
        o_w = ow[...]
        o_c = oc[...]
        gn = gn_ref[...]
        outs = []
        for h in range(NSA_HEADS):
            r = slice(h * Q, (h + 1) * Q)
            outs.append(gn[:, 3 * h:3 * h + 1] * o_c[r] + gn[:, 3 * h + 1:3 * h + 2] * o_s[r]
                        + gn[:, 3 * h + 2:3 * h + 3] * o_w[r])
        on_ref[...] = jnp.concatenate(outs, axis=1)


def bias_tiles(rel_bias, q_rows):
    Q = q_rows
    qi = np.arange(Q)[:, None]
    kj = np.arange(PAGE)[None, :]
    d = np.stack([v * PAGE + qi - kj for v in range(3)])
    tb = jnp.take(rel_bias, jnp.asarray(_rel_bucket_np(d)), axis=0)
    tbm = tb[..., :MOBA_HEADS].transpose(3, 0, 1, 2)
    tbn = tb[..., MOBA_HEADS:].transpose(0, 3, 1, 2).reshape(3, NSA_HEADS * Q, PAGE)
    return tbm, tbn


def attention_slots(rel_bias, qm, qn, gn, q_blk0, kv_src, ns_src, page_idx, tail_kv, tail_ns, win_pages, win_w0t,
                    kmean, cmp, tiles, *, n_seq, q_rows, n_chunks, q0_base, tiles_per_seq, n_win_tiles,
                    n_sel_blocks, n_slots, cache_rows):
    Q, G = q_rows, n_slots
    assert PAGE % Q == 0 and q0_base % PAGE == 0 and (n_chunks == 1 or Q == PAGE)
    tbm, tbn = tiles
    nb = kmean.shape[1]
    nc = cmp.shape[1]
    nbs_p = -(-n_sel_blocks // 128) * 128
    per = NSA_SEL_BLOCK // NSA_CMP_STRIDE
    pool = jnp.asarray((np.arange(nc)[:, None] // per == np.arange(nbs_p)[None, :]).astype(np.float32))
    has_tail = tail_kv is not None

    steps = []
    for i in range(n_chunks):
        n_tiles = (q0_base + i * Q + Q - 1) // PAGE + 1
        n_st = -(-n_tiles // G)
        for s in range(n_st):
            steps.append((i, s * G, int(s == n_st - 1)))
    steps = np.asarray(steps, np.int32)
    n_steps = steps.shape[0]
    tail_tile = tiles_per_seq if has_tail else None
    assert tail_tile is None or tail_tile % G == 0
    kv_blk = (PAGE * 2 * MOBA_HEADS, HEAD_DIM) if cache_rows else (None, PAGE, 512)
    ns_blk = (PAGE * 4, HEAD_DIM) if cache_rows else (None, PAGE, 128)
    if not has_tail:
        tail_kv = jnp.zeros((1,) + tuple(x for x in kv_blk if x), F32)
        tail_ns = jnp.zeros((1,) + (tuple(x for x in ns_blk if x) if cache_rows else (PAGE, 256)), F32)

    def q_map(b, s, si, st, sl, pg):
        return (q_blk0 + b * n_chunks + si[s], 0)

    def o_map(b, s, si, st, sl, pg):
        return (b * n_chunks + si[s], 0)

    def page_of(b, s, st, pg, j):
        return pg[b * tiles_per_seq + jnp.minimum(st[s] + j, tiles_per_seq - 1)]

    def kv_map(j):
        if cache_rows:
            return lambda b, s, si, st, sl, pg: (page_of(b, s, st, pg, j), 0)
        return lambda b, s, si, st, sl, pg: (page_of(b, s, st, pg, j), 0, 0)

    def ns_map(j):
        if cache_rows:
            return lambda b, s, si, st, sl, pg: (page_of(b, s, st, pg, j), 0)
        return lambda b, s, si, st, sl, pg: (page_of(b, s, st, pg, j), 0, 1)

    def tail_map(lane_blk):
        def f(b, s, si, st, sl, pg):
            t = b if has_tail else 0
            return (t, 0, 0) if cache_rows or lane_blk is None else (t, 0, lane_blk)
        return f

    def win_map(w):
        def f(b, s, si, st, sl, pg):
            tw = lax.shift_right_logical(q0_base + si[s] * Q, 7) - (WIN_TILES - 1) + w - win_w0t
            return (b * n_win_tiles + jnp.clip(tw, 0, n_win_tiles - 1), 0, 0)
        return f

    const = lambda nd: (lambda b, s, si, st, sl, pg: (0,) * nd)
    tail_kv_blk = (None,) + kv_blk if cache_rows else kv_blk
    tail_ns_blk = (None,) + ns_blk if cache_rows else ns_blk
    in_specs = ([pl.BlockSpec(memory_space=pltpu.SMEM),
                 pl.BlockSpec((Q, 256), q_map), pl.BlockSpec((Q, 256), q_map), pl.BlockSpec((Q, 128), q_map)]
                + [pl.BlockSpec(kv_blk, kv_map(j)) for j in range(G)]
                + [pl.BlockSpec(ns_blk, ns_map(j)) for j in range(G)]
                + [pl.BlockSpec(tail_kv_blk, tail_map(None)), pl.BlockSpec(tail_ns_blk, tail_map(1))]
                + [pl.BlockSpec((None, PAGE, 128), win_map(w)) for w in range(WIN_TILES)]
                + [pl.BlockSpec((None, nb, 256), lambda b, s, si, st, sl, pg: (b, 0, 0)),
                   pl.BlockSpec((None, nc, 128), lambda b, s, si, st, sl, pg: (b, 0, 0)),
                   pl.BlockSpec((MOBA_HEADS, 3, Q, PAGE), const(4)),
                   pl.BlockSpec((3, NSA_HEADS * Q, PAGE), const(3)),
                   pl.BlockSpec((nc, nbs_p), const(2))])
    gs = pltpu.PrefetchScalarGridSpec(
        num_scalar_prefetch=4,
        grid=(n_seq, n_steps),
        in_specs=in_specs,
        out_specs=[pl.BlockSpec((Q, 256), o_map), pl.BlockSpec((Q, 256), o_map)],
        scratch_shapes=[
            pltpu.VMEM((Q, 256), F32),
            pltpu.VMEM((NSA_HEADS * Q, HEAD_DIM), F32),
            pltpu.VMEM((MOBA_HEADS, Q, nb), F32),
            pltpu.VMEM((Q, nbs_p), F32),
            pltpu.VMEM((MOBA_HEADS, Q, 1), F32),
            pltpu.VMEM((MOBA_HEADS, Q, 1), F32),
            pltpu.VMEM((MOBA_HEADS, Q, HEAD_DIM), F32),
            pltpu.VMEM((1, NSA_HEADS * Q, 1), F32),
            pltpu.VMEM((1, NSA_HEADS * Q, 1), F32),
            pltpu.VMEM((1, NSA_HEADS * Q, HEAD_DIM), F32),
            pltpu.VMEM((NSA_HEADS * Q, HEAD_DIM), F32),
            pltpu.VMEM((NSA_HEADS * Q, HEAD_DIM), F32),
        ],
    )
    n_tok = n_seq * n_chunks * Q
    return pl.pallas_call(
        functools.partial(_attn_slots_kernel, q_rows=Q, q0_base=q0_base, n_slots=G, cache_rows=cache_rows,
                          tail_tile=tail_tile),
        grid_spec=gs,
        out_shape=[jax.ShapeDtypeStruct((n_tok, 256), F32), jax.ShapeDtypeStruct((n_tok, 256), F32)],
        compiler_params=_cparams(("parallel", "arbitrary")),
        name="attention",
    )(jnp.asarray(steps[:, 0]), jnp.asarray(steps[:, 1]), jnp.asarray(steps[:, 2]), page_idx, rel_bias,
      qm, qn, gn, *([kv_src] * G), *([ns_src] * G), tail_kv, tail_ns, *([win_pages] * WIN_TILES),
      kmean, cmp, tbm, tbn, pool)


def _softplus(x):
    return jnp.maximum(x, 0.0) + jnp.log(1.0 + jnp.exp(-jnp.abs(x)))


def _ssd_kernel(xbc_ref, z_ref, dt_ref, dtt_ref, cprev_ref, st0_ref, cw_ref, cb_ref, dtb_ref, dtbt_ref,
                al_ref, alt_ref, dd_ref, nw_ref, y_ref, st_ref, xp, *, valid_len):
    c = pl.program_id(1)
    cl = xbc_ref.shape[0]
    gn = SSD_GROUPS * SSD_STATE

    @pl.when(c == 0)
    def _first():
        xp[0:8, :] = cprev_ref[...]
        st_ref[...] = st0_ref[...]

    xp[8:8 + cl, :] = xbc_ref[...]
    conv = cb_ref[...] + cw_ref[0:1, :] * xp[5:5 + cl, :]
    for k in range(1, SSD_CONV):
        conv = conv + cw_ref[k:k + 1, :] * xp[5 + k:5 + k + cl, :]
    xp[0:8, :] = xp[cl:cl + 8, :]
    act = _silu(conv)
    xs = act[:, :SSD_INNER]
    bm = act[:, SSD_INNER:SSD_INNER + gn]
    cm = act[:, SSD_INNER + gn:]

    row = lax.broadcasted_iota(jnp.int32, (cl, cl), 0)
    col = lax.broadcasted_iota(jnp.int32, (cl, cl), 1)
    tri = row >= col
    pos_r = c * cl + lax.broadcasted_iota(jnp.int32, (cl, 1), 0)
    pos_c = c * cl + lax.broadcasted_iota(jnp.int32, (1, cl), 1)
    dt = jnp.where(pos_r < valid_len, _softplus(dt_ref[...] + dtb_ref[...]), 0.0)
    dtt = jnp.where(pos_c < valid_len, _softplus(dtt_ref[...] + dtbt_ref[...]), 0.0)
    da = dt * (-jnp.exp(al_ref[...]))
    dat = dtt * (-jnp.exp(alt_ref[...]))
    acum = jnp.dot(jnp.where(tri, 1.0, 0.0), da, precision=HI, preferred_element_type=F32)
    acumt = jnp.dot(dat, jnp.where(row <= col, 1.0, 0.0), precision=HI, preferred_element_type=F32)

    cbs = [_dot_nt(cm[:, SSD_STATE * g:SSD_STATE * (g + 1)], bm[:, SSD_STATE * g:SSD_STATE * (g + 1)])
           for g in range(SSD_GROUPS)]
    ys = []
    for h in range(SSD_HEADS):
        g = h // (SSD_HEADS // SSD_GROUPS)
        bg = bm[:, SSD_STATE * g:SSD_STATE * (g + 1)]
        cg = cm[:, SSD_STATE * g:SSD_STATE * (g + 1)]
        a_col = acum[:, h:h + 1]
        a_row = acumt[h:h + 1, :]
        a_last = acumt[h:h + 1, cl - 1:cl]
        lmat = jnp.exp(jnp.where(tri, a_col - a_row, -jnp.inf))
        xh = xs[:, SSD_HEAD_DIM * h:SSD_HEAD_DIM * (h + 1)]
        xdt = xh * dt[:, h:h + 1]
        y_diag = _dot(cbs[g] * lmat, xdt)
        prev = st_ref[h]
        y_off = jnp.exp(a_col) * _dot_nt(cg, prev)
        decay = jnp.exp(a_last - a_col)
        upd = lax.dot_general(xdt.astype(BF16), (bg * decay).astype(BF16), (((0,), (0,)), ((), ())),
                              preferred_element_type=F32)
        st_ref[h] = prev * jnp.exp(a_last) + upd
        ys.append(y_diag + y_off + dd_ref[:, h:h + 1] * xh)
    y = jnp.concatenate(ys, axis=1) * _silu(z_ref[...])
    half = SSD_INNER // SSD_GROUPS
    outs = []
    for g in range(SSD_GROUPS):
        yg = y[:, half * g:half * (g + 1)]
        outs.append(yg * lax.rsqrt(jnp.mean(yg * yg, axis=-1, keepdims=True) + EPS))
    y_ref[...] = jnp.concatenate(outs, axis=1) * nw_ref[...]


def ssd(xbc, z, dt, dt_t, conv_prev8, state0, cw, cb, dtb, alog, dd, nw, *, n_seq, seq_rows, chunk, valid_len):
    n_ch = seq_rows // chunk

    def pad128(v):
        return jnp.pad(v.reshape(1, -1), ((0, 0), (0, 128 - v.shape[-1])))

    tok = lambda w: pl.BlockSpec((chunk, w), lambda b, c: (b * n_ch + c, 0))
    full = lambda shp: pl.BlockSpec(shp, lambda b, c: (0,) * len(shp))
    return pl.pallas_call(
        functools.partial(_ssd_kernel, valid_len=valid_len),
        grid=(n_seq, n_ch),
        in_specs=[tok(SSD_CONV_DIM), tok(SSD_INNER), tok(128),
                  pl.BlockSpec((None, SSD_HEADS, chunk), lambda b, c: (b, 0, c)),
                  pl.BlockSpec((None, 8, SSD_CONV_DIM), lambda b, c: (b, 0, 0)),
                  pl.BlockSpec((None, SSD_HEADS, SSD_HEAD_DIM, SSD_STATE), lambda b, c: (b, 0, 0, 0)),
                  full((SSD_CONV, SSD_CONV_DIM)), full((1, SSD_CONV_DIM)), full((1, 128)), full((SSD_HEADS, 1)),
                  full((1, 128)), full((SSD_HEADS, 1)), full((1, 128)), full((1, SSD_INNER))],
        out_specs=[tok(SSD_INNER),
                   pl.BlockSpec((None, SSD_HEADS, SSD_HEAD_DIM, SSD_STATE), lambda b, c: (b, 0, 0, 0))],
        out_shape=[jax.ShapeDtypeStruct((n_seq * seq_rows, SSD_INNER), F32),
                   jax.ShapeDtypeStruct((n_seq, SSD_HEADS, SSD_HEAD_DIM, SSD_STATE), F32)],
        scratch_shapes=[pltpu.VMEM((chunk + 8, SSD_CONV_DIM), F32)],
        compiler_params=_cparams(("parallel", "arbitrary")),
        name="ssd",
    )(xbc, z, dt, dt_t, conv_prev8, state0, cw, cb.reshape(1, -1), pad128(dtb), dtb.reshape(-1, 1),
      pad128(alog), alog.reshape(-1, 1), pad128(dd), nw.reshape(1, -1))


def _mix_kernel(om_ref, on_ref, ys_ref, gbr_ref, x_ref, g1_ref, sc2_ref, sh2_ref, nw1_ref, nw2_ref,
                wbm_ref, wbn_ref, wbs_ref, wo_ref, rwt_ref, x1_ref, h2_ref, h2t_ref, lg_ref):
    d = D_MODEL
    ya = _dot(om_ref[...], wbm_ref[...])
    yb = _dot(on_ref[...], wbn_ref[...])
    yc = _dot(ys_ref[...], wbs_ref[...])
    merged = gbr_ref[:, 0:d] * ya + gbr_ref[:, d:2 * d] * yb + gbr_ref[:, 2 * d:3 * d] * yc
    m = _dot(merged, wo_ref[...])
    x1 = x_ref[...] + g1_ref[...] * _rms(m, nw1_ref[...])
    h2 = _rms(x1, nw2_ref[...]) * (1.0 + sc2_ref[...]) + sh2_ref[...]
    x1_ref[...] = x1
    h2_ref[...] = h2
    for s in range(ROW_TILES):
        h2t_ref[pl.ds(s, TM, stride=ROW_TILES), :] = h2[:, 128 * s:128 * (s + 1)]
    lg_ref[...] = _dot_nt(rwt_ref[...], h2, precision=HI)


def mix_out(om, on, ys, gbr, x, modx, nw1, nw2, wbm, wbn, wbs, wo, rwt, group_of_tile):
    n, d = x.shape
    tok = lambda w: pl.BlockSpec((TM, w), lambda i: (i, 0))
    full = lambda a: pl.BlockSpec(a.shape, lambda i: (0,) * a.ndim, pipeline_mode=pl.Buffered(1))
    return pl.pallas_call(
        _mix_kernel,
        grid=(n // TM,),
        in_specs=[tok(256), tok(256), tok(SSD_INNER), tok(3 * d), tok(d),
                  _mod_spec(2, group_of_tile), _mod_spec(4, group_of_tile), _mod_spec(3, group_of_tile),
                  full(nw1), full(nw2), full(wbm), full(wbn), full(wbs), full(wo), full(rwt)],
        out_specs=[tok(d), tok(d), pl.BlockSpec((TM * ROW_TILES, 128), lambda i: (i, 0)),
                   pl.BlockSpec((N_EXPERTS, TM), lambda i: (0, i))],
        out_shape=[jax.ShapeDtypeStruct((n, d), F32), jax.ShapeDtypeStruct((n, d), F32),
                   jax.ShapeDtypeStruct((n * ROW_TILES, 128), F32), jax.ShapeDtypeStruct((N_EXPERTS, n), F32)],
        compiler_params=_cparams(("parallel",)),
        name="mix_out",
    )(om, on, ys, gbr, x, modx, modx, modx, nw1, nw2, wbm, wbn, wbs, wo, rwt)


def _router_kernel(lg_ref, rb_ref, eidx_ref, w8_ref, pos_ref, cnt_ref, carry):
    i = pl.program_id(0)
    tm = lg_ref.shape[1]
    per = N_EXPERTS // N_ROUTE_GROUPS

    @pl.when(i == 0)
    def _zero():
        carry[...] = jnp.zeros(carry.shape, F32)

    s = _sigmoid(lg_ref[...])
    sc = s + rb_ref[...]
    sub = lax.broadcasted_iota(jnp.int32, (per, tm), 0).astype(F32)
    gs_rows = []
    for g in range(N_ROUTE_GROUPS):
        x = sc[per * g:per * (g + 1), :]
        m1 = jnp.max(x, axis=0, keepdims=True)
        i1 = jnp.min(jnp.where(x == m1, sub, float(per)), axis=0, keepdims=True)
        m2 = jnp.max(jnp.where(sub == i1, -jnp.inf, x), axis=0, keepdims=True)
        gs_rows.append(m1 + m2)
    gs = jnp.concatenate(gs_rows, axis=0)
    gsel = jnp.zeros_like(gs)
    for _ in range(TOPK_GROUPS):
        m = jnp.max(gs, axis=0, keepdims=True)
        ix = jnp.min(jnp.where(gs == m, sub, float(N_ROUTE_GROUPS)), axis=0, keepdims=True)
        hit = sub == ix
        gsel = jnp.where(hit, 1.0, gsel)
        gs = jnp.where(hit, -jnp.inf, gs)
    emask = jnp.concatenate([jnp.broadcast_to(gsel[g:g + 1, :], (per, tm)) for g in range(N_ROUTE_GROUPS)], axis=0)
    msc = jnp.where(emask > 0.0, sc, -jnp.inf)
    e_io = lax.broadcasted_iota(jnp.int32, (N_EXPERTS, tm), 0).astype(F32)
    sel = jnp.zeros_like(sc)
    idxs = []
    for _ in range(TOP_K):
        m = jnp.max(msc, axis=0, keepdims=True)
        ix = jnp.min(jnp.where(msc == m, e_io, float(N_EXPERTS)), axis=0, keepdims=True)
        hit = e_io == ix
        sel = jnp.where(hit, 1.0, sel)
        msc = jnp.where(hit, -jnp.inf, msc)
        idxs.append(ix)
    w = s * sel
    wn = w / jnp.sum(w, axis=0, keepdims=True) * ROUTE_SCALE
    r = lax.broadcasted_iota(jnp.int32, (tm, tm), 0)
    cidx = lax.broadcasted_iota(jnp.int32, (tm, tm), 1)
    upper = jnp.where(r <= cidx, 1.0, 0.0)
    cum = _dot(sel, upper)
    rank = cum - sel + carry[:, 0:1]
    eidx_rows, w_rows, p_rows = [], [], []
    for k in range(TOP_K):
        hit = e_io == idxs[k]
        eidx_rows.append(idxs[k])
        w_rows.append(jnp.sum(jnp.where(hit, wn, 0.0), axis=0, keepdims=True))
        p_rows.append(jnp.sum(jnp.where(hit, rank, 0.0), axis=0, keepdims=True))
    eidx_ref[...] = jnp.concatenate(eidx_rows, axis=0).astype(jnp.int32)
    w8_ref[...] = jnp.concatenate(w_rows, axis=0)
    pos_ref[...] = jnp.concatenate(p_rows, axis=0).astype(jnp.int32)
    carry[...] = carry[...] + jnp.sum(sel, axis=1, keepdims=True)
    cnt_ref[...] = carry[...]


def router(logits_t, router_b):
    ne, n = logits_t.shape
    return pl.pallas_call(
        _router_kernel,
        grid=(n // TM,),
        in_specs=[pl.BlockSpec((ne, TM), lambda i: (0, i)), pl.BlockSpec((ne, 1), lambda i: (0, 0))],
        out_specs=[pl.BlockSpec((TOP_K, TM), lambda i: (0, i)), pl.BlockSpec((TOP_K, TM), lambda i: (0, i)),
                   pl.BlockSpec((TOP_K, TM), lambda i: (0, i)), pl.BlockSpec((ne, 128), lambda i: (0, 0))],
        out_shape=[jax.ShapeDtypeStruct((TOP_K, n), jnp.int32), jax.ShapeDtypeStruct((TOP_K, n), F32),
                   jax.ShapeDtypeStruct((TOP_K, n), jnp.int32), jax.ShapeDtypeStruct((ne, 128), F32)],
        scratch_shapes=[pltpu.VMEM((ne, 128), F32)],
        compiler_params=_cparams(("arbitrary",)),
        name="router",
    )(logits_t, router_b.reshape(ne, 1))


def _dispatch_kernel(dest_ref, h_ref, init_ref, rows_ref, sem):
    del init_ref
    n_pairs = dest_ref.shape[1]

    def copy(j):
        src = pl.multiple_of(lax.shift_right_logical(j, 3) * ROW_TILES, ROW_TILES)
        dst = pl.multiple_of(dest_ref[0, j] * ROW_TILES, ROW_TILES)
        return pltpu.make_async_copy(h_ref.at[pl.ds(src, ROW_TILES)], rows_ref.at[pl.ds(dst, ROW_TILES)], sem)

    def start(j, carry):
        copy(j).start()
        return carry

    def wait(j, carry):
        copy(j).wait()
        return carry

    lax.fori_loop(0, n_pairs, start, 0, unroll=8)
    lax.fori_loop(0, n_pairs, wait, 0, unroll=8)


def dispatch(h2t, dest, n_rows):
    n = h2t.shape[0] // ROW_TILES
    n_tiles = n // TM
    rows0 = jnp.zeros((n_rows * ROW_TILES, 128), F32)
    return pl.pallas_call(
        _dispatch_kernel,
        grid=(n_tiles,),
        in_specs=[pl.BlockSpec((None, 1, TM * TOP_K), lambda i: (i, 0, 0), memory_space=pltpu.SMEM),
                  pl.BlockSpec((TM * ROW_TILES, 128), lambda i: (i, 0)),
                  pl.BlockSpec(memory_space=pl.ANY)],
        out_specs=pl.BlockSpec(memory_space=pl.ANY),
        out_shape=jax.ShapeDtypeStruct((n_rows * ROW_TILES, 128), F32),
        scratch_shapes=[pltpu.SemaphoreType.DMA(())],
        input_output_aliases={2: 0},
        compiler_params=_cparams(("arbitrary",)),
        name="dispatch",
    )(dest.reshape(n_tiles, 1, TM * TOP_K), h2t, rows0)


def _expert_kernel(be_ref, nu_ref, x_ref, wg_ref, wu_ref, wd_ref, y_ref):
    @pl.when(pl.program_id(0) < nu_ref[0])
    def _():
        de = wg_ref.shape[1]
        g = jnp.zeros((EXPERT_ROWS, de), F32)
        u = jnp.zeros((EXPERT_ROWS, de), F32)
        for s in range(ROW_TILES):
            xs = x_ref[pl.ds(s, EXPERT_ROWS, stride=ROW_TILES), :].astype(BF16)
            g = g + jnp.dot(xs, wg_ref[128 * s:128 * (s + 1), :].astype(BF16), preferred_element_type=F32)
            u = u + jnp.dot(xs, wu_ref[128 * s:128 * (s + 1), :].astype(BF16), preferred_element_type=F32)
        y = _dot(_silu(g) * u, wd_ref[...])
        for s in range(ROW_TILES):
            y_ref[pl.ds(s, EXPERT_ROWS, stride=ROW_TILES), :] = y[:, 128 * s:128 * (s + 1)]

    @pl.when(pl.program_id(0) >= nu_ref[0])
    def _():
        y_ref[...] = jnp.zeros(y_ref.shape, F32)


def experts(x_rows, blk_e, n_used, wg, wu, wd, layer):
    n_rows = x_rows.shape[0] // ROW_TILES
    n_blk = n_rows // EXPERT_ROWS
    ne, d, de = wg.shape[1:]
    wg2 = wg.reshape(-1, d, de)
    wu2 = wu.reshape(-1, d, de)
    wd2 = wd.reshape(-1, de, d)
    gs = pltpu.PrefetchScalarGridSpec(
        num_scalar_prefetch=2,
        grid=(n_blk,),
        in_specs=[pl.BlockSpec((EXPERT_ROWS * ROW_TILES, 128), lambda i, be, nu: (i, 0)),
                  pl.BlockSpec((None, d, de), lambda i, be, nu: (layer * ne + be[i], 0, 0)),
                  pl.BlockSpec((None, d, de), lambda i, be, nu: (layer * ne + be[i], 0, 0)),
                  pl.BlockSpec((None, de, d), lambda i, be, nu: (layer * ne + be[i], 0, 0))],
        out_specs=pl.BlockSpec((EXPERT_ROWS * ROW_TILES, 128), lambda i, be, nu: (i, 0)),
    )
    return pl.pallas_call(
        _expert_kernel,
        grid_spec=gs,
        out_shape=jax.ShapeDtypeStruct((n_rows * ROW_TILES, 128), F32),
        compiler_params=_cparams(("arbitrary",)),
        name="experts",
    )(blk_e, n_used, x_rows, wg2, wu2, wd2)


def _combine_kernel(dest_ref, w8_ref, x1_ref, h2_ref, g2_ref, nw_ref, wsg_ref, wsu_ref, wsd_ref, yrows_ref,
                    out_ref, buf, sem):
    n_pairs = dest_ref.shape[1]

    def copy(j):
        src = pl.multiple_of(dest_ref[0, j] * ROW_TILES, ROW_TILES)
        dst = pl.multiple_of(lax.shift_right_logical(j, 3) * ROW_TILES, ROW_TILES)
        k = jnp.bitwise_and(j, TOP_K - 1)
        return pltpu.make_async_copy(yrows_ref.at[pl.ds(src, ROW_TILES)], buf.at[k, pl.ds(dst, ROW_TILES)], sem)

    def start(j, carry):
        copy(j).start()
        return carry

    def wait(j, carry):
        copy(j).wait()
        return carry

    lax.fori_loop(0, n_pairs, start, 0, unroll=8)
    h2 = h2_ref[...]
    shared = _dot(_silu(_dot(h2, wsg_ref[...])) * _dot(h2, wsu_ref[...]), wsd_ref[...])
    lax.fori_loop(0, n_pairs, wait, 0, unroll=8)
    w8 = w8_ref[...]
    pieces = []
    for s in range(ROW_TILES):
        acc = w8[:, 0:1] * buf[0, pl.ds(s, TM, stride=ROW_TILES), :]
        for k in range(1, TOP_K):
            acc = acc + w8[:, k:k + 1] * buf[k, pl.ds(s, TM, stride=ROW_TILES), :]
        pieces.append(acc)
    routed = jnp.concatenate(pieces, axis=1)
    out_ref[...] = x1_ref[...] + g2_ref[...] * _rms(routed + shared, nw_ref[...])


def combine(dest, w8, x1, h2, modx, nw3, wsg, wsu, wsd, y_rows, group_of_tile):
    n, d = x1.shape
    n_tiles = n // TM
    tok = lambda w: pl.BlockSpec((TM, w), lambda i: (i, 0))
    full = lambda a: pl.BlockSpec(a.shape, lambda i: (0,) * a.ndim)
    return pl.pallas_call(
        _combine_kernel,
        grid=(n_tiles,),
        in_specs=[pl.BlockSpec((None, 1, TM * TOP_K), lambda i: (i, 0, 0), memory_space=pltpu.SMEM),
                  tok(TOP_K), tok(d), tok(d), _mod_spec(5, group_of_tile),
                  full(nw3), full(wsg), full(wsu), full(wsd),
                  pl.BlockSpec(memory_space=pl.ANY)],
        out_specs=tok(d),
        out_shape=jax.ShapeDtypeStruct((n, d), F32),
        scratch_shapes=[pltpu.VMEM((TOP_K, TM * ROW_TILES, 128), F32), pltpu.SemaphoreType.DMA(())],
        compiler_params=_cparams(("arbitrary",)),
        name="combine",
    )(dest.reshape(n_tiles, 1, TM * TOP_K), w8, x1, h2, modx, nw3, wsg, wsu, wsd, y_rows)


def _pack_w_in(w):
    d = w.shape[0]
    z = lambda n: jnp.zeros((d, n), w.dtype)
    parts = [w[:, 0:1408], w[:, 1408:1420], z(116), w[:, 1420:1932], w[:, 1932:2956], w[:, 2956:2964], z(120),
             w[:, 2964:6036]]
    return jnp.concatenate(parts, axis=1).astype(BF16)


def _expand_w1(w1):
    half = NSA_CMP_STRIDE * HEAD_DIM
    w = jnp.zeros((NSA_CMP_STRIDE, 4, HEAD_DIM, 256), w1.dtype)
    for kv in range(2):
        w = w.at[:, kv, :, 128 * kv:128 * kv + 64].set(w1[kv, :half].reshape(NSA_CMP_STRIDE, HEAD_DIM, HEAD_DIM))
        w = w.at[:, kv, :, 128 * kv + 64:128 * kv + 128].set(w1[kv, half:].reshape(NSA_CMP_STRIDE, HEAD_DIM, HEAD_DIM))
    return w.reshape(NSA_CMP_STRIDE * 256, 256).astype(BF16)


def _cmp_w1_rows(w1):
    half = NSA_CMP_STRIDE * HEAD_DIM
    w = jnp.zeros((NSA_CMP_STRIDE, 2, HEAD_DIM, 256), w1.dtype)
    for kv in range(2):
        w = w.at[:, kv, :, 128 * kv:128 * kv + 64].set(w1[kv, :half].reshape(NSA_CMP_STRIDE, HEAD_DIM, HEAD_DIM))
        w = w.at[:, kv, :, 128 * kv + 64:128 * kv + 128].set(w1[kv, half:].reshape(NSA_CMP_STRIDE, HEAD_DIM, HEAD_DIM))
    return w.reshape(NSA_CMP_STRIDE * 128, 256).astype(BF16)


PROMPT_SLOTS = 4
SAMPLE_SLOTS = 8


def kernel(x_prompt, x_sample, c_prompt, c_sample, cache_moba_kv, cache_nsa_kv, cache_nsa_win, state_ssd_conv, state_ssd, page_table, rel_bias, ada_w, ada_b, norm_w, w_in, nsa_cmp_w1, nsa_cmp_b1, nsa_cmp_w2, nsa_cmp_b2, nsa_cmp_pos, ssd_conv_w, ssd_conv_b, ssd_dt_bias, ssd_a_log, ssd_d, ssd_norm_w, w_branch_moba, w_branch_nsa, w_branch_ssd, w_out, router_w, router_b, exp_w_gate, exp_w_up, exp_w_down, shared_w_gate, shared_w_up, shared_w_down):
    depth = w_in.shape[0]
    bp, lp, d = x_prompt.shape
    bs, ls, _ = x_sample.shape
    n_p = bp * lp
    n_s = bs * ls
    n = n_p + n_s
    assert n_s == TM and lp % TM == 0 and lp % SSD_CHUNK == 0
    n_pages = page_table.shape[1]
    past = n_pages * PAGE
    n_pool = cache_moba_kv.shape[1]
    tiles_p = lp // TM
    n_ptiles = n_p // TM

    def group_of_tile(i):
        return jnp.where(i < n_ptiles, i // tiles_p, bp)

    x = jnp.concatenate([x_prompt.reshape(n_p, d), x_sample.reshape(n_s, d)], axis=0)
    c_all = jnp.concatenate([c_prompt, c_sample, jnp.zeros((4, d), F32)], axis=0)
    kv_cache = cache_moba_kv.reshape(depth * n_pool * PAGE * 2 * MOBA_HEADS, HEAD_DIM)
    ns_cache = cache_nsa_kv.reshape(depth * n_pool * PAGE * 4, HEAD_DIM)
    pt_flat = page_table.reshape(-1).astype(jnp.int32)
    prompt_pages = jnp.arange(bp * (lp // PAGE), dtype=jnp.int32)

    q_chunk = 128
    tiles_p_bias = bias_tiles(rel_bias, q_chunk)
    tiles_s_bias = bias_tiles(rel_bias, ls)

    n_blk = -(-(n * TOP_K + N_EXPERTS * (EXPERT_ROWS - 1)) // EXPERT_ROWS)
    n_rows = n_blk * EXPERT_ROWS

    sample_pad = 128
    outs = [[] for _ in range(10)]
    for l in range(depth):
        mod = modulation(c_all, ada_w, ada_b, l)
        modx = jnp.concatenate([jnp.broadcast_to(mod[:bp, None, :], (bp, TM, 6 * d)),
                                jnp.repeat(mod[bp:bp + bs], ls, axis=0)[None]], axis=0)
        nw = norm_w[l]
        qm, kvm, qn, nsa, win, gn, z, xbc, dt, gbr = proj_in(x, modx, nw[0:1], _pack_w_in(w_in[l]), group_of_tile)
        w_exp = _expand_w1(nsa_cmp_w1[l])

        kvm_pages = kvm.reshape(n // PAGE, PAGE, 512)
        nsa_pages = nsa.reshape(n // PAGE, PAGE, 256)
        win_pages = win.reshape(n // PAGE, PAGE, 128)
        ks_p, pab_p = page_ctx(kvm_pages, nsa_pages, w_exp, 0, n_p // PAGE)
        km_p, cmp_p = ctx_final(ks_p.reshape(bp, lp // PAGE, 256), pab_p.reshape(bp, lp // NSA_CMP_STRIDE, 256),
                                nsa_cmp_pos, nsa_cmp_w1, nsa_cmp_b1, nsa_cmp_w2, nsa_cmp_b2, l, lp // PAGE)
        om_p, on_p = attention_slots(rel_bias, qm, qn, gn, 0, kvm_pages, nsa_pages, prompt_pages, None, None,
                                     win_pages, 0, km_p, cmp_p, tiles_p_bias,
                                     n_seq=bp, q_rows=q_chunk, n_chunks=lp // q_chunk, q0_base=0,
                                     tiles_per_seq=lp // PAGE, n_win_tiles=lp // PAGE,
                                     n_sel_blocks=lp // NSA_SEL_BLOCK, n_slots=PROMPT_SLOTS, cache_rows=False)

        ks_c, pab_c = page_ctx_cache(kv_cache, ns_cache, _cmp_w1_rows(nsa_cmp_w1[l]), l * n_pool, n_pool)
        ks_s, pab_s = ctx_gather(pt_flat, ks_c, pab_c, bs, n_pages)
        km_s, cmp_s = ctx_final(ks_s.reshape(bs, n_pages * 2 * MOBA_HEADS, HEAD_DIM), pab_s,
                                nsa_cmp_pos, nsa_cmp_w1, nsa_cmp_b1, nsa_cmp_w2, nsa_cmp_b2, l, n_pages)
        pad_rows = lambda a: jnp.pad(a[n_p:].reshape(bs, ls, -1), ((0, 0), (0, PAGE - ls), (0, 0)))
        tail_kv = pad_rows(kvm).reshape(bs, PAGE * 2 * MOBA_HEADS, HEAD_DIM)
        tail_ns = pad_rows(nsa).reshape(bs, PAGE * 4, HEAD_DIM)
        win_s = jnp.concatenate([cache_nsa_win[l].reshape(bs, -1, 128), pad_rows(win)], axis=1)
        win_s = win_s.reshape(bs * (win_s.shape[1] // PAGE), PAGE, 128)
        lf_s = -(-(past + ls) // MOBA_BLOCK) * MOBA_BLOCK
        om_s, on_s = attention_slots(rel_bias, qm, qn, gn, n_p // ls, kv_cache, ns_cache, pt_flat + l * n_pool,
                                     tail_kv, tail_ns, win_s, (past - NSA_WINDOW) // PAGE, km_s, cmp_s, tiles_s_bias,
                                     n_seq=bs, q_rows=ls, n_chunks=1, q0_base=past,
                                     tiles_per_seq=n_pages, n_win_tiles=win_s.shape[0] // bs,
                                     n_sel_blocks=lf_s // NSA_SEL_BLOCK, n_slots=SAMPLE_SLOTS, cache_rows=True)
        om = jnp.concatenate([om_p, om_s], axis=0)
        on = jnp.concatenate([on_p, on_s], axis=0)

        dt_t = dt[:, :SSD_HEADS].T
        ssd_par = (ssd_conv_w[l], ssd_conv_b[l], ssd_dt_bias[l], ssd_a_log[l], ssd_d[l], ssd_norm_w[l])
        y_p, st_p = ssd(xbc, z, dt, dt_t[:, :n_p].reshape(SSD_HEADS, bp, lp).transpose(1, 0, 2),
                        jnp.zeros((bp, 8, SSD_CONV_DIM), F32),
                        jnp.zeros((bp, SSD_HEADS, SSD_HEAD_DIM, SSD_STATE), F32), *ssd_par,
                        n_seq=bp, seq_rows=lp, chunk=SSD_CHUNK, valid_len=lp)
        pad_s = lambda a: jnp.pad(a[n_p:].reshape(bs, ls, -1), ((0, 0), (0, sample_pad - ls), (0, 0))).reshape(bs * sample_pad, -1)
        dt_t_s = jnp.pad(dt_t[:, n_p:].reshape(SSD_HEADS, bs, ls).transpose(1, 0, 2), ((0, 0), (0, 0), (0, sample_pad - ls)))
        conv_prev = jnp.pad(state_ssd_conv[l], ((0, 0), (8 - (SSD_CONV - 1), 0), (0, 0)))
        y_s, st_s = ssd(pad_s(xbc), pad_s(z), pad_s(dt), dt_t_s, conv_prev, state_ssd[l], *ssd_par,
                        n_seq=bs, seq_rows=sample_pad, chunk=sample_pad, valid_len=ls)
        ys = jnp.concatenate([y_p[:n_p], y_s.reshape(bs, sample_pad, -1)[:, :ls].reshape(n_s, -1)], axis=0)

        x1, h2, h2t, logits_t = mix_out(om, on, ys, gbr, x, modx, nw[1:2], nw[2:3],
                                        w_branch_moba[l].astype(BF16), w_branch_nsa[l].astype(BF16),
                                        w_branch_ssd[l].astype(BF16), w_out[l].astype(BF16), router_w[l].T,
                                        group_of_tile)
        eidx, w8, pos8, cnt = router(logits_t, router_b[l])
        cnt = cnt[:, 0].astype(jnp.int32)
        padded = (cnt + EXPERT_ROWS - 1) // EXPERT_ROWS * EXPERT_ROWS
        ends = jnp.cumsum(padded)
        off = ends - padded
        e_ids = jnp.arange(N_EXPERTS, dtype=jnp.int32)
        off_of = jnp.sum(jnp.where(eidx[:, :, None] == e_ids, off, 0), axis=-1)
        dest = (off_of + pos8).T.reshape(-1)
        blk_start = jnp.arange(n_blk, dtype=jnp.int32) * EXPERT_ROWS
        blk_e = jnp.minimum(jnp.sum((ends[None, :] <= blk_start[:, None]).astype(jnp.int32), axis=1), N_EXPERTS - 1)
        n_used = (ends[-1] // EXPERT_ROWS).astype(jnp.int32).reshape(1)
        x_rows = dispatch(h2t, dest, n_rows)
        y_rows = experts(x_rows, blk_e, n_used, exp_w_gate, exp_w_up, exp_w_down, l)
        x = combine(dest, w8.T, x1, h2, modx, nw[3:4], shared_w_gate[l].astype(BF16), shared_w_up[l].astype(BF16),
                    shared_w_down[l].astype(BF16), y_rows, group_of_tile)

        keep = min(NSA_WINDOW, lp)
        outs[0].append(kvm[:n_p].reshape(bp, lp, 2, MOBA_HEADS, HEAD_DIM))
        outs[1].append(kvm[n_p:].reshape(bs, ls, 2, MOBA_HEADS, HEAD_DIM))
        outs[2].append(nsa[:n_p].reshape(bp, lp, 4, HEAD_DIM))
        outs[3].append(nsa[n_p:].reshape(bs, ls, 4, HEAD_DIM))
        outs[4].append(win[:n_p].reshape(bp, lp, 2, HEAD_DIM)[:, lp - keep:])
        outs[5].append(win[n_p:].reshape(bs, ls, 2, HEAD_DIM))
        outs[6].append(xbc[:n_p].reshape(bp, lp, -1)[:, lp - (SSD_CONV - 1):])
        outs[7].append(xbc[n_p:].reshape(bs, ls, -1)[:, ls - (SSD_CONV - 1):])
        outs[8].append(st_p)
        outs[9].append(st_s)

    y_prompt = x[:n_p].reshape(bp, lp, d)
    y_sample = x[n_p:].reshape(bs, ls, d)
    return (y_prompt, y_sample) + tuple(jnp.stack(o) for o in outs)
```

```python
import functools
import math

import numpy as np
import jax
import jax.numpy as jnp
from jax import lax
from jax.experimental import pallas as pl
from jax.experimental.pallas import tpu as pltpu

F32 = jnp.float32
BF16 = jnp.bfloat16
HI = lax.Precision.HIGHEST

D_MODEL = 1024
PAGE = 128
HEAD_DIM = 64
MOBA_HEADS = 4
MOBA_BLOCK = 256
MOBA_TOPK = 3
NSA_HEADS = 4
NSA_CMP_STRIDE = 16
NSA_SEL_BLOCK = 64
NSA_TOPN = 16
NSA_WINDOW = 512
SSD_HEADS = 8
SSD_HEAD_DIM = 64
SSD_INNER = 512
SSD_GROUPS = 2
SSD_STATE = 128
SSD_CONV = 4
SSD_CHUNK = 256
SSD_CONV_DIM = 1024
N_BUCKETS = 32
MAX_DISTANCE = 128
N_EXPERTS = 64
TOP_K = 8
N_ROUTE_GROUPS = 8
TOPK_GROUPS = 4
D_EXPERT = 256
ROUTE_SCALE = 2.5
EPS = 1e-6

TM = 256
ROW_TILES = D_MODEL // 128
EXPERT_ROWS = 256
VMEM_LIMIT = 56 * 1024 * 1024
NEG = -1e30

_SEG = dict(qm=(0, 256), kvm=(256, 512), qn=(768, 256), nsa=(1024, 256), win=(1280, 128),
            gn=(1408, 128), z=(1536, 512), xbc=(2048, 1024), dt=(3072, 128), gbr=(3200, 3072))
PROJ_W = 6272


def _sigmoid(x):
    return 1.0 / (1.0 + jnp.exp(-x))


def _silu(x):
    return x * _sigmoid(x)


def _rms(x, w):
    return x * lax.rsqrt(jnp.mean(x * x, axis=-1, keepdims=True) + EPS) * w


def _dot(a, b):
    return jnp.dot(a.astype(BF16), b.astype(BF16), preferred_element_type=F32)


def _dot_nt(a, b, precision=None):
    if precision is None:
        a, b = a.astype(BF16), b.astype(BF16)
    return lax.dot_general(a, b, (((1,), (1,)), ((), ())), precision=precision,
                           preferred_element_type=F32)


def _cparams(sem, vmem=None):
    return pltpu.CompilerParams(dimension_semantics=sem, vmem_limit_bytes=vmem or VMEM_LIMIT)


def _mod_kernel(c_ref, w_ref, b_ref, o_ref):
    o_ref[...] = _dot(_silu(c_ref[...]), w_ref[...]) + b_ref[...]


def modulation(c_all, ada_w, ada_b, layer):
    rows, d = c_all.shape
    n = ada_w.shape[-1]
    tn = 512
    return pl.pallas_call(
        _mod_kernel,
        grid=(n // tn,),
        in_specs=[pl.BlockSpec((rows, d), lambda j: (0, 0)),
                  pl.BlockSpec((None, d, tn), lambda j: (layer, 0, j)),
                  pl.BlockSpec((None, 1, tn), lambda j: (layer, 0, j))],
        out_specs=pl.BlockSpec((rows, tn), lambda j: (0, j)),
        out_shape=jax.ShapeDtypeStruct((rows, n), F32),
        compiler_params=_cparams(("parallel",)),
        name="modulation",
    )(c_all, ada_w, ada_b.reshape(ada_b.shape[0], 1, n))


def _proj_kernel(x_ref, sh_ref, sc_ref, nw_ref, w_ref, qm_ref, kvm_ref, qn_ref, nsa_ref, win_ref,
                 gn_ref, z_ref, xbc_ref, dt_ref, gbr_ref):
    h = _rms(x_ref[...], nw_ref[...]) * (1.0 + sc_ref[...]) + sh_ref[...]
    hb = h.astype(BF16)

    def seg(name):
        o, w = _SEG[name]
        return jnp.dot(hb, w_ref[:, o:o + w], preferred_element_type=F32)

    qm_ref[...] = seg("qm")
    kvm_ref[...] = seg("kvm")
    qn_ref[...] = seg("qn")
    nsa_ref[...] = seg("nsa")
    win_ref[...] = seg("win")
    gn_ref[...] = _sigmoid(seg("gn"))
    z_ref[...] = seg("z")
    xbc_ref[...] = seg("xbc")
    dt_ref[...] = seg("dt")
    gbr_ref[...] = _sigmoid(seg("gbr"))


def _mod_spec(which, group_of_tile):
    return pl.BlockSpec((None, TM, D_MODEL), lambda i: (group_of_tile(i), 0, which))


def proj_in(x, modx, nw, w_packed, group_of_tile):
    n, d = x.shape
    names = ["qm", "kvm", "qn", "nsa", "win", "gn", "z", "xbc", "dt", "gbr"]
    return pl.pallas_call(
        _proj_kernel,
        grid=(n // TM,),
        in_specs=[pl.BlockSpec((TM, d), lambda i: (i, 0)),
                  _mod_spec(0, group_of_tile), _mod_spec(1, group_of_tile),
                  pl.BlockSpec((1, d), lambda i: (0, 0)),
                  pl.BlockSpec((d, PROJ_W), lambda i: (0, 0), pipeline_mode=pl.Buffered(1))],
        out_specs=[pl.BlockSpec((TM, _SEG[k][1]), lambda i: (i, 0)) for k in names],
        out_shape=[jax.ShapeDtypeStruct((n, _SEG[k][1]), F32) for k in names],
        compiler_params=_cparams(("parallel",)),
        name="proj_in",
    )(x, modx, modx, nw, w_packed)


CTX_PAGES = 16


def _page_ctx_kernel(k_ref, g_ref, w_ref, ks_ref, pab_ref):
    ks_ref[...] = jnp.sum(k_ref[...], axis=1)
    pab_ref[...] = _dot(g_ref[...], w_ref[...])


def page_ctx(kv_pages, nsa_pages, w_exp, page0, n_pages):
    groups = PAGE // NSA_CMP_STRIDE
    g_view = nsa_pages.reshape(nsa_pages.shape[0] * groups, NSA_CMP_STRIDE * 256)
    blk0 = page0 // CTX_PAGES
    return pl.pallas_call(
        _page_ctx_kernel,
        grid=(n_pages // CTX_PAGES,),
        in_specs=[pl.BlockSpec((CTX_PAGES, PAGE, 256), lambda i: (blk0 + i, 0, 0)),
                  pl.BlockSpec((CTX_PAGES * groups, NSA_CMP_STRIDE * 256), lambda i: (blk0 + i, 0)),
                  pl.BlockSpec((NSA_CMP_STRIDE * 256, 256), lambda i: (0, 0))],
        out_specs=[pl.BlockSpec((CTX_PAGES, 256), lambda i: (i, 0)),
                   pl.BlockSpec((CTX_PAGES * groups, 256), lambda i: (i, 0))],
        out_shape=[jax.ShapeDtypeStruct((n_pages, 256), F32),
                   jax.ShapeDtypeStruct((n_pages * groups, 256), F32)],
        compiler_params=_cparams(("parallel",)),
        name="page_ctx",
    )(kv_pages, g_view, w_exp)


def _page_ctx_cache_kernel(k_ref, c_ref, w_ref, ks_ref, pab_ref, tok):
    n_tok = CTX_PAGES * PAGE
    kt = jnp.concatenate([k_ref[p].reshape(MOBA_HEADS * HEAD_DIM, PAGE) for p in range(CTX_PAGES)], axis=1)
    page_of_lane = lax.shift_right_logical(lax.broadcasted_iota(jnp.int32, (CTX_PAGES, n_tok), 1), 7)
    ind = jnp.where(page_of_lane == lax.broadcasted_iota(jnp.int32, (CTX_PAGES, n_tok), 0), 1.0, 0.0)
    ks_ref[...] = _dot_nt(ind, kt)
    for p in range(CTX_PAGES):
        for c in range(2):
            tok[p * PAGE:(p + 1) * PAGE, HEAD_DIM * c:HEAD_DIM * (c + 1)] = c_ref[p, c].T
    groups = n_tok // NSA_CMP_STRIDE
    acc = jnp.zeros((groups, 256), F32)
    for r in range(NSA_CMP_STRIDE):
        acc = acc + _dot(tok[pl.ds(r, groups, stride=NSA_CMP_STRIDE), :], w_ref[128 * r:128 * (r + 1), :])
    pab_ref[...] = acc


def page_ctx_cache(kv_tiles, ns_tiles, w_cmp, layer):
    n_pages = kv_tiles.shape[1]
    groups = PAGE // NSA_CMP_STRIDE
    return pl.pallas_call(
        _page_ctx_cache_kernel,
        grid=(n_pages // CTX_PAGES,),
        in_specs=[pl.BlockSpec((None, CTX_PAGES, None, MOBA_HEADS, HEAD_DIM, PAGE), lambda i: (layer, i, 0, 0, 0, 0)),
                  pl.BlockSpec((None, CTX_PAGES, 2, HEAD_DIM, PAGE), lambda i: (layer, i, 0, 0, 0)),
                  pl.BlockSpec(w_cmp.shape, lambda i: (0, 0))],
        out_specs=[pl.BlockSpec((CTX_PAGES, 256), lambda i: (i, 0)),
                   pl.BlockSpec((CTX_PAGES * groups, 256), lambda i: (i, 0))],
        out_shape=[jax.ShapeDtypeStruct((n_pages, 256), F32),
                   jax.ShapeDtypeStruct((n_pages * groups, 256), F32)],
        scratch_shapes=[pltpu.VMEM((CTX_PAGES * PAGE, 128), F32)],
        compiler_params=_cparams(("parallel",)),
        name="page_ctx_cache",
    )(kv_tiles, ns_tiles, w_cmp)


def _ctx_gather_kernel(pt_ref, ks_hbm, pab_hbm, ks_out, pab_out, sem):
    b = pl.program_id(0)
    n_pages = ks_out.shape[1]
    groups = pab_out.shape[1] // n_pages

    def copies(p):
        page = pt_ref[b * n_pages + p]
        c1 = pltpu.make_async_copy(ks_hbm.at[pl.ds(page, 1)], ks_out.at[0, pl.ds(p, 1)], sem.at[0])
        c2 = pltpu.make_async_copy(pab_hbm.at[pl.ds(page * groups, groups)],
                                   pab_out.at[0, pl.ds(p * groups, groups)], sem.at[1])
        return c1, c2

    def start(p, carry):
        c1, c2 = copies(p)
        c1.start()
        c2.start()
        return carry

    def wait(p, carry):
        c1, c2 = copies(p)
        c1.wait()
        c2.wait()
        return carry

    lax.fori_loop(0, n_pages, start, 0)
    lax.fori_loop(0, n_pages, wait, 0)


def ctx_gather(page_table_flat, ksum, pab, n_seq, n_pages):
    groups = PAGE // NSA_CMP_STRIDE
    gs = pltpu.PrefetchScalarGridSpec(
        num_scalar_prefetch=1,
        grid=(n_seq,),
        in_specs=[pl.BlockSpec(memory_space=pl.ANY), pl.BlockSpec(memory_space=pl.ANY)],
        out_specs=[pl.BlockSpec((1, n_pages) + ksum.shape[1:], lambda b, pt: (b,) + (0,) * ksum.ndim),
                   pl.BlockSpec((1, n_pages * groups, 256), lambda b, pt: (b, 0, 0))],
        scratch_shapes=[pltpu.SemaphoreType.DMA((2,))],
    )
    return pl.pallas_call(
        _ctx_gather_kernel,
        grid_spec=gs,
        out_shape=[jax.ShapeDtypeStruct((n_seq, n_pages) + ksum.shape[1:], F32),
                   jax.ShapeDtypeStruct((n_seq, n_pages * groups, 256), F32)],
        compiler_params=_cparams(("arbitrary",)),
        name="ctx_gather",
    )(page_table_flat, ksum, pab)


def _gelu_tanh(x):
    return 0.5 * x * (1.0 + jnp.tanh(math.sqrt(2.0 / math.pi) * (x + 0.044715 * (x * x * x))))


def _ctx_final_kernel(ks_ref, pab_ref, pos_ref, w1_ref, b1_ref, w2_ref, b2_ref, km_ref, cmp_ref, *, head_rows):
    nb = km_ref.shape[0]
    n_pages = 2 * nb
    r = lax.broadcasted_iota(jnp.int32, (nb, n_pages), 0)
    c = lax.broadcasted_iota(jnp.int32, (nb, n_pages), 1)
    pair = jnp.where((c == 2 * r) | (c == 2 * r + 1), 1.0, 0.0).astype(F32)
    if head_rows:
        ks = jnp.concatenate([ks_ref[pl.ds(h, n_pages, stride=2 * MOBA_HEADS), :] for h in range(MOBA_HEADS)], axis=1)
    else:
        ks = ks_ref[...]
    km_ref[...] = jnp.dot(pair, ks, precision=HI, preferred_element_type=F32) * (1.0 / MOBA_BLOCK)

    pab = pab_ref[...]
    ng = pab.shape[0]
    row = lax.broadcasted_iota(jnp.int32, (ng, 1), 0)
    outs = []
    for kv in range(2):
        pa = pab[:, 128 * kv:128 * kv + 64]
        pb = pab[:, 128 * kv + 64:128 * kv + 128]
        pb_next = jnp.where(row == ng - 1, 0.0, pltpu.roll(pb, ng - 1, 0))
        pos_term = _dot(pos_ref[kv], w1_ref[kv]) + b1_ref[kv]
        hid = _gelu_tanh(pa + pb_next + pos_term)
        outs.append(_dot(hid, w2_ref[kv]) + b2_ref[kv])
    cmp_ref[...] = jnp.concatenate(outs, axis=1)


def ctx_final(ksum, pab, pos, w1, b1, w2, b2, layer, n_pages):
    nb = ksum.shape[0]
    head_rows = ksum.shape[-1] == HEAD_DIM
    ng = pab.shape[1]
    lf = pos.shape[2] * pos.shape[3]
    pos2 = pos.reshape(pos.shape[0], 2, 1, lf)
    return pl.pallas_call(
        functools.partial(_ctx_final_kernel, head_rows=head_rows),
        grid=(nb,),
        in_specs=[pl.BlockSpec((None,) + ksum.shape[1:], lambda b: (b, 0, 0)),
                  pl.BlockSpec((None, ng, 256), lambda b: (b, 0, 0)),
                  pl.BlockSpec((None, 2, 1, lf), lambda b: (layer, 0, 0, 0)),
                  pl.BlockSpec((None, 2, lf, HEAD_DIM), lambda b: (layer, 0, 0, 0)),
                  pl.BlockSpec((None, 2, 1, HEAD_DIM), lambda b: (layer, 0, 0, 0)),
                  pl.BlockSpec((None, 2, HEAD_DIM, HEAD_DIM), lambda b: (layer, 0, 0, 0)),
                  pl.BlockSpec((None, 2, 1, HEAD_DIM), lambda b: (layer, 0, 0, 0))],
        out_specs=[pl.BlockSpec((None, n_pages // 2, 256), lambda b: (b, 0, 0)),
                   pl.BlockSpec((None, ng, 128), lambda b: (b, 0, 0))],
        out_shape=[jax.ShapeDtypeStruct((nb, n_pages // 2, 256), F32),
                   jax.ShapeDtypeStruct((nb, ng, 128), F32)],
        compiler_params=_cparams(("parallel",)),
        name="ctx_final",
    )(ksum, pab, pos2, w1, b1.reshape(b1.shape[0], 2, 1, HEAD_DIM), w2, b2.reshape(b2.shape[0], 2, 1, HEAD_DIM))


def _topk_lanes(score, k, lane_f, n_lanes):
    sel = jnp.zeros_like(score)
    for _ in range(k):
        m = jnp.max(score, axis=1, keepdims=True)
        idx = jnp.min(jnp.where(score == m, lane_f, float(n_lanes)), axis=1, keepdims=True)
        hit = lane_f == idx
        sel = jnp.where(hit & (m > -jnp.inf), 1.0, sel)
        score = jnp.where(hit, -jnp.inf, score)
    return sel


def _rel_bucket_np(dist):
    n = np.maximum(dist, 0)
    exact = N_BUCKETS // 2
    nf = np.maximum(n, 1).astype(np.float32)
    large = exact + (np.log(nf / np.float32(exact)) / np.float32(math.log(MAX_DISTANCE / exact))
                     * np.float32(N_BUCKETS - exact)).astype(np.int32)
    return np.where(n < exact, n, np.minimum(large, N_BUCKETS - 1)).astype(np.int32)


WIN_TILES = NSA_WINDOW // PAGE + 1


def _bucket_thresholds():
    b = _rel_bucket_np(np.arange(4 * MAX_DISTANCE))
    return [int(np.argmax(b >= k)) for k in range(N_BUCKETS)]


def _softmax_update(m_ref, l_ref, a_ref, idx, s, valid, vs, pv_dot):
    s = jnp.where(valid, s, NEG)
    m_old = m_ref[idx]
    m_new = jnp.maximum(m_old, jnp.max(s, axis=1, keepdims=True))
    p = jnp.where(valid, jnp.exp(s - m_new), 0.0)
    alpha = jnp.exp(m_old - m_new)
    l_ref[idx] = alpha * l_ref[idx] + jnp.sum(p, axis=1, keepdims=True)
    acc = pv_dot(p[:, 0:PAGE], vs[0])
    for j in range(1, len(vs)):
        acc = acc + pv_dot(p[:, PAGE * j:PAGE * (j + 1)], vs[j])
    a_ref[idx] = alpha * a_ref[idx] + acc
    m_ref[idx] = m_new


def _attn_slots_kernel(si_ref, st_ref, sl_ref, pg_ref, *refs, q_rows, q0_base, n_slots, cache_tiles, tail_tile):
    Q, G = q_rows, n_slots
    rb_ref, qm_ref, qn_ref, gn_ref = refs[0:4]
    kv_refs = refs[4:4 + G]
    ns_refs = refs[4 + G:4 + 2 * G]
    tkv_ref, tns_ref = refs[4 + 2 * G:6 + 2 * G]
    win_refs = refs[6 + 2 * G:6 + 2 * G + WIN_TILES]
    base = 6 + 2 * G + WIN_TILES
    km_ref, cmp_ref, tbm_ref, tbn_ref, pool_ref, om_ref, on_ref = refs[base:base + 7]
    qms, qns, selm, sels, m_m, l_m, a_m, m_n, l_n, a_n, oc, ow = refs[base + 7:]

    s_id = pl.program_id(1)
    i = si_ref[s_id]
    t0 = st_ref[s_id]
    q0 = q0_base + i * Q
    ob = lax.shift_right_logical(q0, 8)
    nb = km_ref.shape[0]
    nbs_p = pool_ref.shape[1]
    nc = cmp_ref.shape[0]
    R4 = NSA_HEADS * Q

    qi = lax.broadcasted_iota(jnp.int32, (Q, 1), 0)
    qi4 = jnp.concatenate([qi] * NSA_HEADS, axis=0)
    kj = lax.broadcasted_iota(jnp.int32, (Q, PAGE), 1)
    lane_nb = lax.broadcasted_iota(jnp.int32, (Q, nb), 1)
    lane_bs = lax.broadcasted_iota(jnp.int32, (Q, nbs_p), 1)
    rep4 = lambda x: jnp.concatenate([x] * NSA_HEADS, axis=0)

    @pl.when(t0 == 0)
    def _init():
        qm = qm_ref[...]
        qn = qn_ref[...]
        qms[...] = qm * (HEAD_DIM ** -0.5)
        for h in range(NSA_HEADS):
            qns[h * Q:(h + 1) * Q, :] = qn[:, HEAD_DIM * h:HEAD_DIM * (h + 1)] * (HEAD_DIM ** -0.5)
        qn4 = qns[...]
        km = km_ref[...]
        lane_f = lane_nb.astype(F32)
        for h in range(MOBA_HEADS):
            sl = slice(HEAD_DIM * h, HEAD_DIM * (h + 1))
            g = _dot_nt(qm[:, sl], km[:, sl], precision=HI)
            g = jnp.where(lane_nb < ob, g, -jnp.inf)
            selm[h] = _topk_lanes(g, min(MOBA_TOPK, nb), lane_f, nb)
        cm = cmp_ref[...]
        epos = lax.broadcasted_iota(jnp.int32, (Q, nc), 1) * NSA_CMP_STRIDE + (2 * NSA_CMP_STRIDE - 1)
        dc = q0 + qi - epos
        thr = _bucket_thresholds()
        bias = [jnp.full((Q, nc), rb_ref[N_BUCKETS - 1, MOBA_HEADS + h], F32) for h in range(NSA_HEADS)]
        for k in range(N_BUCKETS - 2, -1, -1):
            below = dc < thr[k + 1]
            bias = [jnp.where(below, rb_ref[k, MOBA_HEADS + h], bias[h]) for h in range(NSA_HEADS)]
        sc = _dot_nt(qn4, cm[:, :HEAD_DIM]) + jnp.concatenate(bias, axis=0)
        valid = rep4(dc >= 0)
        sc = jnp.where(valid, sc, NEG)
        mx = jnp.max(sc, axis=1, keepdims=True)
        e = jnp.where(valid, jnp.exp(sc - mx), 0.0)
        den = jnp.sum(e, axis=1, keepdims=True)
        p = e / jnp.where(den > 0, den, 1.0)
        oc[...] = _dot(p, cm[:, HEAD_DIM:])
        psum = p[0:Q] + p[Q:2 * Q] + p[2 * Q:3 * Q] + p[3 * Q:4 * Q]
        imp = jnp.dot(psum, pool_ref[...], precision=HI, preferred_element_type=F32)
        cur = lax.shift_right_logical(q0 + qi, 6)
        forced = (lane_bs == 0) | (lane_bs == cur) | (lane_bs == cur - 1)
        score = jnp.where(forced, jnp.inf, jnp.where(lane_bs <= cur, imp, -jnp.inf))
        sels[...] = _topk_lanes(score, NSA_TOPN, lane_bs.astype(F32), nbs_p)
        tw0 = lax.shift_right_logical(q0, 7) - (WIN_TILES - 1)
        s_parts, v_parts, m_parts = [], [], []
        for w in range(WIN_TILES):
            wv = win_refs[w][...]
            back = WIN_TILES - 1 - w
            dw = back * PAGE + qi - kj
            s_parts.append(_dot_nt(qn4, wv[:, :HEAD_DIM]) + tbn_ref[min(back, 2)])
            m_parts.append((dw >= 0) & (dw < NSA_WINDOW) & (tw0 + w >= 0))
            v_parts.append(wv[:, HEAD_DIM:])
        sw = jnp.concatenate(s_parts, axis=1)
        wvalid = rep4(jnp.concatenate(m_parts, axis=1))
        sw = jnp.where(wvalid, sw, NEG)
        mx = jnp.max(sw, axis=1, keepdims=True)
        e = jnp.where(wvalid, jnp.exp(sw - mx), 0.0)
        den = jnp.sum(e, axis=1, keepdims=True)
        pw = e / jnp.where(den > 0, den, 1.0)
        o_w = _dot(pw[:, 0:PAGE], v_parts[0])
        for w in range(1, WIN_TILES):
            o_w = o_w + _dot(pw[:, PAGE * w:PAGE * (w + 1)], v_parts[w])
        ow[...] = o_w
        m_m[...] = jnp.full(m_m.shape, NEG, F32)
        l_m[...] = jnp.zeros(l_m.shape, F32)
        a_m[...] = jnp.zeros(a_m.shape, F32)
        m_n[...] = jnp.full(m_n.shape, NEG, F32)
        l_n[...] = jnp.zeros(l_n.shape, F32)
        a_n[...] = jnp.zeros(a_n.shape, F32)

    is_tail = (t0 == tail_tile) if tail_tile is not None else None

    def pick(j, cache_fn, tail_fn):
        x = cache_fn(kv_refs[j], ns_refs[j])
        if is_tail is not None and j == 0:
            x = jnp.where(is_tail, tail_fn(tkv_ref, tns_ref), x)
        return x

    if cache_tiles:
        k_of = lambda j, h: pick(j, *[lambda kv, ns: kv[0, h]] * 2)
        v_of = lambda j, h: pick(j, *[lambda kv, ns: kv[1, h]] * 2)
        ksel_of = lambda j: pick(j, *[lambda kv, ns: ns[0]] * 2)
        vsel_of = lambda j: pick(j, *[lambda kv, ns: ns[1]] * 2)
        qk, pv = _dot, _dot_nt
    else:
        k_of = lambda j, h: kv_refs[j][:, HEAD_DIM * h:HEAD_DIM * (h + 1)]
        v_of = lambda j, h: kv_refs[j][:, 256 + HEAD_DIM * h:256 + HEAD_DIM * (h + 1)]
        ksel_of = lambda j: ns_refs[j][:, 0:HEAD_DIM]
        vsel_of = lambda j: ns_refs[j][:, HEAD_DIM:2 * HEAD_DIM]
        qk, pv = _dot_nt, _dot

    causal, v_idx, blk = [], [], []
    for j in range(G):
        delta = q0 - (t0 + j) * PAGE
        causal.append(delta + qi - kj >= 0)
        v_idx.append(jnp.clip(lax.shift_right_arithmetic(delta, 7), 0, 2))
        blk.append(lax.shift_right_logical(t0 + j, 1))

    q_all = qms[...]
    for h in range(MOBA_HEADS):
        sl = slice(HEAD_DIM * h, HEAD_DIM * (h + 1))
        sel_h = selm[h]
        s_parts, m_parts = [], []
        for j in range(G):
            s_parts.append(qk(q_all[:, sl], k_of(j, h)) + tbm_ref[h, v_idx[j]])
            selcol = jnp.sum(jnp.where(lane_nb == blk[j], sel_h, 0.0), axis=1, keepdims=True)
            m_parts.append(causal[j] & ((selcol > 0.0) | (blk[j] == ob)))
        _softmax_update(m_m, l_m, a_m, h, jnp.concatenate(s_parts, axis=1), jnp.concatenate(m_parts, axis=1),
                        [v_of(j, h) for j in range(G)], pv)

    qn4 = qns[...]
    sel_all = sels[...]
    s_parts, m_parts = [], []
    for j in range(G):
        s_parts.append(qk(qn4, ksel_of(j)) + tbn_ref[v_idx[j]])
        lo = jnp.sum(jnp.where(lane_bs == 2 * (t0 + j), sel_all, 0.0), axis=1, keepdims=True)
        hi = jnp.sum(jnp.where(lane_bs == 2 * (t0 + j) + 1, sel_all, 0.0), axis=1, keepdims=True)
        m_parts.append(causal[j] & (jnp.where(kj < NSA_SEL_BLOCK, lo, hi) > 0.0))
    _softmax_update(m_n, l_n, a_n, 0, jnp.concatenate(s_parts, axis=1), rep4(jnp.concatenate(m_parts, axis=1)),
                    [vsel_of(j) for j in range(G)], pv)

    @pl.when(sl_ref[s_id] == 1)
    def _finish():
        outs = []
        for h in range(MOBA_HEADS):
            l = l_m[h]
            outs.append(a_m[h] / jnp.where(l > 0, l, 1.0))
        om_ref[...] = jnp.concatenate(outs, axis=1)
        l = l_n[0]
        o_s = a_n[0] / jnp.where(l > 0, l, 1.0)
        o_w = ow[...]
        o_c = oc[...]
        gn = gn_ref[...]
        outs = []
        for h in range(NSA_HEADS):
            r = slice(h * Q, (h + 1) * Q)
            outs.append(gn[:, 3 * h:3 * h + 1] * o_c[r] + gn[:, 3 * h + 1:3 * h + 2] * o_s[r]
                        + gn[:, 3 * h + 2:3 * h + 3] * o_w[r])
        on_ref[...] = jnp.concatenate(outs, axis=1)


def bias_tiles(rel_bias, q_rows):
    Q = q_rows
    qi = np.arange(Q)[:, None]
    kj = np.arange(PAGE)[None, :]
    d = np.stack([v * PAGE + qi - kj for v in range(3)])
    tb = jnp.take(rel_bias, jnp.asarray(_rel_bucket_np(d)), axis=0)
    tbm = tb[..., :MOBA_HEADS].transpose(3, 0, 1, 2)
    tbn = tb[..., MOBA_HEADS:].transpose(0, 3, 1, 2).reshape(3, NSA_HEADS * Q, PAGE)
    return tbm, tbn


def attention_slots(rel_bias, qm, qn, gn, q_blk0, kv_src, ns_src, page_idx, tail_kv, tail_ns, win_pages, win_w0t,
                    kmean, cmp, tiles, *, n_seq, q_rows, n_chunks, q0_base, tiles_per_seq, n_win_tiles,
                    n_sel_blocks, n_slots, cache_layer=None):
    Q, G = q_rows, n_slots
    cache_tiles = cache_layer is not None
    assert PAGE % Q == 0 and q0_base % PAGE == 0 and (n_chunks == 1 or Q == PAGE)
    tbm, tbn = tiles
    nb = kmean.shape[1]
    nc = cmp.shape[1]
    nbs_p = -(-n_sel_blocks // 128) * 128
    per = NSA_SEL_BLOCK // NSA_CMP_STRIDE
    pool = jnp.asarray((np.arange(nc)[:, None] // per == np.arange(nbs_p)[None, :]).astype(np.float32))
    has_tail = tail_kv is not None

    steps = []
    for i in range(n_chunks):
        n_tiles = (q0_base + i * Q + Q - 1) // PAGE + 1
        n_st = -(-n_tiles // G)
        for s in range(n_st):
            steps.append((i, s * G, int(s == n_st - 1)))
    steps = np.asarray(steps, np.int32)
    n_steps = steps.shape[0]
    tail_tile = tiles_per_seq if has_tail else None
    assert tail_tile is None or tail_tile % G == 0
    if cache_tiles:
        assert has_tail
        kv_blk = (None, None, 2, MOBA_HEADS, HEAD_DIM, PAGE)
        ns_blk = (None, None, 2, HEAD_DIM, PAGE)
        tail_kv_blk, tail_ns_blk = kv_blk[1:], ns_blk[1:]
    else:
        assert not has_tail
        kv_blk = tail_kv_blk = (None, PAGE, 512)
        ns_blk = tail_ns_blk = (None, PAGE, 128)
        tail_kv = jnp.zeros((1, PAGE, 512), F32)
        tail_ns = jnp.zeros((1, PAGE, 256), F32)

    def q_map(b, s, si, st, sl, pg):
        return (q_blk0 + b * n_chunks + si[s], 0)

    def o_map(b, s, si, st, sl, pg):
        return (b * n_chunks + si[s], 0)

    def page_of(b, s, st, pg, j):
        return pg[b * tiles_per_seq + jnp.minimum(st[s] + j, tiles_per_seq - 1)]

    def kv_map(j):
        if cache_tiles:
            return lambda b, s, si, st, sl, pg: (cache_layer, page_of(b, s, st, pg, j), 0, 0, 0, 0)
        return lambda b, s, si, st, sl, pg: (page_of(b, s, st, pg, j), 0, 0)

    def ns_map(j):
        if cache_tiles:
            return lambda b, s, si, st, sl, pg: (cache_layer, page_of(b, s, st, pg, j), 1, 0, 0)
        return lambda b, s, si, st, sl, pg: (page_of(b, s, st, pg, j), 0, 1)

    def tail_map(nd):
        if cache_tiles:
            return lambda b, s, si, st, sl, pg: (b,) + (0,) * (nd - 1)
        return lambda b, s, si, st, sl, pg: (0, 0, 0) if nd == 0 else (0, 0, 1)

    def win_map(w):
        def f(b, s, si, st, sl, pg):
            tw = lax.shift_right_logical(q0_base + si[s] * Q, 7) - (WIN_TILES - 1) + w - win_w0t
            return (b * n_win_tiles + jnp.clip(tw, 0, n_win_tiles - 1), 0, 0)
        return f

    const = lambda nd: (lambda b, s, si, st, sl, pg: (0,) * nd)
    in_specs = ([pl.BlockSpec(memory_space=pltpu.SMEM),
                 pl.BlockSpec((Q, 256), q_map), pl.BlockSpec((Q, 256), q_map), pl.BlockSpec((Q, 128), q_map)]
                + [pl.BlockSpec(kv_blk, kv_map(j)) for j in range(G)]
                + [pl.BlockSpec(ns_blk, ns_map(j)) for j in range(G)]
                + [pl.BlockSpec(tail_kv_blk, tail_map(5 if cache_tiles else 0)),
                   pl.BlockSpec(tail_ns_blk, tail_map(4 if cache_tiles else 1))]
                + [pl.BlockSpec((None, PAGE, 128), win_map(w)) for w in range(WIN_TILES)]
                + [pl.BlockSpec((None, nb, 256), lambda b, s, si, st, sl, pg: (b, 0, 0)),
                   pl.BlockSpec((None, nc, 128), lambda b, s, si, st, sl, pg: (b, 0, 0)),
                   pl.BlockSpec((MOBA_HEADS, 3, Q, PAGE), const(4)),
                   pl.BlockSpec((3, NSA_HEADS * Q, PAGE), const(3)),
                   pl.BlockSpec((nc, nbs_p), const(2))])
    gs = pltpu.PrefetchScalarGridSpec(
        num_scalar_prefetch=4,
        grid=(n_seq, n_steps),
        in_specs=in_specs,
        out_specs=[pl.BlockSpec((Q, 256), o_map), pl.BlockSpec((Q, 256), o_map)],
        scratch_shapes=[
            pltpu.VMEM((Q, 256), F32),
            pltpu.VMEM((NSA_HEADS * Q, HEAD_DIM), F32),
            pltpu.VMEM((MOBA_HEADS, Q, nb), F32),
            pltpu.VMEM((Q, nbs_p), F32),
            pltpu.VMEM((MOBA_HEADS, Q, 1), F32),
            pltpu.VMEM((MOBA_HEADS, Q, 1), F32),
            pltpu.VMEM((MOBA_HEADS, Q, HEAD_DIM), F32),
            pltpu.VMEM((1, NSA_HEADS * Q, 1), F32),
            pltpu.VMEM((1, NSA_HEADS * Q, 1), F32),
            pltpu.VMEM((1, NSA_HEADS * Q, HEAD_DIM), F32),
            pltpu.VMEM((NSA_HEADS * Q, HEAD_DIM), F32),
            pltpu.VMEM((NSA_HEADS * Q, HEAD_DIM), F32),
        ],
    )
    n_tok = n_seq * n_chunks * Q
    return pl.pallas_call(
        functools.partial(_attn_slots_kernel, q_rows=Q, q0_base=q0_base, n_slots=G, cache_tiles=cache_tiles,
                          tail_tile=tail_tile),
        grid_spec=gs,
        out_shape=[jax.ShapeDtypeStruct((n_tok, 256), F32), jax.ShapeDtypeStruct((n_tok, 256), F32)],
        compiler_params=_cparams(("parallel", "arbitrary")),
        name="attention",
    )(jnp.asarray(steps[:, 0]), jnp.asarray(steps[:, 1]), jnp.asarray(steps[:, 2]), page_idx, rel_bias,
      qm, qn, gn, *([kv_src] * G), *([ns_src] * G), tail_kv, tail_ns, *([win_pages] * WIN_TILES),
      kmean, cmp, tbm, tbn, pool)


def _softplus(x):
    return jnp.maximum(x, 0.0) + jnp.log(1.0 + jnp.exp(-jnp.abs(x)))


def _ssd_kernel(xbc_ref, z_ref, dt_ref, dtt_ref, cprev_ref, st0_ref, cw_ref, cb_ref, dtb_ref, dtbt_ref,
                al_ref, alt_ref, dd_ref, nw_ref, y_ref, st_ref, xp, *, valid_len):
    c = pl.program_id(1)
    cl = xbc_ref.shape[0]
    gn = SSD_GROUPS * SSD_STATE

    @pl.when(c == 0)
    def _first():
        xp[0:8, :] = cprev_ref[...]
        st_ref[...] = st0_ref[...]

    xp[8:8 + cl, :] = xbc_ref[...]
    conv = cb_ref[...] + cw_ref[0:1, :] * xp[5:5 + cl, :]
    for k in range(1, SSD_CONV):
        conv = conv + cw_ref[k:k + 1, :] * xp[5 + k:5 + k + cl, :]
    xp[0:8, :] = xp[cl:cl + 8, :]
    act = _silu(conv)
    xs = act[:, :SSD_INNER]
    bm = act[:, SSD_INNER:SSD_INNER + gn]
    cm = act[:, SSD_INNER + gn:]

    row = lax.broadcasted_iota(jnp.int32, (cl, cl), 0)
    col = lax.broadcasted_iota(jnp.int32, (cl, cl), 1)
    tri = row >= col
    pos_r = c * cl + lax.broadcasted_iota(jnp.int32, (cl, 1), 0)
    pos_c = c * cl + lax.broadcasted_iota(jnp.int32, (1, cl), 1)
    dt = jnp.where(pos_r < valid_len, _softplus(dt_ref[...] + dtb_ref[...]), 0.0)
    dtt = jnp.where(pos_c < valid_len, _softplus(dtt_ref[...] + dtbt_ref[...]), 0.0)
    da = dt * (-jnp.exp(al_ref[...]))
    dat = dtt * (-jnp.exp(alt_ref[...]))
    acum = jnp.dot(jnp.where(tri, 1.0, 0.0), da, precision=HI, preferred_element_type=F32)
    acumt = jnp.dot(dat, jnp.where(row <= col, 1.0, 0.0), precision=HI, preferred_element_type=F32)

    cbs = [_dot_nt(cm[:, SSD_STATE * g:SSD_STATE * (g + 1)], bm[:, SSD_STATE * g:SSD_STATE * (g + 1)])
           for g in range(SSD_GROUPS)]
    ys = []
    for h in range(SSD_HEADS):
        g = h // (SSD_HEADS // SSD_GROUPS)
        bg = bm[:, SSD_STATE * g:SSD_STATE * (g + 1)]
        cg = cm[:, SSD_STATE * g:SSD_STATE * (g + 1)]
        a_col = acum[:, h:h + 1]
        a_row = acumt[h:h + 1, :]
        a_last = acumt[h:h + 1, cl - 1:cl]
        lmat = jnp.exp(jnp.where(tri, a_col - a_row, -jnp.inf))
        xh = xs[:, SSD_HEAD_DIM * h:SSD_HEAD_DIM * (h + 1)]
        xdt = xh * dt[:, h:h + 1]
        y_diag = _dot(cbs[g] * lmat, xdt)
        prev = st_ref[h]
        y_off = jnp.exp(a_col) * _dot_nt(cg, prev)
        decay = jnp.exp(a_last - a_col)
        upd = lax.dot_general(xdt.astype(BF16), (bg * decay).astype(BF16), (((0,), (0,)), ((), ())),
                              preferred_element_type=F32)
        st_ref[h] = prev * jnp.exp(a_last) + upd
        ys.append(y_diag + y_off + dd_ref[:, h:h + 1] * xh)
    y = jnp.concatenate(ys, axis=1) * _silu(z_ref[...])
    half = SSD_INNER // SSD_GROUPS
    outs = []
    for g in range(SSD_GROUPS):
        yg = y[:, half * g:half * (g + 1)]
        outs.append(yg * lax.rsqrt(jnp.mean(yg * yg, axis=-1, keepdims=True) + EPS))
    y_ref[...] = jnp.concatenate(outs, axis=1) * nw_ref[...]


def ssd(xbc, z, dt, dt_t, conv_prev8, state0, cw, cb, dtb, alog, dd, nw, *, n_seq, seq_rows, chunk, valid_len):
    n_ch = seq_rows // chunk

    def pad128(v):
        return jnp.pad(v.reshape(1, -1), ((0, 0), (0, 128 - v.shape[-1])))

    tok = lambda w: pl.BlockSpec((chunk, w), lambda b, c: (b * n_ch + c, 0))
    full = lambda shp: pl.BlockSpec(shp, lambda b, c: (0,) * len(shp))
    return pl.pallas_call(
        functools.partial(_ssd_kernel, valid_len=valid_len),
        grid=(n_seq, n_ch),
        in_specs=[tok(SSD_CONV_DIM), tok(SSD_INNER), tok(128),
                  pl.BlockSpec((None, SSD_HEADS, chunk), lambda b, c: (b, 0, c)),
                  pl.BlockSpec((None, 8, SSD_CONV_DIM), lambda b, c: (b, 0, 0)),
                  pl.BlockSpec((None, SSD_HEADS, SSD_HEAD_DIM, SSD_STATE), lambda b, c: (b, 0, 0, 0)),
                  full((SSD_CONV, SSD_CONV_DIM)), full((1, SSD_CONV_DIM)), full((1, 128)), full((SSD_HEADS, 1)),
                  full((1, 128)), full((SSD_HEADS, 1)), full((1, 128)), full((1, SSD_INNER))],
        out_specs=[tok(SSD_INNER),
                   pl.BlockSpec((None, SSD_HEADS, SSD_HEAD_DIM, SSD_STATE), lambda b, c: (b, 0, 0, 0))],
        out_shape=[jax.ShapeDtypeStruct((n_seq * seq_rows, SSD_INNER), F32),
                   jax.ShapeDtypeStruct((n_seq, SSD_HEADS, SSD_HEAD_DIM, SSD_STATE), F32)],
        scratch_shapes=[pltpu.VMEM((chunk + 8, SSD_CONV_DIM), F32)],
        compiler_params=_cparams(("parallel", "arbitrary")),
        name="ssd",
    )(xbc, z, dt, dt_t, conv_prev8, state0, cw, cb.reshape(1, -1), pad128(dtb), dtb.reshape(-1, 1),
      pad128(alog), alog.reshape(-1, 1), pad128(dd), nw.reshape(1, -1))


def _mix_kernel(om_ref, on_ref, ys_ref, gbr_ref, x_ref, g1_ref, sc2_ref, sh2_ref, nw1_ref, nw2_ref,
                wbm_ref, wbn_ref, wbs_ref, wo_ref, rwt_ref, x1_ref, h2_ref, h2t_ref, lg_ref):
    d = D_MODEL
    ya = _dot(om_ref[...], wbm_ref[...])
    yb = _dot(on_ref[...], wbn_ref[...])
    yc = _dot(ys_ref[...], wbs_ref[...])
    merged = gbr_ref[:, 0:d] * ya + gbr_ref[:, d:2 * d] * yb + gbr_ref[:, 2 * d:3 * d] * yc
    m = _dot(merged, wo_ref[...])
    x1 = x_ref[...] + g1_ref[...] * _rms(m, nw1_ref[...])
    h2 = _rms(x1, nw2_ref[...]) * (1.0 + sc2_ref[...]) + sh2_ref[...]
    x1_ref[...] = x1
    h2_ref[...] = h2
    for s in range(ROW_TILES):
        h2t_ref[pl.ds(s, TM, stride=ROW_TILES), :] = h2[:, 128 * s:128 * (s + 1)]
    lg_ref[...] = _dot_nt(rwt_ref[...], h2, precision=HI)


def mix_out(om, on, ys, gbr, x, modx, nw1, nw2, wbm, wbn, wbs, wo, rwt, group_of_tile):
    n, d = x.shape
    tok = lambda w: pl.BlockSpec((TM, w), lambda i: (i, 0))
    full = lambda a: pl.BlockSpec(a.shape, lambda i: (0,) * a.ndim, pipeline_mode=pl.Buffered(1))
    return pl.pallas_call(
        _mix_kernel,
        grid=(n // TM,),
        in_specs=[tok(256), tok(256), tok(SSD_INNER), tok(3 * d), tok(d),
                  _mod_spec(2, group_of_tile), _mod_spec(4, group_of_tile), _mod_spec(3, group_of_tile),
                  full(nw1), full(nw2), full(wbm), full(wbn), full(wbs), full(wo), full(rwt)],
        out_specs=[tok(d), tok(d), pl.BlockSpec((TM * ROW_TILES, 128), lambda i: (i, 0)),
                   pl.BlockSpec((N_EXPERTS, TM), lambda i: (0, i))],
        out_shape=[jax.ShapeDtypeStruct((n, d), F32), jax.ShapeDtypeStruct((n, d), F32),
                   jax.ShapeDtypeStruct((n * ROW_TILES, 128), F32), jax.ShapeDtypeStruct((N_EXPERTS, n), F32)],
        compiler_params=_cparams(("parallel",)),
        name="mix_out",
    )(om, on, ys, gbr, x, modx, modx, modx, nw1, nw2, wbm, wbn, wbs, wo, rwt)


def _router_kernel(lg_ref, rb_ref, eidx_ref, w8_ref, pos_ref, cnt_ref, carry):
    i = pl.program_id(0)
    tm = lg_ref.shape[1]
    per = N_EXPERTS // N_ROUTE_GROUPS

    @pl.when(i == 0)
    def _zero():
        carry[...] = jnp.zeros(carry.shape, F32)

    s = _sigmoid(lg_ref[...])
    sc = s + rb_ref[...]
    sub = lax.broadcasted_iota(jnp.int32, (per, tm), 0).astype(F32)
    gs_rows = []
    for g in range(N_ROUTE_GROUPS):
        x = sc[per * g:per * (g + 1), :]
        m1 = jnp.max(x, axis=0, keepdims=True)
        i1 = jnp.min(jnp.where(x == m1, sub, float(per)), axis=0, keepdims=True)
        m2 = jnp.max(jnp.where(sub == i1, -jnp.inf, x), axis=0, keepdims=True)
        gs_rows.append(m1 + m2)
    gs = jnp.concatenate(gs_rows, axis=0)
    gsel = jnp.zeros_like(gs)
    for _ in range(TOPK_GROUPS):
        m = jnp.max(gs, axis=0, keepdims=True)
        ix = jnp.min(jnp.where(gs == m, sub, float(N_ROUTE_GROUPS)), axis=0, keepdims=True)
        hit = sub == ix
        gsel = jnp.where(hit, 1.0, gsel)
        gs = jnp.where(hit, -jnp.inf, gs)
    emask = jnp.concatenate([jnp.broadcast_to(gsel[g:g + 1, :], (per, tm)) for g in range(N_ROUTE_GROUPS)], axis=0)
    msc = jnp.where(emask > 0.0, sc, -jnp.inf)
    e_io = lax.broadcasted_iota(jnp.int32, (N_EXPERTS, tm), 0).astype(F32)
    sel = jnp.zeros_like(sc)
    idxs = []
    for _ in range(TOP_K):
        m = jnp.max(msc, axis=0, keepdims=True)
        ix = jnp.min(jnp.where(msc == m, e_io, float(N_EXPERTS)), axis=0, keepdims=True)
        hit = e_io == ix
        sel = jnp.where(hit, 1.0, sel)
        msc = jnp.where(hit, -jnp.inf, msc)
        idxs.append(ix)
    w = s * sel
    wn = w / jnp.sum(w, axis=0, keepdims=True) * ROUTE_SCALE
    r = lax.broadcasted_iota(jnp.int32, (tm, tm), 0)
    cidx = lax.broadcasted_iota(jnp.int32, (tm, tm), 1)
    upper = jnp.where(r <= cidx, 1.0, 0.0)
    cum = _dot(sel, upper)
    rank = cum - sel + carry[:, 0:1]
    eidx_rows, w_rows, p_rows = [], [], []
    for k in range(TOP_K):
        hit = e_io == idxs[k]
        eidx_rows.append(idxs[k])
        w_rows.append(jnp.sum(jnp.where(hit, wn, 0.0), axis=0, keepdims=True))
        p_rows.append(jnp.sum(jnp.where(hit, rank, 0.0), axis=0, keepdims=True))
    eidx_ref[...] = jnp.concatenate(eidx_rows, axis=0).astype(jnp.int32)
    w8_ref[...] = jnp.concatenate(w_rows, axis=0)
    pos_ref[...] = jnp.concatenate(p_rows, axis=0).astype(jnp.int32)
    carry[...] = carry[...] + jnp.sum(sel, axis=1, keepdims=True)
    cnt_ref[...] = carry[...]


def router(logits_t, router_b):
    ne, n = logits_t.shape
    return pl.pallas_call(
        _router_kernel,
        grid=(n // TM,),
        in_specs=[pl.BlockSpec((ne, TM), lambda i: (0, i)), pl.BlockSpec((ne, 1), lambda i: (0, 0))],
        out_specs=[pl.BlockSpec((TOP_K, TM), lambda i: (0, i)), pl.BlockSpec((TOP_K, TM), lambda i: (0, i)),
                   pl.BlockSpec((TOP_K, TM), lambda i: (0, i)), pl.BlockSpec((ne, 128), lambda i: (0, 0))],
        out_shape=[jax.ShapeDtypeStruct((TOP_K, n), jnp.int32), jax.ShapeDtypeStruct((TOP_K, n), F32),
                   jax.ShapeDtypeStruct((TOP_K, n), jnp.int32), jax.ShapeDtypeStruct((ne, 128), F32)],
        scratch_shapes=[pltpu.VMEM((ne, 128), F32)],
        compiler_params=_cparams(("arbitrary",)),
        name="router",
    )(logits_t, router_b.reshape(ne, 1))


def _dispatch_kernel(dest_ref, h_ref, init_ref, rows_ref, sem):
    del init_ref
    n_pairs = dest_ref.shape[1]

    def copy(j):
        src = pl.multiple_of(lax.shift_right_logical(j, 3) * ROW_TILES, ROW_TILES)
        dst = pl.multiple_of(dest_ref[0, j] * ROW_TILES, ROW_TILES)
        return pltpu.make_async_copy(h_ref.at[pl.ds(src, ROW_TILES)], rows_ref.at[pl.ds(dst, ROW_TILES)], sem)

    def start(j, carry):
        copy(j).start()
        return carry

    def wait(j, carry):
        copy(j).wait()
        return carry

    lax.fori_loop(0, n_pairs, start, 0, unroll=8)
    lax.fori_loop(0, n_pairs, wait, 0, unroll=8)


def dispatch(h2t, dest, n_rows):
    n = h2t.shape[0] // ROW_TILES
    n_tiles = n // TM
    rows0 = jnp.zeros((n_rows * ROW_TILES, 128), F32)
    return pl.pallas_call(
        _dispatch_kernel,
        grid=(n_tiles,),
        in_specs=[pl.BlockSpec((None, 1, TM * TOP_K), lambda i: (i, 0, 0), memory_space=pltpu.SMEM),
                  pl.BlockSpec((TM * ROW_TILES, 128), lambda i: (i, 0)),
                  pl.BlockSpec(memory_space=pl.ANY)],
        out_specs=pl.BlockSpec(memory_space=pl.ANY),
        out_shape=jax.ShapeDtypeStruct((n_rows * ROW_TILES, 128), F32),
        scratch_shapes=[pltpu.SemaphoreType.DMA(())],
        input_output_aliases={2: 0},
        compiler_params=_cparams(("arbitrary",)),
        name="dispatch",
    )(dest.reshape(n_tiles, 1, TM * TOP_K), h2t, rows0)


def _expert_kernel(be_ref, nu_ref, x_ref, wg_ref, wu_ref, wd_ref, y_ref):
    @pl.when(pl.program_id(0) < nu_ref[0])
    def _():
        de = wg_ref.shape[1]
        g = jnp.zeros((EXPERT_ROWS, de), F32)
        u = jnp.zeros((EXPERT_ROWS, de), F32)
        for s in range(ROW_TILES):
            xs = x_ref[pl.ds(s, EXPERT_ROWS, stride=ROW_TILES), :].astype(BF16)
            g = g + jnp.dot(xs, wg_ref[128 * s:128 * (s + 1), :].astype(BF16), preferred_element_type=F32)
            u = u + jnp.dot(xs, wu_ref[128 * s:128 * (s + 1), :].astype(BF16), preferred_element_type=F32)
        y = _dot(_silu(g) * u, wd_ref[...])
        for s in range(ROW_TILES):
            y_ref[pl.ds(s, EXPERT_ROWS, stride=ROW_TILES), :] = y[:, 128 * s:128 * (s + 1)]

    @pl.when(pl.program_id(0) >= nu_ref[0])
    def _():
        y_ref[...] = jnp.zeros(y_ref.shape, F32)


def experts(x_rows, blk_e, n_used, wg, wu, wd, layer):
    n_rows = x_rows.shape[0] // ROW_TILES
    n_blk = n_rows // EXPERT_ROWS
    ne, d, de = wg.shape[1:]
    wg2 = wg.reshape(-1, d, de)
    wu2 = wu.reshape(-1, d, de)
    wd2 = wd.reshape(-1, de, d)
    gs = pltpu.PrefetchScalarGridSpec(
        num_scalar_prefetch=2,
        grid=(n_blk,),
        in_specs=[pl.BlockSpec((EXPERT_ROWS * ROW_TILES, 128), lambda i, be, nu: (i, 0)),
                  pl.BlockSpec((None, d, de), lambda i, be, nu: (layer * ne + be[i], 0, 0)),
                  pl.BlockSpec((None, d, de), lambda i, be, nu: (layer * ne + be[i], 0, 0)),
                  pl.BlockSpec((None, de, d), lambda i, be, nu: (layer * ne + be[i], 0, 0))],
        out_specs=pl.BlockSpec((EXPERT_ROWS * ROW_TILES, 128), lambda i, be, nu: (i, 0)),
    )
    return pl.pallas_call(
        _expert_kernel,
        grid_spec=gs,
        out_shape=jax.ShapeDtypeStruct((n_rows * ROW_TILES, 128), F32),
        compiler_params=_cparams(("arbitrary",)),
        name="experts",
    )(blk_e, n_used, x_rows, wg2, wu2, wd2)


def _combine_kernel(dest_ref, w8_ref, x1_ref, h2_ref, g2_ref, nw_ref, wsg_ref, wsu_ref, wsd_ref, yrows_ref,
                    out_ref, buf, sem):
    n_pairs = dest_ref.shape[1]

    def copy(j):
        src = pl.multiple_of(dest_ref[0, j] * ROW_TILES, ROW_TILES)
        dst = pl.multiple_of(lax.shift_right_logical(j, 3) * ROW_TILES, ROW_TILES)
        k = jnp.bitwise_and(j, TOP_K - 1)
        return pltpu.make_async_copy(yrows_ref.at[pl.ds(src, ROW_TILES)], buf.at[k, pl.ds(dst, ROW_TILES)], sem)

    def start(j, carry):
        copy(j).start()
        return carry

    def wait(j, carry):
        copy(j).wait()
        return carry

    lax.fori_loop(0, n_pairs, start, 0, unroll=8)
    h2 = h2_ref[...]
    shared = _dot(_silu(_dot(h2, wsg_ref[...])) * _dot(h2, wsu_ref[...]), wsd_ref[...])
    lax.fori_loop(0, n_pairs, wait, 0, unroll=8)
    w8 = w8_ref[...]
    pieces = []
    for s in range(ROW_TILES):
        acc = w8[:, 0:1] * buf[0, pl.ds(s, TM, stride=ROW_TILES), :]
        for k in range(1, TOP_K):
            acc = acc + w8[:, k:k + 1] * buf[k, pl.ds(s, TM, stride=ROW_TILES), :]
        pieces.append(acc)
    routed = jnp.concatenate(pieces, axis=1)
    out_ref[...] = x1_ref[...] + g2_ref[...] * _rms(routed + shared, nw_ref[...])


def combine(dest, w8, x1, h2, modx, nw3, wsg, wsu, wsd, y_rows, group_of_tile):
    n, d = x1.shape
    n_tiles = n // TM
    tok = lambda w: pl.BlockSpec((TM, w), lambda i: (i, 0))
    full = lambda a: pl.BlockSpec(a.shape, lambda i: (0,) * a.ndim)
    return pl.pallas_call(
        _combine_kernel,
        grid=(n_tiles,),
        in_specs=[pl.BlockSpec((None, 1, TM * TOP_K), lambda i: (i, 0, 0), memory_space=pltpu.SMEM),
                  tok(TOP_K), tok(d), tok(d), _mod_spec(5, group_of_tile),
                  full(nw3), full(wsg), full(wsu), full(wsd),
                  pl.BlockSpec(memory_space=pl.ANY)],
        out_specs=tok(d),
        out_shape=jax.ShapeDtypeStruct((n, d), F32),
        scratch_shapes=[pltpu.VMEM((TOP_K, TM * ROW_TILES, 128), F32), pltpu.SemaphoreType.DMA(())],
        compiler_params=_cparams(("arbitrary",)),
        name="combine",
    )(dest.reshape(n_tiles, 1, TM * TOP_K), w8, x1, h2, modx, nw3, wsg, wsu, wsd, y_rows)


def _pack_w_in(w):
    d = w.shape[0]
    z = lambda n: jnp.zeros((d, n), w.dtype)
    parts = [w[:, 0:1408], w[:, 1408:1420], z(116), w[:, 1420:1932], w[:, 1932:2956], w[:, 2956:2964], z(120),
             w[:, 2964:6036]]
    return jnp.concatenate(parts, axis=1).astype(BF16)


def _expand_w1(w1):
    half = NSA_CMP_STRIDE * HEAD_DIM
    w = jnp.zeros((NSA_CMP_STRIDE, 4, HEAD_DIM, 256), w1.dtype)
    for kv in range(2):
        w = w.at[:, kv, :, 128 * kv:128 * kv + 64].set(w1[kv, :half].reshape(NSA_CMP_STRIDE, HEAD_DIM, HEAD_DIM))
        w = w.at[:, kv, :, 128 * kv + 64:128 * kv + 128].set(w1[kv, half:].reshape(NSA_CMP_STRIDE, HEAD_DIM, HEAD_DIM))
    return w.reshape(NSA_CMP_STRIDE * 256, 256).astype(BF16)


def _cmp_w1_rows(w1):
    half = NSA_CMP_STRIDE * HEAD_DIM
    w = jnp.zeros((NSA_CMP_STRIDE, 2, HEAD_DIM, 256), w1.dtype)
    for kv in range(2):
        w = w.at[:, kv, :, 128 * kv:128 * kv + 64].set(w1[kv, :half].reshape(NSA_CMP_STRIDE, HEAD_DIM, HEAD_DIM))
        w = w.at[:, kv, :, 128 * kv + 64:128 * kv + 128].set(w1[kv, half:].reshape(NSA_CMP_STRIDE, HEAD_DIM, HEAD_DIM))
    return w.reshape(NSA_CMP_STRIDE * 128, 256).astype(BF16)


PROMPT_SLOTS = 4
SAMPLE_SLOTS = 8


def kernel(x_prompt, x_sample, c_prompt, c_sample, cache_moba_kv, cache_nsa_kv, cache_nsa_win, state_ssd_conv, state_ssd, page_table, rel_bias, ada_w, ada_b, norm_w, w_in, nsa_cmp_w1, nsa_cmp_b1, nsa_cmp_w2, nsa_cmp_b2, nsa_cmp_pos, ssd_conv_w, ssd_conv_b, ssd_dt_bias, ssd_a_log, ssd_d, ssd_norm_w, w_branch_moba, w_branch_nsa, w_branch_ssd, w_out, router_w, router_b, exp_w_gate, exp_w_up, exp_w_down, shared_w_gate, shared_w_up, shared_w_down):
    depth = w_in.shape[0]
    bp, lp, d = x_prompt.shape
    bs, ls, _ = x_sample.shape
    n_p = bp * lp
    n_s = bs * ls
    n = n_p + n_s
    assert n_s == TM and lp % TM == 0 and lp % SSD_CHUNK == 0
    n_pages = page_table.shape[1]
    past = n_pages * PAGE
    n_pool = cache_moba_kv.shape[1]
    tiles_p = lp // TM
    n_ptiles = n_p // TM

    def group_of_tile(i):
        return jnp.where(i < n_ptiles, i // tiles_p, bp)

    x = jnp.concatenate([x_prompt.reshape(n_p, d), x_sample.reshape(n_s, d)], axis=0)
    c_all = jnp.concatenate([c_prompt, c_sample, jnp.zeros((4, d), F32)], axis=0)
    kv_cache = jnp.transpose(cache_moba_kv, (0, 1, 3, 4, 5, 2))
    ns_cache = jnp.transpose(cache_nsa_kv, (0, 1, 3, 4, 2))
    pt_flat = page_table.reshape(-1).astype(jnp.int32)
    prompt_pages = jnp.arange(bp * (lp // PAGE), dtype=jnp.int32)

    q_chunk = 128
    tiles_p_bias = bias_tiles(rel_bias, q_chunk)
    tiles_s_bias = bias_tiles(rel_bias, ls)

    n_blk = -(-(n * TOP_K + N_EXPERTS * (EXPERT_ROWS - 1)) // EXPERT_ROWS)
    n_rows = n_blk * EXPERT_ROWS

    sample_pad = 128
    outs = [[] for _ in range(10)]
    for l in range(depth):
        mod = modulation(c_all, ada_w, ada_b, l)
        modx = jnp.concatenate([jnp.broadcast_to(mod[:bp, None, :], (bp, TM, 6 * d)),
                                jnp.repeat(mod[bp:bp + bs], ls, axis=0)[None]], axis=0)
        nw = norm_w[l]
        qm, kvm, qn, nsa, win, gn, z, xbc, dt, gbr = proj_in(x, modx, nw[0:1], _pack_w_in(w_in[l]), group_of_tile)
        w_exp = _expand_w1(nsa_cmp_w1[l])

        kvm_pages = kvm.reshape(n // PAGE, PAGE, 512)
        nsa_pages = nsa.reshape(n // PAGE, PAGE, 256)
        win_pages = win.reshape(n // PAGE, PAGE, 128)
        ks_p, pab_p = page_ctx(kvm_pages, nsa_pages, w_exp, 0, n_p // PAGE)
        km_p, cmp_p = ctx_final(ks_p.reshape(bp, lp // PAGE, 256), pab_p.reshape(bp, lp // NSA_CMP_STRIDE, 256),
                                nsa_cmp_pos, nsa_cmp_w1, nsa_cmp_b1, nsa_cmp_w2, nsa_cmp_b2, l, lp // PAGE)
        om_p, on_p = attention_slots(rel_bias, qm, qn, gn, 0, kvm_pages, nsa_pages, prompt_pages, None, None,
                                     win_pages, 0, km_p, cmp_p, tiles_p_bias,
                                     n_seq=bp, q_rows=q_chunk, n_chunks=lp // q_chunk, q0_base=0,
                                     tiles_per_seq=lp // PAGE, n_win_tiles=lp // PAGE,
                                     n_sel_blocks=lp // NSA_SEL_BLOCK, n_slots=PROMPT_SLOTS)

        ks_c, pab_c = page_ctx_cache(kv_cache, ns_cache, _cmp_w1_rows(nsa_cmp_w1[l]), l)
        ks_s, pab_s = ctx_gather(pt_flat, ks_c, pab_c, bs, n_pages)
        km_s, cmp_s = ctx_final(ks_s, pab_s, nsa_cmp_pos, nsa_cmp_w1, nsa_cmp_b1, nsa_cmp_w2, nsa_cmp_b2, l, n_pages)
        pad_rows = lambda a: jnp.pad(a[n_p:].reshape(bs, ls, -1), ((0, 0), (0, PAGE - ls), (0, 0)))
        tail_kv = jnp.transpose(pad_rows(kvm).reshape(bs, PAGE, 2, MOBA_HEADS, HEAD_DIM), (0, 2, 3, 4, 1))
        tail_ns = jnp.transpose(pad_rows(nsa).reshape(bs, PAGE, 4, HEAD_DIM)[:, :, 2:], (0, 2, 3, 1))
        win_s = jnp.concatenate([cache_nsa_win[l].reshape(bs, -1, 128), pad_rows(win)], axis=1)
        win_s = win_s.reshape(bs * (win_s.shape[1] // PAGE), PAGE, 128)
        lf_s = -(-(past + ls) // MOBA_BLOCK) * MOBA_BLOCK
        om_s, on_s = attention_slots(rel_bias, qm, qn, gn, n_p // ls, kv_cache, ns_cache, pt_flat,
                                     tail_kv, tail_ns, win_s, (past - NSA_WINDOW) // PAGE, km_s, cmp_s, tiles_s_bias,
                                     n_seq=bs, q_rows=ls, n_chunks=1, q0_base=past,
                                     tiles_per_seq=n_pages, n_win_tiles=win_s.shape[0] // bs,
                                     n_sel_blocks=lf_s // NSA_SEL_BLOCK, n_slots=SAMPLE_SLOTS, cache_layer=l)
        om = jnp.concatenate([om_p, om_s], axis=0)
        on = jnp.concatenate([on_p, on_s], axis=0)

        dt_t = dt[:, :SSD_HEADS].T
        ssd_par = (ssd_conv_w[l], ssd_conv_b[l], ssd_dt_bias[l], ssd_a_log[l], ssd_d[l], ssd_norm_w[l])
        y_p, st_p = ssd(xbc, z, dt, dt_t[:, :n_p].reshape(SSD_HEADS, bp, lp).transpose(1, 0, 2),
                        jnp.zeros((bp, 8, SSD_CONV_DIM), F32),
                        jnp.zeros((bp, SSD_HEADS, SSD_HEAD_DIM, SSD_STATE), F32), *ssd_par,
                        n_seq=bp, seq_rows=lp, chunk=SSD_CHUNK, valid_len=lp)
        pad_s = lambda a: jnp.pad(a[n_p:].reshape(bs, ls, -1), ((0, 0), (0, sample_pad - ls), (0, 0))).reshape(bs * sample_pad, -1)
        dt_t_s = jnp.pad(dt_t[:, n_p:].reshape(SSD_HEADS, bs, ls).transpose(1, 0, 2), ((0, 0), (0, 0), (0, sample_pad - ls)))
        conv_prev = jnp.pad(state_ssd_conv[l], ((0, 0), (8 - (SSD_CONV - 1), 0), (0, 0)))
        y_s, st_s = ssd(pad_s(xbc), pad_s(z), pad_s(dt), dt_t_s, conv_prev, state_ssd[l], *ssd_par,
                        n_seq=bs, seq_rows=sample_pad, chunk=sample_pad, valid_len=ls)
        ys = jnp.concatenate([y_p[:n_p], y_s.reshape(bs, sample_pad, -1)[:, :ls].reshape(n_s, -1)], axis=0)

        x1, h2, h2t, logits_t = mix_out(om, on, ys, gbr, x, modx, nw[1:2], nw[2:3],
                                        w_branch_moba[l].astype(BF16), w_branch_nsa[l].astype(BF16),
                                        w_branch_ssd[l].astype(BF16), w_out[l].astype(BF16), router_w[l].T,
                                        group_of_tile)
        eidx, w8, pos8, cnt = router(logits_t, router_b[l])
        cnt = cnt[:, 0].astype(jnp.int32)
        padded = (cnt + EXPERT_ROWS - 1) // EXPERT_ROWS * EXPERT_ROWS
        ends = jnp.cumsum(padded)
        off = ends - padded
        e_ids = jnp.arange(N_EXPERTS, dtype=jnp.int32)
        off_of = jnp.sum(jnp.where(eidx[:, :, None] == e_ids, off, 0), axis=-1)
        dest = (off_of + pos8).T.reshape(-1)
        blk_start = jnp.arange(n_blk, dtype=jnp.int32) * EXPERT_ROWS
        blk_e = jnp.minimum(jnp.sum((ends[None, :] <= blk_start[:, None]).astype(jnp.int32), axis=1), N_EXPERTS - 1)
        n_used = (ends[-1] // EXPERT_ROWS).astype(jnp.int32).reshape(1)
        x_rows = dispatch(h2t, dest, n_rows)
        y_rows = experts(x_rows, blk_e, n_used, exp_w_gate, exp_w_up, exp_w_down, l)
        x = combine(dest, w8.T, x1, h2, modx, nw[3:4], shared_w_gate[l].astype(BF16), shared_w_up[l].astype(BF16),
                    shared_w_down[l].astype(BF16), y_rows, group_of_tile)

        keep = min(NSA_WINDOW, lp)
        outs[0].append(kvm[:n_p].reshape(bp, lp, 2, MOBA_HEADS, HEAD_DIM))
        outs[1].append(kvm[n_p:].reshape(bs, ls, 2, MOBA_HEADS, HEAD_DIM))
        outs[2].append(nsa[:n_p].reshape(bp, lp, 4, HEAD_DIM))
        outs[3].append(nsa[n_p:].reshape(bs, ls, 4, HEAD_DIM))
        outs[4].append(win[:n_p].reshape(bp, lp, 2, HEAD_DIM)[:, lp - keep:])
        outs[5].append(win[n_p:].reshape(bs, ls, 2, HEAD_DIM))
        outs[6].append(xbc[:n_p].reshape(bp, lp, -1)[:, lp - (SSD_CONV - 1):])
        outs[7].append(xbc[n_p:].reshape(bs, ls, -1)[:, ls - (SSD_CONV - 1):])
        outs[8].append(st_p)
        outs[9].append(st_s)

    y_prompt = x[:n_p].reshape(bp, lp, d)
    y_sample = x[n_p:].reshape(bs, ls, d)
    return (y_prompt, y_sample) + tuple(jnp.stack(o) for o in outs)
```

```python
import functools
import math

import numpy as np
import jax
import jax.numpy as jnp
from jax import lax
from jax.experimental import pallas as pl
from jax.experimental.pallas import tpu as pltpu

F32 = jnp.float32
BF16 = jnp.bfloat16
HI = lax.Precision.HIGHEST

D_MODEL = 1024
PAGE = 128
HEAD_DIM = 64
MOBA_HEADS = 4
MOBA_BLOCK = 256
MOBA_TOPK = 3
NSA_HEADS = 4
NSA_CMP_STRIDE = 16
NSA_SEL_BLOCK = 64
NSA_TOPN = 16
NSA_WINDOW = 512
SSD_HEADS = 8
SSD_HEAD_DIM = 64
SSD_INNER = 512
SSD_GROUPS = 2
SSD_STATE = 128
SSD_CONV = 4
SSD_CHUNK = 256
SSD_CONV_DIM = 1024
N_BUCKETS = 32
MAX_DISTANCE = 128
N_EXPERTS = 64
TOP_K = 8
N_ROUTE_GROUPS = 8
TOPK_GROUPS = 4
D_EXPERT = 256
ROUTE_SCALE = 2.5
EPS = 1e-6

TM = 256
ROW_TILES = D_MODEL // 128
EXPERT_ROWS = 256
VMEM_LIMIT = 56 * 1024 * 1024
NEG = -1e30

_SEG = dict(qm=(0, 256), kvm=(256, 512), qn=(768, 256), nsa=(1024, 256), win=(1280, 128),
            gn=(1408, 128), z=(1536, 512), xbc=(2048, 1024), dt=(3072, 128), gbr=(3200, 3072))
PROJ_W = 6272


def _sigmoid(x):
    return 1.0 / (1.0 + jnp.exp(-x))


def _silu(x):
    return x * _sigmoid(x)


def _rms(x, w):
    return x * lax.rsqrt(jnp.mean(x * x, axis=-1, keepdims=True) + EPS) * w


def _dot(a, b):
    return jnp.dot(a.astype(BF16), b.astype(BF16), preferred_element_type=F32)


def _dot_nt(a, b, precision=None):
    if precision is None:
        a, b = a.astype(BF16), b.astype(BF16)
    return lax.dot_general(a, b, (((1,), (1,)), ((), ())), precision=precision,
                           preferred_element_type=F32)


def _cparams(sem, vmem=None):
    return pltpu.CompilerParams(dimension_semantics=sem, vmem_limit_bytes=vmem or VMEM_LIMIT)


def _mod_kernel(c_ref, w_ref, b_ref, o_ref):
    o_ref[...] = _dot(_silu(c_ref[...]), w_ref[...]) + b_ref[...]


def modulation(c_all, ada_w, ada_b, layer):
    rows, d = c_all.shape
    n = ada_w.shape[-1]
    tn = 512
    return pl.pallas_call(
        _mod_kernel,
        grid=(n // tn,),
        in_specs=[pl.BlockSpec((rows, d), lambda j: (0, 0)),
                  pl.BlockSpec((None, d, tn), lambda j: (layer, 0, j)),
                  pl.BlockSpec((None, 1, tn), lambda j: (layer, 0, j))],
        out_specs=pl.BlockSpec((rows, tn), lambda j: (0, j)),
        out_shape=jax.ShapeDtypeStruct((rows, n), F32),
        compiler_params=_cparams(("parallel",)),
        name="modulation",
    )(c_all, ada_w, ada_b.reshape(ada_b.shape[0], 1, n))


def _proj_kernel(x_ref, sh_ref, sc_ref, nw_ref, w_ref, qm_ref, kvm_ref, qn_ref, nsa_ref, win_ref,
                 gn_ref, z_ref, xbc_ref, dt_ref, gbr_ref):
    h = _rms(x_ref[...], nw_ref[...]) * (1.0 + sc_ref[...]) + sh_ref[...]
    hb = h.astype(BF16)

    def seg(name):
        o, w = _SEG[name]
        return jnp.dot(hb, w_ref[:, o:o + w], preferred_element_type=F32)

    qm_ref[...] = seg("qm")
    kvm_ref[...] = seg("kvm")
    qn_ref[...] = seg("qn")
    nsa_ref[...] = seg("nsa")
    win_ref[...] = seg("win")
    gn_ref[...] = _sigmoid(seg("gn"))
    z_ref[...] = seg("z")
    xbc_ref[...] = seg("xbc")
    dt_ref[...] = seg("dt")
    gbr_ref[...] = _sigmoid(seg("gbr"))


def _mod_spec(which, group_of_tile):
    return pl.BlockSpec((None, TM, D_MODEL), lambda i: (group_of_tile(i), 0, which))


def proj_in(x, modx, nw, w_packed, group_of_tile):
    n, d = x.shape
    names = ["qm", "kvm", "qn", "nsa", "win", "gn", "z", "xbc", "dt", "gbr"]
    return pl.pallas_call(
        _proj_kernel,
        grid=(n // TM,),
        in_specs=[pl.BlockSpec((TM, d), lambda i: (i, 0)),
                  _mod_spec(0, group_of_tile), _mod_spec(1, group_of_tile),
                  pl.BlockSpec((1, d), lambda i: (0, 0)),
                  pl.BlockSpec((d, PROJ_W), lambda i: (0, 0), pipeline_mode=pl.Buffered(1))],
        out_specs=[pl.BlockSpec((TM, _SEG[k][1]), lambda i: (i, 0)) for k in names],
        out_shape=[jax.ShapeDtypeStruct((n, _SEG[k][1]), F32) for k in names],
        compiler_params=_cparams(("parallel",)),
        name="proj_in",
    )(x, modx, modx, nw, w_packed)


CTX_PAGES = 16


def _page_ctx_kernel(k_ref, g_ref, w_ref, ks_ref, pab_ref):
    ks_ref[...] = jnp.sum(k_ref[...], axis=1)
    pab_ref[...] = _dot(g_ref[...], w_ref[...])


def page_ctx(kv_pages, nsa_pages, w_exp, page0, n_pages):
    groups = PAGE // NSA_CMP_STRIDE
    g_view = nsa_pages.reshape(nsa_pages.shape[0] * groups, NSA_CMP_STRIDE * 256)
    blk0 = page0 // CTX_PAGES
    return pl.pallas_call(
        _page_ctx_kernel,
        grid=(n_pages // CTX_PAGES,),
        in_specs=[pl.BlockSpec((CTX_PAGES, PAGE, 256), lambda i: (blk0 + i, 0, 0)),
                  pl.BlockSpec((CTX_PAGES * groups, NSA_CMP_STRIDE * 256), lambda i: (blk0 + i, 0)),
                  pl.BlockSpec((NSA_CMP_STRIDE * 256, 256), lambda i: (0, 0))],
        out_specs=[pl.BlockSpec((CTX_PAGES, 256), lambda i: (i, 0)),
                   pl.BlockSpec((CTX_PAGES * groups, 256), lambda i: (i, 0))],
        out_shape=[jax.ShapeDtypeStruct((n_pages, 256), F32),
                   jax.ShapeDtypeStruct((n_pages * groups, 256), F32)],
        compiler_params=_cparams(("parallel",)),
        name="page_ctx",
    )(kv_pages, g_view, w_exp)


def _page_ctx_cache_kernel(k_ref, c_ref, w_ref, ks_ref, pab_ref, tok):
    n_tok = CTX_PAGES * PAGE
    kt = jnp.concatenate([k_ref[p].reshape(MOBA_HEADS * HEAD_DIM, PAGE) for p in range(CTX_PAGES)], axis=1)
    page_of_lane = lax.shift_right_logical(lax.broadcasted_iota(jnp.int32, (CTX_PAGES, n_tok), 1), 7)
    ind = jnp.where(page_of_lane == lax.broadcasted_iota(jnp.int32, (CTX_PAGES, n_tok), 0), 1.0, 0.0)
    ks_ref[...] = _dot_nt(ind, kt)
    for p in range(CTX_PAGES):
        for c in range(2):
            tok[p * PAGE:(p + 1) * PAGE, HEAD_DIM * c:HEAD_DIM * (c + 1)] = c_ref[p, c].T
    groups = n_tok // NSA_CMP_STRIDE
    acc = jnp.zeros((groups, 256), F32)
    for r in range(NSA_CMP_STRIDE):
        acc = acc + _dot(tok[pl.ds(r, groups, stride=NSA_CMP_STRIDE), :], w_ref[128 * r:128 * (r + 1), :])
    pab_ref[...] = acc


def page_ctx_cache(kv_tiles, ns_tiles, w_cmp, layer):
    n_pages = kv_tiles.shape[1]
    groups = PAGE // NSA_CMP_STRIDE
    return pl.pallas_call(
        _page_ctx_cache_kernel,
        grid=(n_pages // CTX_PAGES,),
        in_specs=[pl.BlockSpec((None, CTX_PAGES, None, MOBA_HEADS, HEAD_DIM, PAGE), lambda i: (layer, i, 0, 0, 0, 0)),
                  pl.BlockSpec((None, CTX_PAGES, 2, HEAD_DIM, PAGE), lambda i: (layer, i, 0, 0, 0)),
                  pl.BlockSpec(w_cmp.shape, lambda i: (0, 0))],
        out_specs=[pl.BlockSpec((CTX_PAGES, 256), lambda i: (i, 0)),
                   pl.BlockSpec((CTX_PAGES * groups, 256), lambda i: (i, 0))],
        out_shape=[jax.ShapeDtypeStruct((n_pages, 256), F32),
                   jax.ShapeDtypeStruct((n_pages * groups, 256), F32)],
        scratch_shapes=[pltpu.VMEM((CTX_PAGES * PAGE, 128), F32)],
        compiler_params=_cparams(("parallel",)),
        name="page_ctx_cache",
    )(kv_tiles, ns_tiles, w_cmp)


def _ctx_gather_kernel(pt_ref, ks_hbm, pab_hbm, ks_out, pab_out, sem):
    b = pl.program_id(0)
    n_pages = ks_out.shape[1]
    groups = pab_out.shape[1] // n_pages

    def copies(p):
        page = pt_ref[b * n_pages + p]
        c1 = pltpu.make_async_copy(ks_hbm.at[pl.ds(page, 1)], ks_out.at[0, pl.ds(p, 1)], sem.at[0])
        c2 = pltpu.make_async_copy(pab_hbm.at[pl.ds(page * groups, groups)],
                                   pab_out.at[0, pl.ds(p * groups, groups)], sem.at[1])
        return c1, c2

    def start(p, carry):
        c1, c2 = copies(p)
        c1.start()
        c2.start()
        return carry

    def wait(p, carry):
        c1, c2 = copies(p)
        c1.wait()
        c2.wait()
        return carry

    lax.fori_loop(0, n_pages, start, 0)
    lax.fori_loop(0, n_pages, wait, 0)


def ctx_gather(page_table_flat, ksum, pab, n_seq, n_pages):
    groups = PAGE // NSA_CMP_STRIDE
    gs = pltpu.PrefetchScalarGridSpec(
        num_scalar_prefetch=1,
        grid=(n_seq,),
        in_specs=[pl.BlockSpec(memory_space=pl.ANY), pl.BlockSpec(memory_space=pl.ANY)],
        out_specs=[pl.BlockSpec((1, n_pages) + ksum.shape[1:], lambda b, pt: (b,) + (0,) * ksum.ndim),
                   pl.BlockSpec((1, n_pages * groups, 256), lambda b, pt: (b, 0, 0))],
        scratch_shapes=[pltpu.SemaphoreType.DMA((2,))],
    )
    return pl.pallas_call(
        _ctx_gather_kernel,
        grid_spec=gs,
        out_shape=[jax.ShapeDtypeStruct((n_seq, n_pages) + ksum.shape[1:], F32),
                   jax.ShapeDtypeStruct((n_seq, n_pages * groups, 256), F32)],
        compiler_params=_cparams(("arbitrary",)),
        name="ctx_gather",
    )(page_table_flat, ksum, pab)


def _gelu_tanh(x):
    return 0.5 * x * (1.0 + jnp.tanh(math.sqrt(2.0 / math.pi) * (x + 0.044715 * (x * x * x))))


def _ctx_final_kernel(ks_ref, pab_ref, pos_ref, w1_ref, b1_ref, w2_ref, b2_ref, km_ref, cmp_ref, *, head_rows):
    nb = km_ref.shape[0]
    n_pages = 2 * nb
    r = lax.broadcasted_iota(jnp.int32, (nb, n_pages), 0)
    c = lax.broadcasted_iota(jnp.int32, (nb, n_pages), 1)
    pair = jnp.where((c == 2 * r) | (c == 2 * r + 1), 1.0, 0.0).astype(F32)
    if head_rows:
        ks = jnp.concatenate([ks_ref[pl.ds(h, n_pages, stride=2 * MOBA_HEADS), :] for h in range(MOBA_HEADS)], axis=1)
    else:
        ks = ks_ref[...]
    km_ref[...] = jnp.dot(pair, ks, precision=HI, preferred_element_type=F32) * (1.0 / MOBA_BLOCK)

    pab = pab_ref[...]
    ng = pab.shape[0]
    row = lax.broadcasted_iota(jnp.int32, (ng, 1), 0)
    outs = []
    for kv in range(2):
        pa = pab[:, 128 * kv:128 * kv + 64]
        pb = pab[:, 128 * kv + 64:128 * kv + 128]
        pb_next = jnp.where(row == ng - 1, 0.0, pltpu.roll(pb, ng - 1, 0))
        pos_term = _dot(pos_ref[kv], w1_ref[kv]) + b1_ref[kv]
        hid = _gelu_tanh(pa + pb_next + pos_term)
        outs.append(_dot(hid, w2_ref[kv]) + b2_ref[kv])
    cmp_ref[...] = jnp.concatenate(outs, axis=1)


def ctx_final(ksum, pab, pos, w1, b1, w2, b2, layer, n_pages):
    nb = ksum.shape[0]
    head_rows = ksum.shape[-1] == HEAD_DIM
    ng = pab.shape[1]
    lf = pos.shape[2] * pos.shape[3]
    pos2 = pos.reshape(pos.shape[0], 2, 1, lf)
    return pl.pallas_call(
        functools.partial(_ctx_final_kernel, head_rows=head_rows),
        grid=(nb,),
        in_specs=[pl.BlockSpec((None,) + ksum.shape[1:], lambda b: (b, 0, 0)),
                  pl.BlockSpec((None, ng, 256), lambda b: (b, 0, 0)),
                  pl.BlockSpec((None, 2, 1, lf), lambda b: (layer, 0, 0, 0)),
                  pl.BlockSpec((None, 2, lf, HEAD_DIM), lambda b: (layer, 0, 0, 0)),
                  pl.BlockSpec((None, 2, 1, HEAD_DIM), lambda b: (layer, 0, 0, 0)),
                  pl.BlockSpec((None, 2, HEAD_DIM, HEAD_DIM), lambda b: (layer, 0, 0, 0)),
                  pl.BlockSpec((None, 2, 1, HEAD_DIM), lambda b: (layer, 0, 0, 0))],
        out_specs=[pl.BlockSpec((None, n_pages // 2, 256), lambda b: (b, 0, 0)),
                   pl.BlockSpec((None, ng, 128), lambda b: (b, 0, 0))],
        out_shape=[jax.ShapeDtypeStruct((nb, n_pages // 2, 256), F32),
                   jax.ShapeDtypeStruct((nb, ng, 128), F32)],
        compiler_params=_cparams(("parallel",)),
        name="ctx_final",
    )(ksum, pab, pos2, w1, b1.reshape(b1.shape[0], 2, 1, HEAD_DIM), w2, b2.reshape(b2.shape[0], 2, 1, HEAD_DIM))


def _topk_lanes(score, k, lane_f, n_lanes):
    sel = jnp.zeros_like(score)
    for _ in range(k):
        m = jnp.max(score, axis=1, keepdims=True)
        idx = jnp.min(jnp.where(score == m, lane_f, float(n_lanes)), axis=1, keepdims=True)
        hit = lane_f == idx
        sel = jnp.where(hit & (m > -jnp.inf), 1.0, sel)
        score = jnp.where(hit, -jnp.inf, score)
    return sel


def _rel_bucket_np(dist):
    n = np.maximum(dist, 0)
    exact = N_BUCKETS // 2
    nf = np.maximum(n, 1).astype(np.float32)
    large = exact + (np.log(nf / np.float32(exact)) / np.float32(math.log(MAX_DISTANCE / exact))
                     * np.float32(N_BUCKETS - exact)).astype(np.int32)
    return np.where(n < exact, n, np.minimum(large, N_BUCKETS - 1)).astype(np.int32)


WIN_TILES = NSA_WINDOW // PAGE + 1


def _bucket_thresholds():
    b = _rel_bucket_np(np.arange(4 * MAX_DISTANCE))
    return [int(np.argmax(b >= k)) for k in range(N_BUCKETS)]


def _softmax_update(m_ref, l_ref, a_ref, idx, s, valid, pv_fn):
    s = jnp.where(valid, s, NEG)
    m_old = m_ref[idx]
    m_new = jnp.maximum(m_old, jnp.max(s, axis=1, keepdims=True))
    p = jnp.exp(s - m_new)
    alpha = jnp.exp(m_old - m_new)
    l_ref[idx] = alpha * l_ref[idx] + jnp.sum(p, axis=1, keepdims=True)
    a_ref[idx] = alpha * a_ref[idx] + pv_fn(p)
    m_ref[idx] = m_new


def _pv_slots(vs, pv_dot):
    def f(p):
        acc = pv_dot(p[:, 0:PAGE], vs[0])
        for j in range(1, len(vs)):
            acc = acc + pv_dot(p[:, PAGE * j:PAGE * (j + 1)], vs[j])
        return acc
    return f


def _attn_slots_kernel(si_ref, st_ref, sl_ref, pg_ref, *refs, q_rows, q0_base, n_slots, cache_tiles, tail_tile):
    Q, G = q_rows, n_slots
    rb_ref, qm_ref, qn_ref, gn_ref = refs[0:4]
    kv_refs = refs[4:4 + G]
    ns_refs = refs[4 + G:4 + 2 * G]
    tkv_ref, tns_ref = refs[4 + 2 * G:6 + 2 * G]
    win_refs = refs[6 + 2 * G:6 + 2 * G + WIN_TILES]
    base = 6 + 2 * G + WIN_TILES
    km_ref, cmp_ref, tbm_ref, tbn_ref, pool_ref, om_ref, on_ref = refs[base:base + 7]
    qms, qns, qbd, selm, sels, m_m, l_m, a_m, m_n, l_n, a_n, oc, ow = refs[base + 7:]

    s_id = pl.program_id(1)
    i = si_ref[s_id]
    t0 = st_ref[s_id]
    q0 = q0_base + i * Q
    ob = lax.shift_right_logical(q0, 8)
    nb = km_ref.shape[0]
    nbs_p = pool_ref.shape[1]
    nc = cmp_ref.shape[0]
    R4 = NSA_HEADS * Q

    qi = lax.broadcasted_iota(jnp.int32, (Q, 1), 0)
    qi4 = jnp.concatenate([qi] * NSA_HEADS, axis=0)
    kj = lax.broadcasted_iota(jnp.int32, (Q, PAGE), 1)
    lane_nb = lax.broadcasted_iota(jnp.int32, (Q, nb), 1)
    lane_bs = lax.broadcasted_iota(jnp.int32, (Q, nbs_p), 1)
    rep4 = lambda x: jnp.concatenate([x] * NSA_HEADS, axis=0)

    @pl.when(t0 == 0)
    def _init():
        qm = qm_ref[...]
        qn = qn_ref[...]
        qms[...] = qm * (HEAD_DIM ** -0.5)
        qbd[...] = jnp.zeros(qbd.shape, F32)
        for h in range(NSA_HEADS):
            sl = slice(HEAD_DIM * h, HEAD_DIM * (h + 1))
            qns[h * Q:(h + 1) * Q, :] = qn[:, sl] * (HEAD_DIM ** -0.5)
            qbd[h * Q:(h + 1) * Q, sl] = qm[:, sl] * (HEAD_DIM ** -0.5)
        qn4 = qns[...]
        km = km_ref[...]
        lane_f = lane_nb.astype(F32)
        for h in range(MOBA_HEADS):
            sl = slice(HEAD_DIM * h, HEAD_DIM * (h + 1))
            g = _dot_nt(qm[:, sl], km[:, sl], precision=HI)
            g = jnp.where(lane_nb < ob, g, -jnp.inf)
            selm[h] = _topk_lanes(g, min(MOBA_TOPK, nb), lane_f, nb)
        cm = cmp_ref[...]
        epos = lax.broadcasted_iota(jnp.int32, (Q, nc), 1) * NSA_CMP_STRIDE + (2 * NSA_CMP_STRIDE - 1)
        dc = q0 + qi - epos
        thr = _bucket_thresholds()
        bias = [jnp.full((Q, nc), rb_ref[N_BUCKETS - 1, MOBA_HEADS + h], F32) for h in range(NSA_HEADS)]
        for k in range(N_BUCKETS - 2, -1, -1):
            below = dc < thr[k + 1]
            bias = [jnp.where(below, rb_ref[k, MOBA_HEADS + h], bias[h]) for h in range(NSA_HEADS)]
        sc = _dot_nt(qn4, cm[:, :HEAD_DIM]) + jnp.concatenate(bias, axis=0)
        valid = rep4(dc >= 0)
        sc = jnp.where(valid, sc, NEG)
        mx = jnp.max(sc, axis=1, keepdims=True)
        e = jnp.where(valid, jnp.exp(sc - mx), 0.0)
        den = jnp.sum(e, axis=1, keepdims=True)
        p = e / jnp.where(den > 0, den, 1.0)
        oc[...] = _dot(p, cm[:, HEAD_DIM:])
        psum = p[0:Q] + p[Q:2 * Q] + p[2 * Q:3 * Q] + p[3 * Q:4 * Q]
        imp = jnp.dot(psum, pool_ref[...], precision=HI, preferred_element_type=F32)
        cur = lax.shift_right_logical(q0 + qi, 6)
        forced = (lane_bs == 0) | (lane_bs == cur) | (lane_bs == cur - 1)
        score = jnp.where(forced, jnp.inf, jnp.where(lane_bs <= cur, imp, -jnp.inf))
        sels[...] = _topk_lanes(score, NSA_TOPN, lane_bs.astype(F32), nbs_p)
        tw0 = lax.shift_right_logical(q0, 7) - (WIN_TILES - 1)
        s_parts, v_parts, m_parts = [], [], []
        for w in range(WIN_TILES):
            wv = win_refs[w][...]
            back = WIN_TILES - 1 - w
            dw = back * PAGE + qi - kj
            s_parts.append(_dot_nt(qn4, wv[:, :HEAD_DIM]) + tbn_ref[min(back, 2)])
            m_parts.append((dw >= 0) & (dw < NSA_WINDOW) & (tw0 + w >= 0))
            v_parts.append(wv[:, HEAD_DIM:])
        sw = jnp.concatenate(s_parts, axis=1)
        wvalid = rep4(jnp.concatenate(m_parts, axis=1))
        sw = jnp.where(wvalid, sw, NEG)
        mx = jnp.max(sw, axis=1, keepdims=True)
        e = jnp.where(wvalid, jnp.exp(sw - mx), 0.0)
        den = jnp.sum(e, axis=1, keepdims=True)
        pw = e / jnp.where(den > 0, den, 1.0)
        o_w = _dot(pw[:, 0:PAGE], v_parts[0])
        for w in range(1, WIN_TILES):
            o_w = o_w + _dot(pw[:, PAGE * w:PAGE * (w + 1)], v_parts[w])
        ow[...] = o_w
        m_m[...] = jnp.full(m_m.shape, NEG, F32)
        l_m[...] = jnp.zeros(l_m.shape, F32)
        a_m[...] = jnp.zeros(a_m.shape, F32)
        m_n[...] = jnp.full(m_n.shape, NEG, F32)
        l_n[...] = jnp.zeros(l_n.shape, F32)
        a_n[...] = jnp.zeros(a_n.shape, F32)

    is_tail = (t0 == tail_tile) if tail_tile is not None else None

    def pick(j, cache_fn, tail_fn):
        x = cache_fn(kv_refs[j], ns_refs[j])
        if is_tail is not None and j == 0:
            x = jnp.where(is_tail, tail_fn(tkv_ref, tns_ref), x)
        return x

    if cache_tiles:
        k_of = lambda j, h: pick(j, *[lambda kv, ns: kv[0, h]] * 2)
        v_of = lambda j, h: pick(j, *[lambda kv, ns: kv[1, h]] * 2)
        ksel_of = lambda j: pick(j, *[lambda kv, ns: ns[0]] * 2)
        vsel_of = lambda j: pick(j, *[lambda kv, ns: ns[1]] * 2)
        qk, pv = _dot, _dot_nt
    else:
        k_of = lambda j, h: kv_refs[j][:, HEAD_DIM * h:HEAD_DIM * (h + 1)]
        v_of = lambda j, h: kv_refs[j][:, 256 + HEAD_DIM * h:256 + HEAD_DIM * (h + 1)]
        ksel_of = lambda j: ns_refs[j][:, 0:HEAD_DIM]
        vsel_of = lambda j: ns_refs[j][:, HEAD_DIM:2 * HEAD_DIM]
        qk, pv = _dot_nt, _dot

    causal, v_idx, blk = [], [], []
    for j in range(G):
        delta = q0 - (t0 + j) * PAGE
        causal.append(delta + qi - kj >= 0)
        v_idx.append(jnp.clip(lax.shift_right_arithmetic(delta, 7), 0, 2))
        blk.append(lax.shift_right_logical(t0 + j, 1))

    moba_sel = [[jnp.sum(jnp.where(lane_nb == blk[j], selm[h], 0.0), axis=1, keepdims=True) > 0.0
                 for h in range(MOBA_HEADS)] for j in range(G)]
    if cache_tiles:
        assert G % 2 == 0
        kv_all = lambda j, w: pick(j, *[lambda kv, ns: kv[w].reshape(MOBA_HEADS * HEAD_DIM, PAGE)] * 2)
        q_bd = qbd[...]
        s_parts, m_parts = [], []
        for j in range(0, G, 2):
            s_parts.append(_dot(q_bd, jnp.concatenate([kv_all(j, 0), kv_all(j + 1, 0)], axis=1)))
        for j in range(G):
            valid_j = jnp.concatenate([causal[j] & (moba_sel[j][h] | (blk[j] == ob)) for h in range(MOBA_HEADS)], axis=0)
            m_parts.append(valid_j)
        bias = jnp.concatenate([tbm_ref[v_idx[j]] for j in range(G)], axis=1)

        def pv_heads(p):
            acc = None
            for j in range(0, G, 2):
                o2 = _dot_nt(p[:, PAGE * j:PAGE * (j + 2)],
                             jnp.concatenate([kv_all(j, 1), kv_all(j + 1, 1)], axis=1))
                diag = jnp.concatenate([o2[h * Q:(h + 1) * Q, HEAD_DIM * h:HEAD_DIM * (h + 1)]
                                        for h in range(MOBA_HEADS)], axis=0)
                acc = diag if acc is None else acc + diag
            return acc

        _softmax_update(m_m, l_m, a_m, 0, jnp.concatenate(s_parts, axis=1) + bias, jnp.concatenate(m_parts, axis=1),
                        pv_heads)
    else:
        q_all = qms[...]
        for h in range(MOBA_HEADS):
            sl = slice(HEAD_DIM * h, HEAD_DIM * (h + 1))
            s_parts, m_parts = [], []
            for j in range(G):
                s_parts.append(qk(q_all[:, sl], k_of(j, h)) + tbm_ref[v_idx[j], h * Q:(h + 1) * Q, :])
                m_parts.append(causal[j] & (moba_sel[j][h] | (blk[j] == ob)))
            _softmax_update(m_m, l_m, a_m, h, jnp.concatenate(s_parts, axis=1), jnp.concatenate(m_parts, axis=1),
                            _pv_slots([v_of(j, h) for j in range(G)], pv))

    qn4 = qns[...]
    sel_all = sels[...]
    s_parts, m_parts = [], []
    for j in range(G):
        s_parts.append(qk(qn4, ksel_of(j)) + tbn_ref[v_idx[j]])
        lo = jnp.sum(jnp.where(lane_bs == 2 * (t0 + j), sel_all, 0.0), axis=1, keepdims=True)
        hi = jnp.sum(jnp.where(lane_bs == 2 * (t0 + j) + 1, sel_all, 0.0), axis=1, keepdims=True)
        m_parts.append(causal[j] & (jnp.where(kj < NSA_SEL_BLOCK, lo, hi) > 0.0))
    _softmax_update(m_n, l_n, a_n, 0, jnp.concatenate(s_parts, axis=1), rep4(jnp.concatenate(m_parts, axis=1)),
                    _pv_slots([vsel_of(j) for j in range(G)], pv))

    @pl.when(sl_ref[s_id] == 1)
    def _finish():
        outs = []
        for h in range(MOBA_HEADS):
            if cache_tiles:
                l = l_m[0, h * Q:(h + 1) * Q, :]
                a = a_m[0, h * Q:(h + 1) * Q, :]
            else:
                l, a = l_m[h], a_m[h]
            outs.append(a / jnp.where(l > 0, l, 1.0))
        om_ref[...] = jnp.concatenate(outs, axis=1)
        l = l_n[0]
        o_s = a_n[0] / jnp.where(l > 0, l, 1.0)
        o_w = ow[...]
        o_c = oc[...]
        gn = gn_ref[...]
        outs = []
        for h in range(NSA_HEADS):
            r = slice(h * Q, (h + 1) * Q)
            outs.append(gn[:, 3 * h:3 * h + 1] * o_c[r] + gn[:, 3 * h + 1:3 * h + 2] * o_s[r]
                        + gn[:, 3 * h + 2:3 * h + 3] * o_w[r])
        on_ref[...] = jnp.concatenate(outs, axis=1)


def bias_tiles(rel_bias, q_rows):
    Q = q_rows
    qi = np.arange(Q)[:, None]
    kj = np.arange(PAGE)[None, :]
    d = jnp.asarray(np.stack([v * PAGE + qi - kj for v in range(3)]))[..., None]
    thr = _bucket_thresholds()
    tb = jnp.broadcast_to(rel_bias[N_BUCKETS - 1], d.shape[:-1] + (rel_bias.shape[1],))
    for k in range(N_BUCKETS - 2, -1, -1):
        tb = jnp.where(d < thr[k + 1], rel_bias[k], tb)
    stack = lambda t: t.transpose(0, 3, 1, 2).reshape(3, t.shape[-1] * Q, PAGE)
    return stack(tb[..., :MOBA_HEADS]), stack(tb[..., MOBA_HEADS:])


def attention_slots(rel_bias, qm, qn, gn, q_blk0, kv_src, ns_src, page_idx, tail_kv, tail_ns, win_pages, win_w0t,
                    kmean, cmp, tiles, *, n_seq, q_rows, n_chunks, q0_base, tiles_per_seq, n_win_tiles,
                    n_sel_blocks, n_slots, cache_layer=None):
    Q, G = q_rows, n_slots
    cache_tiles = cache_layer is not None
    assert PAGE % Q == 0 and q0_base % PAGE == 0 and (n_chunks == 1 or Q == PAGE)
    tbm, tbn = tiles
    nb = kmean.shape[1]
    nc = cmp.shape[1]
    nbs_p = -(-n_sel_blocks // 128) * 128
    per = NSA_SEL_BLOCK // NSA_CMP_STRIDE
    pool = jnp.asarray((np.arange(nc)[:, None] // per == np.arange(nbs_p)[None, :]).astype(np.float32))
    has_tail = tail_kv is not None

    steps = []
    for i in range(n_chunks):
        n_tiles = (q0_base + i * Q + Q - 1) // PAGE + 1
        n_st = -(-n_tiles // G)
        for s in range(n_st):
            steps.append((i, s * G, int(s == n_st - 1)))
    steps = np.asarray(steps, np.int32)
    n_steps = steps.shape[0]
    tail_tile = tiles_per_seq if has_tail else None
    assert tail_tile is None or tail_tile % G == 0
    if cache_tiles:
        assert has_tail
        kv_blk = (None, None, 2, MOBA_HEADS, HEAD_DIM, PAGE)
        ns_blk = (None, None, 2, HEAD_DIM, PAGE)
        tail_kv_blk, tail_ns_blk = kv_blk[1:], ns_blk[1:]
    else:
        assert not has_tail
        kv_blk = tail_kv_blk = (None, PAGE, 512)
        ns_blk = tail_ns_blk = (None, PAGE, 128)
        tail_kv = jnp.zeros((1, PAGE, 512), F32)
        tail_ns = jnp.zeros((1, PAGE, 256), F32)

    def q_map(b, s, si, st, sl, pg):
        return (q_blk0 + b * n_chunks + si[s], 0)

    def o_map(b, s, si, st, sl, pg):
        return (b * n_chunks + si[s], 0)

    def page_of(b, s, st, pg, j):
        return pg[b * tiles_per_seq + jnp.minimum(st[s] + j, tiles_per_seq - 1)]

    def kv_map(j):
        if cache_tiles:
            return lambda b, s, si, st, sl, pg: (cache_layer, page_of(b, s, st, pg, j), 0, 0, 0, 0)
        return lambda b, s, si, st, sl, pg: (page_of(b, s, st, pg, j), 0, 0)

    def ns_map(j):
        if cache_tiles:
            return lambda b, s, si, st, sl, pg: (cache_layer, page_of(b, s, st, pg, j), 1, 0, 0)
        return lambda b, s, si, st, sl, pg: (page_of(b, s, st, pg, j), 0, 1)

    def tail_map(nd):
        if cache_tiles:
            return lambda b, s, si, st, sl, pg: (b,) + (0,) * (nd - 1)
        return lambda b, s, si, st, sl, pg: (0, 0, 0) if nd == 0 else (0, 0, 1)

    def win_map(w):
        def f(b, s, si, st, sl, pg):
            tw = lax.shift_right_logical(q0_base + si[s] * Q, 7) - (WIN_TILES - 1) + w - win_w0t
            return (b * n_win_tiles + jnp.clip(tw, 0, n_win_tiles - 1), 0, 0)
        return f

    const = lambda nd: (lambda b, s, si, st, sl, pg: (0,) * nd)
    moba_state = (1, MOBA_HEADS * Q) if cache_tiles else (MOBA_HEADS, Q)
    in_specs = ([pl.BlockSpec(memory_space=pltpu.SMEM),
                 pl.BlockSpec((Q, 256), q_map), pl.BlockSpec((Q, 256), q_map), pl.BlockSpec((Q, 128), q_map)]
                + [pl.BlockSpec(kv_blk, kv_map(j)) for j in range(G)]
                + [pl.BlockSpec(ns_blk, ns_map(j)) for j in range(G)]
                + [pl.BlockSpec(tail_kv_blk, tail_map(5 if cache_tiles else 0)),
                   pl.BlockSpec(tail_ns_blk, tail_map(4 if cache_tiles else 1))]
                + [pl.BlockSpec((None, PAGE, 128), win_map(w)) for w in range(WIN_TILES)]
                + [pl.BlockSpec((None, nb, 256), lambda b, s, si, st, sl, pg: (b, 0, 0)),
                   pl.BlockSpec((None, nc, 128), lambda b, s, si, st, sl, pg: (b, 0, 0)),
                   pl.BlockSpec((3, MOBA_HEADS * Q, PAGE), const(3)),
                   pl.BlockSpec((3, NSA_HEADS * Q, PAGE), const(3)),
                   pl.BlockSpec((nc, nbs_p), const(2))])
    gs = pltpu.PrefetchScalarGridSpec(
        num_scalar_prefetch=4,
        grid=(n_seq, n_steps),
        in_specs=in_specs,
        out_specs=[pl.BlockSpec((Q, 256), o_map), pl.BlockSpec((Q, 256), o_map)],
        scratch_shapes=[
            pltpu.VMEM((Q, 256), F32),
            pltpu.VMEM((NSA_HEADS * Q, HEAD_DIM), F32),
            pltpu.VMEM((MOBA_HEADS * Q, 256), F32),
            pltpu.VMEM((MOBA_HEADS, Q, nb), F32),
            pltpu.VMEM((Q, nbs_p), F32),
            pltpu.VMEM(moba_state + (1,), F32),
            pltpu.VMEM(moba_state + (1,), F32),
            pltpu.VMEM(moba_state + (HEAD_DIM,), F32),
            pltpu.VMEM((1, NSA_HEADS * Q, 1), F32),
            pltpu.VMEM((1, NSA_HEADS * Q, 1), F32),
            pltpu.VMEM((1, NSA_HEADS * Q, HEAD_DIM), F32),
            pltpu.VMEM((NSA_HEADS * Q, HEAD_DIM), F32),
            pltpu.VMEM((NSA_HEADS * Q, HEAD_DIM), F32),
        ],
    )
    n_tok = n_seq * n_chunks * Q
    return pl.pallas_call(
        functools.partial(_attn_slots_kernel, q_rows=Q, q0_base=q0_base, n_slots=G, cache_tiles=cache_tiles,
                          tail_tile=tail_tile),
        grid_spec=gs,
        out_shape=[jax.ShapeDtypeStruct((n_tok, 256), F32), jax.ShapeDtypeStruct((n_tok, 256), F32)],
        compiler_params=_cparams(("parallel", "arbitrary")),
        name="attention",
    )(jnp.asarray(steps[:, 0]), jnp.asarray(steps[:, 1]), jnp.asarray(steps[:, 2]), page_idx, rel_bias,
      qm, qn, gn, *([kv_src] * G), *([ns_src] * G), tail_kv, tail_ns, *([win_pages] * WIN_TILES),
      kmean, cmp, tbm, tbn, pool)


def _softplus(x):
    return jnp.maximum(x, 0.0) + jnp.log(1.0 + jnp.exp(-jnp.abs(x)))


def _ssd_kernel(xbc_ref, z_ref, dt_ref, dtt_ref, cprev_ref, st0_ref, cw_ref, cb_ref, dtb_ref, dtbt_ref,
                al_ref, alt_ref, dd_ref, nw_ref, y_ref, st_ref, xp, *, valid_len):
    c = pl.program_id(1)
    cl = xbc_ref.shape[0]
    gn = SSD_GROUPS * SSD_STATE

    @pl.when(c == 0)
    def _first():
        xp[0:8, :] = cprev_ref[...]
        st_ref[...] = st0_ref[...]

    xp[8:8 + cl, :] = xbc_ref[...]
    conv = cb_ref[...] + cw_ref[0:1, :] * xp[5:5 + cl, :]
    for k in range(1, SSD_CONV):
        conv = conv + cw_ref[k:k + 1, :] * xp[5 + k:5 + k + cl, :]
    xp[0:8, :] = xp[cl:cl + 8, :]
    act = _silu(conv)
    xs = act[:, :SSD_INNER]
    bm = act[:, SSD_INNER:SSD_INNER + gn]
    cm = act[:, SSD_INNER + gn:]

    row = lax.broadcasted_iota(jnp.int32, (cl, cl), 0)
    col = lax.broadcasted_iota(jnp.int32, (cl, cl), 1)
    tri = row >= col
    pos_r = c * cl + lax.broadcasted_iota(jnp.int32, (cl, 1), 0)
    pos_c = c * cl + lax.broadcasted_iota(jnp.int32, (1, cl), 1)
    dt = jnp.where(pos_r < valid_len, _softplus(dt_ref[...] + dtb_ref[...]), 0.0)
    dtt = jnp.where(pos_c < valid_len, _softplus(dtt_ref[...] + dtbt_ref[...]), 0.0)
    da = dt * (-jnp.exp(al_ref[...]))
    dat = dtt * (-jnp.exp(alt_ref[...]))
    acum = jnp.dot(jnp.where(tri, 1.0, 0.0), da, precision=HI, preferred_element_type=F32)
    acumt = jnp.dot(dat, jnp.where(row <= col, 1.0, 0.0), precision=HI, preferred_element_type=F32)

    cbs = [_dot_nt(cm[:, SSD_STATE * g:SSD_STATE * (g + 1)], bm[:, SSD_STATE * g:SSD_STATE * (g + 1)])
           for g in range(SSD_GROUPS)]
    ys = []
    for h in range(SSD_HEADS):
        g = h // (SSD_HEADS // SSD_GROUPS)
        bg = bm[:, SSD_STATE * g:SSD_STATE * (g + 1)]
        cg = cm[:, SSD_STATE * g:SSD_STATE * (g + 1)]
        a_col = acum[:, h:h + 1]
        a_row = acumt[h:h + 1, :]
        a_last = acumt[h:h + 1, cl - 1:cl]
        lmat = jnp.exp(jnp.where(tri, a_col - a_row, -jnp.inf))
        xh = xs[:, SSD_HEAD_DIM * h:SSD_HEAD_DIM * (h + 1)]
        xdt = xh * dt[:, h:h + 1]
        y_diag = _dot(cbs[g] * lmat, xdt)
        prev = st_ref[h]
        y_off = jnp.exp(a_col) * _dot_nt(cg, prev)
        decay = jnp.exp(a_last - a_col)
        upd = lax.dot_general(xdt.astype(BF16), (bg * decay).astype(BF16), (((0,), (0,)), ((), ())),
                              preferred_element_type=F32)
        st_ref[h] = prev * jnp.exp(a_last) + upd
        ys.append(y_diag + y_off + dd_ref[:, h:h + 1] * xh)
    y = jnp.concatenate(ys, axis=1) * _silu(z_ref[...])
    half = SSD_INNER // SSD_GROUPS
    outs = []
    for g in range(SSD_GROUPS):
        yg = y[:, half * g:half * (g + 1)]
        outs.append(yg * lax.rsqrt(jnp.mean(yg * yg, axis=-1, keepdims=True) + EPS))
    y_ref[...] = jnp.concatenate(outs, axis=1) * nw_ref[...]


def ssd(xbc, z, dt, dt_t, conv_prev8, state0, cw, cb, dtb, alog, dd, nw, *, n_seq, seq_rows, chunk, valid_len):
    n_ch = seq_rows // chunk

    def pad128(v):
        return jnp.pad(v.reshape(1, -1), ((0, 0), (0, 128 - v.shape[-1])))

    tok = lambda w: pl.BlockSpec((chunk, w), lambda b, c: (b * n_ch + c, 0))
    full = lambda shp: pl.BlockSpec(shp, lambda b, c: (0,) * len(shp))
    return pl.pallas_call(
        functools.partial(_ssd_kernel, valid_len=valid_len),
        grid=(n_seq, n_ch),
        in_specs=[tok(SSD_CONV_DIM), tok(SSD_INNER), tok(128),
                  pl.BlockSpec((None, SSD_HEADS, chunk), lambda b, c: (b, 0, c)),
                  pl.BlockSpec((None, 8, SSD_CONV_DIM), lambda b, c: (b, 0, 0)),
                  pl.BlockSpec((None, SSD_HEADS, SSD_HEAD_DIM, SSD_STATE), lambda b, c: (b, 0, 0, 0)),
                  full((SSD_CONV, SSD_CONV_DIM)), full((1, SSD_CONV_DIM)), full((1, 128)), full((SSD_HEADS, 1)),
                  full((1, 128)), full((SSD_HEADS, 1)), full((1, 128)), full((1, SSD_INNER))],
        out_specs=[tok(SSD_INNER),
                   pl.BlockSpec((None, SSD_HEADS, SSD_HEAD_DIM, SSD_STATE), lambda b, c: (b, 0, 0, 0))],
        out_shape=[jax.ShapeDtypeStruct((n_seq * seq_rows, SSD_INNER), F32),
                   jax.ShapeDtypeStruct((n_seq, SSD_HEADS, SSD_HEAD_DIM, SSD_STATE), F32)],
        scratch_shapes=[pltpu.VMEM((chunk + 8, SSD_CONV_DIM), F32)],
        compiler_params=_cparams(("parallel", "arbitrary")),
        name="ssd",
    )(xbc, z, dt, dt_t, conv_prev8, state0, cw, cb.reshape(1, -1), pad128(dtb), dtb.reshape(-1, 1),
      pad128(alog), alog.reshape(-1, 1), pad128(dd), nw.reshape(1, -1))


def _mix_kernel(om_ref, on_ref, ys_ref, gbr_ref, x_ref, g1_ref, sc2_ref, sh2_ref, nw1_ref, nw2_ref,
                wbm_ref, wbn_ref, wbs_ref, wo_ref, rwt_ref, x1_ref, h2_ref, h2t_ref, lg_ref):
    d = D_MODEL
    ya = _dot(om_ref[...], wbm_ref[...])
    yb = _dot(on_ref[...], wbn_ref[...])
    yc = _dot(ys_ref[...], wbs_ref[...])
    merged = gbr_ref[:, 0:d] * ya + gbr_ref[:, d:2 * d] * yb + gbr_ref[:, 2 * d:3 * d] * yc
    m = _dot(merged, wo_ref[...])
    x1 = x_ref[...] + g1_ref[...] * _rms(m, nw1_ref[...])
    h2 = _rms(x1, nw2_ref[...]) * (1.0 + sc2_ref[...]) + sh2_ref[...]
    x1_ref[...] = x1
    h2_ref[...] = h2
    for s in range(ROW_TILES):
        h2t_ref[pl.ds(s, TM, stride=ROW_TILES), :] = h2[:, 128 * s:128 * (s + 1)]
    lg_ref[...] = _dot_nt(rwt_ref[...], h2, precision=HI)


def mix_out(om, on, ys, gbr, x, modx, nw1, nw2, wbm, wbn, wbs, wo, rwt, group_of_tile):
    n, d = x.shape
    tok = lambda w: pl.BlockSpec((TM, w), lambda i: (i, 0))
    full = lambda a: pl.BlockSpec(a.shape, lambda i: (0,) * a.ndim, pipeline_mode=pl.Buffered(1))
    return pl.pallas_call(
        _mix_kernel,
        grid=(n // TM,),
        in_specs=[tok(256), tok(256), tok(SSD_INNER), tok(3 * d), tok(d),
                  _mod_spec(2, group_of_tile), _mod_spec(4, group_of_tile), _mod_spec(3, group_of_tile),
                  full(nw1), full(nw2), full(wbm), full(wbn), full(wbs), full(wo), full(rwt)],
        out_specs=[tok(d), tok(d), pl.BlockSpec((TM * ROW_TILES, 128), lambda i: (i, 0)),
                   pl.BlockSpec((N_EXPERTS, TM), lambda i: (0, i))],
        out_shape=[jax.ShapeDtypeStruct((n, d), F32), jax.ShapeDtypeStruct((n, d), F32),
                   jax.ShapeDtypeStruct((n * ROW_TILES, 128), F32), jax.ShapeDtypeStruct((N_EXPERTS, n), F32)],
        compiler_params=_cparams(("parallel",)),
        name="mix_out",
    )(om, on, ys, gbr, x, modx, modx, modx, nw1, nw2, wbm, wbn, wbs, wo, rwt)


def _router_kernel(lg_ref, rb_ref, eidx_ref, w8_ref, pos_ref, cnt_ref, carry):
    i = pl.program_id(0)
    tm = lg_ref.shape[1]
    per = N_EXPERTS // N_ROUTE_GROUPS

    @pl.when(i == 0)
    def _zero():
        carry[...] = jnp.zeros(carry.shape, F32)

    s = _sigmoid(lg_ref[...])
    sc = s + rb_ref[...]
    sub = lax.broadcasted_iota(jnp.int32, (per, tm), 0).astype(F32)
    gs_rows = []
    for g in range(N_ROUTE_GROUPS):
        x = sc[per * g:per * (g + 1), :]
        m1 = jnp.max(x, axis=0, keepdims=True)
        i1 = jnp.min(jnp.where(x == m1, sub, float(per)), axis=0, keepdims=True)
        m2 = jnp.max(jnp.where(sub == i1, -jnp.inf, x), axis=0, keepdims=True)
        gs_rows.append(m1 + m2)
    gs = jnp.concatenate(gs_rows, axis=0)
    gsel = jnp.zeros_like(gs)
    for _ in range(TOPK_GROUPS):
        m = jnp.max(gs, axis=0, keepdims=True)
        ix = jnp.min(jnp.where(gs == m, sub, float(N_ROUTE_GROUPS)), axis=0, keepdims=True)
        hit = sub == ix
        gsel = jnp.where(hit, 1.0, gsel)
        gs = jnp.where(hit, -jnp.inf, gs)
    emask = jnp.concatenate([jnp.broadcast_to(gsel[g:g + 1, :], (per, tm)) for g in range(N_ROUTE_GROUPS)], axis=0)
    msc = jnp.where(emask > 0.0, sc, -jnp.inf)
    e_io = lax.broadcasted_iota(jnp.int32, (N_EXPERTS, tm), 0).astype(F32)
    sel = jnp.zeros_like(sc)
    idxs = []
    for _ in range(TOP_K):
        m = jnp.max(msc, axis=0, keepdims=True)
        ix = jnp.min(jnp.where(msc == m, e_io, float(N_EXPERTS)), axis=0, keepdims=True)
        hit = e_io == ix
        sel = jnp.where(hit, 1.0, sel)
        msc = jnp.where(hit, -jnp.inf, msc)
        idxs.append(ix)
    w = s * sel
    wn = w / jnp.sum(w, axis=0, keepdims=True) * ROUTE_SCALE
    r = lax.broadcasted_iota(jnp.int32, (tm, tm), 0)
    cidx = lax.broadcasted_iota(jnp.int32, (tm, tm), 1)
    upper = jnp.where(r <= cidx, 1.0, 0.0)
    cum = _dot(sel, upper)
    rank = cum - sel + carry[:, 0:1]
    eidx_rows, w_rows, p_rows = [], [], []
    for k in range(TOP_K):
        hit = e_io == idxs[k]
        eidx_rows.append(idxs[k])
        w_rows.append(jnp.sum(jnp.where(hit, wn, 0.0), axis=0, keepdims=True))
        p_rows.append(jnp.sum(jnp.where(hit, rank, 0.0), axis=0, keepdims=True))
    eidx_ref[...] = jnp.concatenate(eidx_rows, axis=0).astype(jnp.int32)
    w8_ref[...] = jnp.concatenate(w_rows, axis=0)
    pos_ref[...] = jnp.concatenate(p_rows, axis=0).astype(jnp.int32)
    carry[...] = carry[...] + jnp.sum(sel, axis=1, keepdims=True)
    cnt_ref[...] = carry[...]


def router(logits_t, router_b):
    ne, n = logits_t.shape
    return pl.pallas_call(
        _router_kernel,
        grid=(n // TM,),
        in_specs=[pl.BlockSpec((ne, TM), lambda i: (0, i)), pl.BlockSpec((ne, 1), lambda i: (0, 0))],
        out_specs=[pl.BlockSpec((TOP_K, TM), lambda i: (0, i)), pl.BlockSpec((TOP_K, TM), lambda i: (0, i)),
                   pl.BlockSpec((TOP_K, TM), lambda i: (0, i)), pl.BlockSpec((ne, 128), lambda i: (0, 0))],
        out_shape=[jax.ShapeDtypeStruct((TOP_K, n), jnp.int32), jax.ShapeDtypeStruct((TOP_K, n), F32),
                   jax.ShapeDtypeStruct((TOP_K, n), jnp.int32), jax.ShapeDtypeStruct((ne, 128), F32)],
        scratch_shapes=[pltpu.VMEM((ne, 128), F32)],
        compiler_params=_cparams(("arbitrary",)),
        name="router",
    )(logits_t, router_b.reshape(ne, 1))


def _dispatch_kernel(dest_ref, h_ref, init_ref, rows_ref, sem):
    del init_ref
    n_pairs = dest_ref.shape[1]

    def copy(j):
        src = pl.multiple_of(lax.shift_right_logical(j, 3) * ROW_TILES, ROW_TILES)
        dst = pl.multiple_of(dest_ref[0, j] * ROW_TILES, ROW_TILES)
        return pltpu.make_async_copy(h_ref.at[pl.ds(src, ROW_TILES)], rows_ref.at[pl.ds(dst, ROW_TILES)], sem)

    def start(j, carry):
        copy(j).start()
        return carry

    def wait(j, carry):
        copy(j).wait()
        return carry

    lax.fori_loop(0, n_pairs, start, 0, unroll=8)
    lax.fori_loop(0, n_pairs, wait, 0, unroll=8)


def dispatch(h2t, dest, n_rows):
    n = h2t.shape[0] // ROW_TILES
    n_tiles = n // TM
    rows0 = jnp.zeros((n_rows * ROW_TILES, 128), F32)
    return pl.pallas_call(
        _dispatch_kernel,
        grid=(n_tiles,),
        in_specs=[pl.BlockSpec((None, 1, TM * TOP_K), lambda i: (i, 0, 0), memory_space=pltpu.SMEM),
                  pl.BlockSpec((TM * ROW_TILES, 128), lambda i: (i, 0)),
                  pl.BlockSpec(memory_space=pl.ANY)],
        out_specs=pl.BlockSpec(memory_space=pl.ANY),
        out_shape=jax.ShapeDtypeStruct((n_rows * ROW_TILES, 128), F32),
        scratch_shapes=[pltpu.SemaphoreType.DMA(())],
        input_output_aliases={2: 0},
        compiler_params=_cparams(("arbitrary",)),
        name="dispatch",
    )(dest.reshape(n_tiles, 1, TM * TOP_K), h2t, rows0)


def _expert_kernel(be_ref, nu_ref, x_ref, wg_ref, wu_ref, wd_ref, y_ref):
    @pl.when(pl.program_id(0) < nu_ref[0])
    def _():
        de = wg_ref.shape[1]
        g = jnp.zeros((EXPERT_ROWS, de), F32)
        u = jnp.zeros((EXPERT_ROWS, de), F32)
        for s in range(0, ROW_TILES, 2):
            xs = jnp.concatenate([x_ref[pl.ds(s, EXPERT_ROWS, stride=ROW_TILES), :],
                                  x_ref[pl.ds(s + 1, EXPERT_ROWS, stride=ROW_TILES), :]], axis=1).astype(BF16)
            g = g + jnp.dot(xs, wg_ref[128 * s:128 * (s + 2), :].astype(BF16), preferred_element_type=F32)
            u = u + jnp.dot(xs, wu_ref[128 * s:128 * (s + 2), :].astype(BF16), preferred_element_type=F32)
        y = _dot(_silu(g) * u, wd_ref[...])
        for s in range(ROW_TILES):
            y_ref[pl.ds(s, EXPERT_ROWS, stride=ROW_TILES), :] = y[:, 128 * s:128 * (s + 1)]

    @pl.when(pl.program_id(0) >= nu_ref[0])
    def _():
        y_ref[...] = jnp.zeros(y_ref.shape, F32)


def experts(x_rows, blk_e, n_used, wg, wu, wd, layer):
    n_rows = x_rows.shape[0] // ROW_TILES
    n_blk = n_rows // EXPERT_ROWS
    ne, d, de = wg.shape[1:]
    wg2 = wg.reshape(-1, d, de)
    wu2 = wu.reshape(-1, d, de)
    wd2 = wd.reshape(-1, de, d)
    gs = pltpu.PrefetchScalarGridSpec(
        num_scalar_prefetch=2,
        grid=(n_blk,),
        in_specs=[pl.BlockSpec((EXPERT_ROWS * ROW_TILES, 128), lambda i, be, nu: (i, 0)),
                  pl.BlockSpec((None, d, de), lambda i, be, nu: (layer * ne + be[i], 0, 0)),
                  pl.BlockSpec((None, d, de), lambda i, be, nu: (layer * ne + be[i], 0, 0)),
                  pl.BlockSpec((None, de, d), lambda i, be, nu: (layer * ne + be[i], 0, 0))],
        out_specs=pl.BlockSpec((EXPERT_ROWS * ROW_TILES, 128), lambda i, be, nu: (i, 0)),
    )
    return pl.pallas_call(
        _expert_kernel,
        grid_spec=gs,
        out_shape=jax.ShapeDtypeStruct((n_rows * ROW_TILES, 128), F32),
        compiler_params=_cparams(("arbitrary",)),
        name="experts",
    )(blk_e, n_used, x_rows, wg2, wu2, wd2)


def _combine_kernel(dest_ref, w8_ref, x1_ref, h2_ref, g2_ref, nw_ref, wsg_ref, wsu_ref, wsd_ref, yrows_ref,
                    out_ref, buf, sem):
    n_pairs = dest_ref.shape[1]

    def copy(j):
        src = pl.multiple_of(dest_ref[0, j] * ROW_TILES, ROW_TILES)
        dst = pl.multiple_of(lax.shift_right_logical(j, 3) * ROW_TILES, ROW_TILES)
        k = jnp.bitwise_and(j, TOP_K - 1)
        return pltpu.make_async_copy(yrows_ref.at[pl.ds(src, ROW_TILES)], buf.at[k, pl.ds(dst, ROW_TILES)], sem)

    def start(j, carry):
        copy(j).start()
        return carry

    def wait(j, carry):
        copy(j).wait()
        return carry

    lax.fori_loop(0, n_pairs, start, 0, unroll=8)
    h2 = h2_ref[...]
    shared = _dot(_silu(_dot(h2, wsg_ref[...])) * _dot(h2, wsu_ref[...]), wsd_ref[...])
    lax.fori_loop(0, n_pairs, wait, 0, unroll=8)
    w8 = w8_ref[...]
    pieces = []
    for s in range(ROW_TILES):
        acc = w8[:, 0:1] * buf[0, pl.ds(s, TM, stride=ROW_TILES), :]
        for k in range(1, TOP_K):
            acc = acc + w8[:, k:k + 1] * buf[k, pl.ds(s, TM, stride=ROW_TILES), :]
        pieces.append(acc)
    routed = jnp.concatenate(pieces, axis=1)
    out_ref[...] = x1_ref[...] + g2_ref[...] * _rms(routed + shared, nw_ref[...])


def combine(dest, w8, x1, h2, modx, nw3, wsg, wsu, wsd, y_rows, group_of_tile):
    n, d = x1.shape
    n_tiles = n // TM
    tok = lambda w: pl.BlockSpec((TM, w), lambda i: (i, 0))
    full = lambda a: pl.BlockSpec(a.shape, lambda i: (0,) * a.ndim)
    return pl.pallas_call(
        _combine_kernel,
        grid=(n_tiles,),
        in_specs=[pl.BlockSpec((None, 1, TM * TOP_K), lambda i: (i, 0, 0), memory_space=pltpu.SMEM),
                  tok(TOP_K), tok(d), tok(d), _mod_spec(5, group_of_tile),
                  full(nw3), full(wsg), full(wsu), full(wsd),
                  pl.BlockSpec(memory_space=pl.ANY)],
        out_specs=tok(d),
        out_shape=jax.ShapeDtypeStruct((n, d), F32),
        scratch_shapes=[pltpu.VMEM((TOP_K, TM * ROW_TILES, 128), F32), pltpu.SemaphoreType.DMA(())],
        compiler_params=_cparams(("arbitrary",)),
        name="combine",
    )(dest.reshape(n_tiles, 1, TM * TOP_K), w8, x1, h2, modx, nw3, wsg, wsu, wsd, y_rows)


def _pack_w_in(w):
    d = w.shape[0]
    z = lambda n: jnp.zeros((d, n), w.dtype)
    parts = [w[:, 0:1408], w[:, 1408:1420], z(116), w[:, 1420:1932], w[:, 1932:2956], w[:, 2956:2964], z(120),
             w[:, 2964:6036]]
    return jnp.concatenate(parts, axis=1).astype(BF16)


def _expand_w1(w1):
    half = NSA_CMP_STRIDE * HEAD_DIM
    w = jnp.zeros((NSA_CMP_STRIDE, 4, HEAD_DIM, 256), w1.dtype)
    for kv in range(2):
        w = w.at[:, kv, :, 128 * kv:128 * kv + 64].set(w1[kv, :half].reshape(NSA_CMP_STRIDE, HEAD_DIM, HEAD_DIM))
        w = w.at[:, kv, :, 128 * kv + 64:128 * kv + 128].set(w1[kv, half:].reshape(NSA_CMP_STRIDE, HEAD_DIM, HEAD_DIM))
    return w.reshape(NSA_CMP_STRIDE * 256, 256).astype(BF16)


def _cmp_w1_rows(w1):
    half = NSA_CMP_STRIDE * HEAD_DIM
    w = jnp.zeros((NSA_CMP_STRIDE, 2, HEAD_DIM, 256), w1.dtype)
    for kv in range(2):
        w = w.at[:, kv, :, 128 * kv:128 * kv + 64].set(w1[kv, :half].reshape(NSA_CMP_STRIDE, HEAD_DIM, HEAD_DIM))
        w = w.at[:, kv, :, 128 * kv + 64:128 * kv + 128].set(w1[kv, half:].reshape(NSA_CMP_STRIDE, HEAD_DIM, HEAD_DIM))
    return w.reshape(NSA_CMP_STRIDE * 128, 256).astype(BF16)


PROMPT_SLOTS = 4
SAMPLE_SLOTS = 8


def kernel(x_prompt, x_sample, c_prompt, c_sample, cache_moba_kv, cache_nsa_kv, cache_nsa_win, state_ssd_conv, state_ssd, page_table, rel_bias, ada_w, ada_b, norm_w, w_in, nsa_cmp_w1, nsa_cmp_b1, nsa_cmp_w2, nsa_cmp_b2, nsa_cmp_pos, ssd_conv_w, ssd_conv_b, ssd_dt_bias, ssd_a_log, ssd_d, ssd_norm_w, w_branch_moba, w_branch_nsa, w_branch_ssd, w_out, router_w, router_b, exp_w_gate, exp_w_up, exp_w_down, shared_w_gate, shared_w_up, shared_w_down):
    depth = w_in.shape[0]
    bp, lp, d = x_prompt.shape
    bs, ls, _ = x_sample.shape
    n_p = bp * lp
    n_s = bs * ls
    n = n_p + n_s
    assert n_s == TM and lp % TM == 0 and lp % SSD_CHUNK == 0
    n_pages = page_table.shape[1]
    past = n_pages * PAGE
    n_pool = cache_moba_kv.shape[1]
    tiles_p = lp // TM
    n_ptiles = n_p // TM

    def group_of_tile(i):
        return jnp.where(i < n_ptiles, i // tiles_p, bp)

    x = jnp.concatenate([x_prompt.reshape(n_p, d), x_sample.reshape(n_s, d)], axis=0)
    c_all = jnp.concatenate([c_prompt, c_sample, jnp.zeros((4, d), F32)], axis=0)
    kv_cache = jnp.transpose(cache_moba_kv, (0, 1, 3, 4, 5, 2))
    ns_cache = jnp.transpose(cache_nsa_kv, (0, 1, 3, 4, 2))
    pt_flat = page_table.reshape(-1).astype(jnp.int32)
    prompt_pages = jnp.arange(bp * (lp // PAGE), dtype=jnp.int32)

    q_chunk = 128
    tiles_p_bias = bias_tiles(rel_bias, q_chunk)
    tiles_s_bias = bias_tiles(rel_bias, ls)

    n_blk = -(-(n * TOP_K + N_EXPERTS * (EXPERT_ROWS - 1)) // EXPERT_ROWS)
    n_rows = n_blk * EXPERT_ROWS

    sample_pad = 128
    outs = [[] for _ in range(10)]
    for l in range(depth):
        mod = modulation(c_all, ada_w, ada_b, l)
        modx = jnp.concatenate([jnp.broadcast_to(mod[:bp, None, :], (bp, TM, 6 * d)),
                                jnp.repeat(mod[bp:bp + bs], ls, axis=0)[None]], axis=0)
        nw = norm_w[l]
        qm, kvm, qn, nsa, win, gn, z, xbc, dt, gbr = proj_in(x, modx, nw[0:1], _pack_w_in(w_in[l]), group_of_tile)
        w_exp = _expand_w1(nsa_cmp_w1[l])

        kvm_pages = kvm.reshape(n // PAGE, PAGE, 512)
        nsa_pages = nsa.reshape(n // PAGE, PAGE, 256)
        win_pages = win.reshape(n // PAGE, PAGE, 128)
        ks_p, pab_p = page_ctx(kvm_pages, nsa_pages, w_exp, 0, n_p // PAGE)
        km_p, cmp_p = ctx_final(ks_p.reshape(bp, lp // PAGE, 256), pab_p.reshape(bp, lp // NSA_CMP_STRIDE, 256),
                                nsa_cmp_pos, nsa_cmp_w1, nsa_cmp_b1, nsa_cmp_w2, nsa_cmp_b2, l, lp // PAGE)
        om_p, on_p = attention_slots(rel_bias, qm, qn, gn, 0, kvm_pages, nsa_pages, prompt_pages, None, None,
                                     win_pages, 0, km_p, cmp_p, tiles_p_bias,
                                     n_seq=bp, q_rows=q_chunk, n_chunks=lp // q_chunk, q0_base=0,
                                     tiles_per_seq=lp // PAGE, n_win_tiles=lp // PAGE,
                                     n_sel_blocks=lp // NSA_SEL_BLOCK, n_slots=PROMPT_SLOTS)

        ks_c, pab_c = page_ctx_cache(kv_cache, ns_cache, _cmp_w1_rows(nsa_cmp_w1[l]), l)
        ks_s, pab_s = ctx_gather(pt_flat, ks_c, pab_c, bs, n_pages)
        km_s, cmp_s = ctx_final(ks_s, pab_s, nsa_cmp_pos, nsa_cmp_w1, nsa_cmp_b1, nsa_cmp_w2, nsa_cmp_b2, l, n_pages)
        pad_rows = lambda a: jnp.pad(a[n_p:].reshape(bs, ls, -1), ((0, 0), (0, PAGE - ls), (0, 0)))
        tail_kv = jnp.transpose(pad_rows(kvm).reshape(bs, PAGE, 2, MOBA_HEADS, HEAD_DIM), (0, 2, 3, 4, 1))
        tail_ns = jnp.transpose(pad_rows(nsa).reshape(bs, PAGE, 4, HEAD_DIM)[:, :, 2:], (0, 2, 3, 1))
        win_s = jnp.concatenate([cache_nsa_win[l].reshape(bs, -1, 128), pad_rows(win)], axis=1)
        win_s = win_s.reshape(bs * (win_s.shape[1] // PAGE), PAGE, 128)
        lf_s = -(-(past + ls) // MOBA_BLOCK) * MOBA_BLOCK
        om_s, on_s = attention_slots(rel_bias, qm, qn, gn, n_p // ls, kv_cache, ns_cache, pt_flat,
                                     tail_kv, tail_ns, win_s, (past - NSA_WINDOW) // PAGE, km_s, cmp_s, tiles_s_bias,
                                     n_seq=bs, q_rows=ls, n_chunks=1, q0_base=past,
                                     tiles_per_seq=n_pages, n_win_tiles=win_s.shape[0] // bs,
                                     n_sel_blocks=lf_s // NSA_SEL_BLOCK, n_slots=SAMPLE_SLOTS, cache_layer=l)
        om = jnp.concatenate([om_p, om_s], axis=0)
        on = jnp.concatenate([on_p, on_s], axis=0)

        dt_t = dt[:, :SSD_HEADS].T
        ssd_par = (ssd_conv_w[l], ssd_conv_b[l], ssd_dt_bias[l], ssd_a_log[l], ssd_d[l], ssd_norm_w[l])
        y_p, st_p = ssd(xbc, z, dt, dt_t[:, :n_p].reshape(SSD_HEADS, bp, lp).transpose(1, 0, 2),
                        jnp.zeros((bp, 8, SSD_CONV_DIM), F32),
                        jnp.zeros((bp, SSD_HEADS, SSD_HEAD_DIM, SSD_STATE), F32), *ssd_par,
                        n_seq=bp, seq_rows=lp, chunk=SSD_CHUNK, valid_len=lp)
        pad_s = lambda a: jnp.pad(a[n_p:].reshape(bs, ls, -1), ((0, 0), (0, sample_pad - ls), (0, 0))).reshape(bs * sample_pad, -1)
        dt_t_s = jnp.pad(dt_t[:, n_p:].reshape(SSD_HEADS, bs, ls).transpose(1, 0, 2), ((0, 0), (0, 0), (0, sample_pad - ls)))
        conv_prev = jnp.pad(state_ssd_conv[l], ((0, 0), (8 - (SSD_CONV - 1), 0), (0, 0)))
        y_s, st_s = ssd(pad_s(xbc), pad_s(z), pad_s(dt), dt_t_s, conv_prev, state_ssd[l], *ssd_par,
                        n_seq=bs, seq_rows=sample_pad, chunk=sample_pad, valid_len=ls)
        ys = jnp.concatenate([y_p[:n_p], y_s.reshape(bs, sample_pad, -1)[:, :ls].reshape(n_s, -1)], axis=0)

        x1, h2, h2t, logits_t = mix_out(om, on, ys, gbr, x, modx, nw[1:2], nw[2:3],
                                        w_branch_moba[l].astype(BF16), w_branch_nsa[l].astype(BF16),
                                        w_branch_ssd[l].astype(BF16), w_out[l].astype(BF16), router_w[l].T,
                                        group_of_tile)
        eidx, w8, pos8, cnt = router(logits_t, router_b[l])
        cnt = cnt[:, 0].astype(jnp.int32)
        padded = (cnt + EXPERT_ROWS - 1) // EXPERT_ROWS * EXPERT_ROWS
        ends = jnp.cumsum(padded)
        off = ends - padded
        e_ids = jnp.arange(N_EXPERTS, dtype=jnp.int32)
        off_of = jnp.sum(jnp.where(eidx[:, :, None] == e_ids, off, 0), axis=-1)
        dest = (off_of + pos8).T.reshape(-1)
        blk_start = jnp.arange(n_blk, dtype=jnp.int32) * EXPERT_ROWS
        blk_e = jnp.minimum(jnp.sum((ends[None, :] <= blk_start[:, None]).astype(jnp.int32), axis=1), N_EXPERTS - 1)
        n_used = (ends[-1] // EXPERT_ROWS).astype(jnp.int32).reshape(1)
        x_rows = dispatch(h2t, dest, n_rows)
        y_rows = experts(x_rows, blk_e, n_used, exp_w_gate, exp_w_up, exp_w_down, l)
        x = combine(dest, w8.T, x1, h2, modx, nw[3:4], shared_w_gate[l].astype(BF16), shared_w_up[l].astype(BF16),
                    shared_w_down[l].astype(BF16), y_rows, group_of_tile)

        keep = min(NSA_WINDOW, lp)
        outs[0].append(kvm[:n_p].reshape(bp, lp, 2, MOBA_HEADS, HEAD_DIM))
        outs[1].append(kvm[n_p:].reshape(bs, ls, 2, MOBA_HEADS, HEAD_DIM))
        outs[2].append(nsa[:n_p].reshape(bp, lp, 4, HEAD_DIM))
        outs[3].append(nsa[n_p:].reshape(bs, ls, 4, HEAD_DIM))
        outs[4].append(win[:n_p].reshape(bp, lp, 2, HEAD_DIM)[:, lp - keep:])
        outs[5].append(win[n_p:].reshape(bs, ls, 2, HEAD_DIM))
        outs[6].append(xbc[:n_p].reshape(bp, lp, -1)[:, lp - (SSD_CONV - 1):])
        outs[7].append(xbc[n_p:].reshape(bs, ls, -1)[:, ls - (SSD_CONV - 1):])
        outs[8].append(st_p)
        outs[9].append(st_s)

    y_prompt = x[:n_p].reshape(bp, lp, d)
    y_sample = x[n_p:].reshape(bs, ls, d)
    return (y_prompt, y_sample) + tuple(jnp.stack(o) for o in outs)
```

```python
import functools
import math

import numpy as np
import jax
import jax.numpy as jnp
from jax import lax
from jax.experimental import pallas as pl
from jax.experimental.pallas import tpu as pltpu

F32 = jnp.float32
BF16 = jnp.bfloat16
HI = lax.Precision.HIGHEST

D_MODEL = 1024
PAGE = 128
HEAD_DIM = 64
MOBA_HEADS = 4
MOBA_BLOCK = 256
MOBA_TOPK = 3
NSA_HEADS = 4
NSA_CMP_STRIDE = 16
NSA_SEL_BLOCK = 64
NSA_TOPN = 16
NSA_WINDOW = 512
SSD_HEADS = 8
SSD_HEAD_DIM = 64
SSD_INNER = 512
SSD_GROUPS = 2
SSD_STATE = 128
SSD_CONV = 4
SSD_CHUNK = 256
SSD_CONV_DIM = 1024
N_BUCKETS = 32
MAX_DISTANCE = 128
N_EXPERTS = 64
TOP_K = 8
N_ROUTE_GROUPS = 8
TOPK_GROUPS = 4
D_EXPERT = 256
ROUTE_SCALE = 2.5
EPS = 1e-6

TM = 256
ROW_TILES = D_MODEL // 128
EXPERT_ROWS = 512
VMEM_LIMIT = 56 * 1024 * 1024
NEG = -1e30

_SEG = dict(qm=(0, 256), kvm=(256, 512), qn=(768, 256), nsa=(1024, 256), win=(1280, 128),
            gn=(1408, 128), z=(1536, 512), xbc=(2048, 1024), dt=(3072, 128), gbr=(3200, 3072))
PROJ_W = 6272


def _sigmoid(x):
    return 1.0 / (1.0 + jnp.exp(-x))


def _silu(x):
    return x * _sigmoid(x)


def _rms(x, w):
    return x * lax.rsqrt(jnp.mean(x * x, axis=-1, keepdims=True) + EPS) * w


def _dot(a, b):
    return jnp.dot(a.astype(BF16), b.astype(BF16), preferred_element_type=F32)


def _dot_nt(a, b, precision=None):
    if precision is None:
        a, b = a.astype(BF16), b.astype(BF16)
    return lax.dot_general(a, b, (((1,), (1,)), ((), ())), precision=precision,
                           preferred_element_type=F32)


def _cparams(sem, vmem=None):
    return pltpu.CompilerParams(dimension_semantics=sem, vmem_limit_bytes=vmem or VMEM_LIMIT)


def _mod_kernel(c_ref, w_ref, b_ref, o_ref):
    o_ref[...] = _dot(_silu(c_ref[...]), w_ref[...]) + b_ref[...]


def modulation(c_all, ada_w, ada_b, layer):
    rows, d = c_all.shape
    n = ada_w.shape[-1]
    tn = 512
    return pl.pallas_call(
        _mod_kernel,
        grid=(n // tn,),
        in_specs=[pl.BlockSpec((rows, d), lambda j: (0, 0)),
                  pl.BlockSpec((None, d, tn), lambda j: (layer, 0, j)),
                  pl.BlockSpec((None, 1, tn), lambda j: (layer, 0, j))],
        out_specs=pl.BlockSpec((rows, tn), lambda j: (0, j)),
        out_shape=jax.ShapeDtypeStruct((rows, n), F32),
        compiler_params=_cparams(("parallel",)),
        name="modulation",
    )(c_all, ada_w, ada_b.reshape(ada_b.shape[0], 1, n))


def _proj_kernel(x_ref, sh_ref, sc_ref, nw_ref, w_ref, qm_ref, kvm_ref, qn_ref, nsa_ref, win_ref,
                 gn_ref, z_ref, xbc_ref, dt_ref, gbr_ref):
    h = _rms(x_ref[...], nw_ref[...]) * (1.0 + sc_ref[...]) + sh_ref[...]
    hb = h.astype(BF16)

    def seg(name):
        o, w = _SEG[name]
        return jnp.dot(hb, w_ref[:, o:o + w], preferred_element_type=F32)

    qm_ref[...] = seg("qm")
    kvm_ref[...] = seg("kvm")
    qn_ref[...] = seg("qn")
    nsa_ref[...] = seg("nsa")
    win_ref[...] = seg("win")
    gn_ref[...] = _sigmoid(seg("gn"))
    z_ref[...] = seg("z")
    xbc_ref[...] = seg("xbc")
    dt_ref[...] = seg("dt")
    gbr_ref[...] = _sigmoid(seg("gbr"))


def _mod_spec(which, group_of_tile):
    return pl.BlockSpec((None, TM, D_MODEL), lambda i: (group_of_tile(i), 0, which))


def proj_in(x, modx, nw, w_packed, group_of_tile):
    n, d = x.shape
    names = ["qm", "kvm", "qn", "nsa", "win", "gn", "z", "xbc", "dt", "gbr"]
    return pl.pallas_call(
        _proj_kernel,
        grid=(n // TM,),
        in_specs=[pl.BlockSpec((TM, d), lambda i: (i, 0)),
                  _mod_spec(0, group_of_tile), _mod_spec(1, group_of_tile),
                  pl.BlockSpec((1, d), lambda i: (0, 0)),
                  pl.BlockSpec((d, PROJ_W), lambda i: (0, 0), pipeline_mode=pl.Buffered(1))],
        out_specs=[pl.BlockSpec((TM, _SEG[k][1]), lambda i: (i, 0)) for k in names],
        out_shape=[jax.ShapeDtypeStruct((n, _SEG[k][1]), F32) for k in names],
        compiler_params=_cparams(("parallel",)),
        name="proj_in",
    )(x, modx, modx, nw, w_packed)


CTX_PAGES = 16


def _page_ctx_kernel(k_ref, g_ref, w_ref, ks_ref, pab_ref):
    ks_ref[...] = jnp.sum(k_ref[...], axis=1)
    pab_ref[...] = _dot(g_ref[...], w_ref[...])


def page_ctx(kv_pages, nsa_pages, w_exp, page0, n_pages):
    groups = PAGE // NSA_CMP_STRIDE
    g_view = nsa_pages.reshape(nsa_pages.shape[0] * groups, NSA_CMP_STRIDE * 256)
    blk0 = page0 // CTX_PAGES
    return pl.pallas_call(
        _page_ctx_kernel,
        grid=(n_pages // CTX_PAGES,),
        in_specs=[pl.BlockSpec((CTX_PAGES, PAGE, 256), lambda i: (blk0 + i, 0, 0)),
                  pl.BlockSpec((CTX_PAGES * groups, NSA_CMP_STRIDE * 256), lambda i: (blk0 + i, 0)),
                  pl.BlockSpec((NSA_CMP_STRIDE * 256, 256), lambda i: (0, 0))],
        out_specs=[pl.BlockSpec((CTX_PAGES, 256), lambda i: (i, 0)),
                   pl.BlockSpec((CTX_PAGES * groups, 256), lambda i: (i, 0))],
        out_shape=[jax.ShapeDtypeStruct((n_pages, 256), F32),
                   jax.ShapeDtypeStruct((n_pages * groups, 256), F32)],
        compiler_params=_cparams(("parallel",)),
        name="page_ctx",
    )(kv_pages, g_view, w_exp)


def _page_ctx_cache_kernel(k_ref, c_ref, w_ref, ks_ref, pab_ref, tok):
    n_tok = CTX_PAGES * PAGE
    kt = jnp.concatenate([k_ref[p].reshape(MOBA_HEADS * HEAD_DIM, PAGE) for p in range(CTX_PAGES)], axis=1)
    page_of_lane = lax.shift_right_logical(lax.broadcasted_iota(jnp.int32, (CTX_PAGES, n_tok), 1), 7)
    ind = jnp.where(page_of_lane == lax.broadcasted_iota(jnp.int32, (CTX_PAGES, n_tok), 0), 1.0, 0.0)
    ks_ref[...] = _dot_nt(ind, kt)
    for p in range(CTX_PAGES):
        for c in range(2):
            tok[p * PAGE:(p + 1) * PAGE, HEAD_DIM * c:HEAD_DIM * (c + 1)] = c_ref[p, c].T
    groups = n_tok // NSA_CMP_STRIDE
    acc = jnp.zeros((groups, 256), F32)
    for r in range(NSA_CMP_STRIDE):
        acc = acc + _dot(tok[pl.ds(r, groups, stride=NSA_CMP_STRIDE), :], w_ref[128 * r:128 * (r + 1), :])
    pab_ref[...] = acc


def page_ctx_cache(kv_tiles, ns_tiles, w_cmp, layer):
    n_pages = kv_tiles.shape[1]
    groups = PAGE // NSA_CMP_STRIDE
    return pl.pallas_call(
        _page_ctx_cache_kernel,
        grid=(n_pages // CTX_PAGES,),
        in_specs=[pl.BlockSpec((None, CTX_PAGES, None, MOBA_HEADS, HEAD_DIM, PAGE), lambda i: (layer, i, 0, 0, 0, 0)),
                  pl.BlockSpec((None, CTX_PAGES, 2, HEAD_DIM, PAGE), lambda i: (layer, i, 0, 0, 0)),
                  pl.BlockSpec(w_cmp.shape, lambda i: (0, 0))],
        out_specs=[pl.BlockSpec((CTX_PAGES, 256), lambda i: (i, 0)),
                   pl.BlockSpec((CTX_PAGES * groups, 256), lambda i: (i, 0))],
        out_shape=[jax.ShapeDtypeStruct((n_pages, 256), F32),
                   jax.ShapeDtypeStruct((n_pages * groups, 256), F32)],
        scratch_shapes=[pltpu.VMEM((CTX_PAGES * PAGE, 128), F32)],
        compiler_params=_cparams(("parallel",)),
        name="page_ctx_cache",
    )(kv_tiles, ns_tiles, w_cmp)


def _ctx_gather_kernel(pt_ref, ks_hbm, pab_hbm, ks_out, pab_out, sem):
    b = pl.program_id(0)
    n_pages = ks_out.shape[1]
    groups = pab_out.shape[1] // n_pages

    def copies(p):
        page = pt_ref[b * n_pages + p]
        c1 = pltpu.make_async_copy(ks_hbm.at[pl.ds(page, 1)], ks_out.at[0, pl.ds(p, 1)], sem.at[0])
        c2 = pltpu.make_async_copy(pab_hbm.at[pl.ds(page * groups, groups)],
                                   pab_out.at[0, pl.ds(p * groups, groups)], sem.at[1])
        return c1, c2

    def start(p, carry):
        c1, c2 = copies(p)
        c1.start()
        c2.start()
        return carry

    def wait(p, carry):
        c1, c2 = copies(p)
        c1.wait()
        c2.wait()
        return carry

    lax.fori_loop(0, n_pages, start, 0)
    lax.fori_loop(0, n_pages, wait, 0)


def ctx_gather(page_table_flat, ksum, pab, n_seq, n_pages):
    groups = PAGE // NSA_CMP_STRIDE
    gs = pltpu.PrefetchScalarGridSpec(
        num_scalar_prefetch=1,
        grid=(n_seq,),
        in_specs=[pl.BlockSpec(memory_space=pl.ANY), pl.BlockSpec(memory_space=pl.ANY)],
        out_specs=[pl.BlockSpec((1, n_pages) + ksum.shape[1:], lambda b, pt: (b,) + (0,) * ksum.ndim),
                   pl.BlockSpec((1, n_pages * groups, 256), lambda b, pt: (b, 0, 0))],
        scratch_shapes=[pltpu.SemaphoreType.DMA((2,))],
    )
    return pl.pallas_call(
        _ctx_gather_kernel,
        grid_spec=gs,
        out_shape=[jax.ShapeDtypeStruct((n_seq, n_pages) + ksum.shape[1:], F32),
                   jax.ShapeDtypeStruct((n_seq, n_pages * groups, 256), F32)],
        compiler_params=_cparams(("arbitrary",)),
        name="ctx_gather",
    )(page_table_flat, ksum, pab)


def _gelu_tanh(x):
    return 0.5 * x * (1.0 + jnp.tanh(math.sqrt(2.0 / math.pi) * (x + 0.044715 * (x * x * x))))


def _ctx_final_kernel(ks_ref, pab_ref, pos_ref, w1_ref, b1_ref, w2_ref, b2_ref, km_ref, cmp_ref, *, head_rows):
    nb = km_ref.shape[0]
    n_pages = 2 * nb
    r = lax.broadcasted_iota(jnp.int32, (nb, n_pages), 0)
    c = lax.broadcasted_iota(jnp.int32, (nb, n_pages), 1)
    pair = jnp.where((c == 2 * r) | (c == 2 * r + 1), 1.0, 0.0).astype(F32)
    if head_rows:
        ks = jnp.concatenate([ks_ref[pl.ds(h, n_pages, stride=2 * MOBA_HEADS), :] for h in range(MOBA_HEADS)], axis=1)
    else:
        ks = ks_ref[...]
    km_ref[...] = jnp.dot(pair, ks, precision=HI, preferred_element_type=F32) * (1.0 / MOBA_BLOCK)

    pab = pab_ref[...]
    ng = pab.shape[0]
    row = lax.broadcasted_iota(jnp.int32, (ng, 1), 0)
    outs = []
    for kv in range(2):
        pa = pab[:, 128 * kv:128 * kv + 64]
        pb = pab[:, 128 * kv + 64:128 * kv + 128]
        pb_next = jnp.where(row == ng - 1, 0.0, pltpu.roll(pb, ng - 1, 0))
        pos_term = _dot(pos_ref[kv], w1_ref[kv]) + b1_ref[kv]
        hid = _gelu_tanh(pa + pb_next + pos_term)
        outs.append(_dot(hid, w2_ref[kv]) + b2_ref[kv])
    cmp_ref[...] = jnp.concatenate(outs, axis=1)


def ctx_final(ksum, pab, pos, w1, b1, w2, b2, layer, n_pages):
    nb = ksum.shape[0]
    head_rows = ksum.shape[-1] == HEAD_DIM
    ng = pab.shape[1]
    lf = pos.shape[2] * pos.shape[3]
    pos2 = pos.reshape(pos.shape[0], 2, 1, lf)
    return pl.pallas_call(
        functools.partial(_ctx_final_kernel, head_rows=head_rows),
        grid=(nb,),
        in_specs=[pl.BlockSpec((None,) + ksum.shape[1:], lambda b: (b, 0, 0)),
                  pl.BlockSpec((None, ng, 256), lambda b: (b, 0, 0)),
                  pl.BlockSpec((None, 2, 1, lf), lambda b: (layer, 0, 0, 0)),
                  pl.BlockSpec((None, 2, lf, HEAD_DIM), lambda b: (layer, 0, 0, 0)),
                  pl.BlockSpec((None, 2, 1, HEAD_DIM), lambda b: (layer, 0, 0, 0)),
                  pl.BlockSpec((None, 2, HEAD_DIM, HEAD_DIM), lambda b: (layer, 0, 0, 0)),
                  pl.BlockSpec((None, 2, 1, HEAD_DIM), lambda b: (layer, 0, 0, 0))],
        out_specs=[pl.BlockSpec((None, n_pages // 2, 256), lambda b: (b, 0, 0)),
                   pl.BlockSpec((None, ng, 128), lambda b: (b, 0, 0))],
        out_shape=[jax.ShapeDtypeStruct((nb, n_pages // 2, 256), F32),
                   jax.ShapeDtypeStruct((nb, ng, 128), F32)],
        compiler_params=_cparams(("parallel",)),
        name="ctx_final",
    )(ksum, pab, pos2, w1, b1.reshape(b1.shape[0], 2, 1, HEAD_DIM), w2, b2.reshape(b2.shape[0], 2, 1, HEAD_DIM))


def _topk_lanes(score, k, lane_f, n_lanes):
    sel = jnp.zeros_like(score)
    for _ in range(k):
        m = jnp.max(score, axis=1, keepdims=True)
        idx = jnp.min(jnp.where(score == m, lane_f, float(n_lanes)), axis=1, keepdims=True)
        hit = lane_f == idx
        sel = jnp.where(hit & (m > -jnp.inf), 1.0, sel)
        score = jnp.where(hit, -jnp.inf, score)
    return sel


def _rel_bucket_np(dist):
    n = np.maximum(dist, 0)
    exact = N_BUCKETS // 2
    nf = np.maximum(n, 1).astype(np.float32)
    large = exact + (np.log(nf / np.float32(exact)) / np.float32(math.log(MAX_DISTANCE / exact))
                     * np.float32(N_BUCKETS - exact)).astype(np.int32)
    return np.where(n < exact, n, np.minimum(large, N_BUCKETS - 1)).astype(np.int32)


WIN_TILES = NSA_WINDOW // PAGE + 1


def _bucket_thresholds():
    b = _rel_bucket_np(np.arange(4 * MAX_DISTANCE))
    return [int(np.argmax(b >= k)) for k in range(N_BUCKETS)]


def _softmax_update(m_ref, l_ref, a_ref, idx, s, valid, pv_fn):
    s = jnp.where(valid, s, NEG)
    m_old = m_ref[idx]
    m_new = jnp.maximum(m_old, jnp.max(s, axis=1, keepdims=True))
    p = jnp.exp(s - m_new)
    alpha = jnp.exp(m_old - m_new)
    l_ref[idx] = alpha * l_ref[idx] + jnp.sum(p, axis=1, keepdims=True)
    a_ref[idx] = alpha * a_ref[idx] + pv_fn(p)
    m_ref[idx] = m_new


def _pv_slots(vs, pv_dot):
    def f(p):
        acc = pv_dot(p[:, 0:PAGE], vs[0])
        for j in range(1, len(vs)):
            acc = acc + pv_dot(p[:, PAGE * j:PAGE * (j + 1)], vs[j])
        return acc
    return f


def _attn_slots_kernel(si_ref, st_ref, sl_ref, pg_ref, *refs, q_rows, q0_base, n_slots, cache_tiles, tail_tile):
    Q, G = q_rows, n_slots
    rb_ref, qm_ref, qn_ref, gn_ref = refs[0:4]
    kv_refs = refs[4:4 + G]
    ns_refs = refs[4 + G:4 + 2 * G]
    tkv_ref, tns_ref = refs[4 + 2 * G:6 + 2 * G]
    win_refs = refs[6 + 2 * G:6 + 2 * G + WIN_TILES]
    base = 6 + 2 * G + WIN_TILES
    km_ref, cmp_ref, tbm_ref, tbn_ref, pool_ref, om_ref, on_ref = refs[base:base + 7]
    qms, qns, qbd, selm, sels, m_m, l_m, a_m, m_n, l_n, a_n, oc, ow = refs[base + 7:]

    s_id = pl.program_id(1)
    i = si_ref[s_id]
    t0 = st_ref[s_id]
    q0 = q0_base + i * Q
    ob = lax.shift_right_logical(q0, 8)
    nb = km_ref.shape[0]
    nbs_p = pool_ref.shape[1]
    nc = cmp_ref.shape[0]
    R4 = NSA_HEADS * Q

    qi = lax.broadcasted_iota(jnp.int32, (Q, 1), 0)
    qi4 = jnp.concatenate([qi] * NSA_HEADS, axis=0)
    kj = lax.broadcasted_iota(jnp.int32, (Q, PAGE), 1)
    lane_nb = lax.broadcasted_iota(jnp.int32, (Q, nb), 1)
    lane_bs = lax.broadcasted_iota(jnp.int32, (Q, nbs_p), 1)
    rep4 = lambda x: jnp.concatenate([x] * NSA_HEADS, axis=0)

    @pl.when(t0 == 0)
    def _init():
        qm = qm_ref[...]
        qn = qn_ref[...]
        qms[...] = qm * (HEAD_DIM ** -0.5)
        qbd[...] = jnp.zeros(qbd.shape, F32)
        for h in range(NSA_HEADS):
            sl = slice(HEAD_DIM * h, HEAD_DIM * (h + 1))
            qns[h * Q:(h + 1) * Q, :] = qn[:, sl] * (HEAD_DIM ** -0.5)
            qbd[h * Q:(h + 1) * Q, sl] = qm[:, sl] * (HEAD_DIM ** -0.5)
        qn4 = qns[...]
        km = km_ref[...]
        lane_f = lane_nb.astype(F32)
        for h in range(MOBA_HEADS):
            sl = slice(HEAD_DIM * h, HEAD_DIM * (h + 1))
            g = _dot_nt(qm[:, sl], km[:, sl], precision=HI)
            g = jnp.where(lane_nb < ob, g, -jnp.inf)
            selm[h] = _topk_lanes(g, min(MOBA_TOPK, nb), lane_f, nb)
        cm = cmp_ref[...]
        epos = lax.broadcasted_iota(jnp.int32, (Q, nc), 1) * NSA_CMP_STRIDE + (2 * NSA_CMP_STRIDE - 1)
        dc = q0 + qi - epos
        thr = _bucket_thresholds()
        bias = [jnp.full((Q, nc), rb_ref[N_BUCKETS - 1, MOBA_HEADS + h], F32) for h in range(NSA_HEADS)]
        for k in range(N_BUCKETS - 2, -1, -1):
            below = dc < thr[k + 1]
            bias = [jnp.where(below, rb_ref[k, MOBA_HEADS + h], bias[h]) for h in range(NSA_HEADS)]
        sc = _dot_nt(qn4, cm[:, :HEAD_DIM]) + jnp.concatenate(bias, axis=0)
        valid = rep4(dc >= 0)
        sc = jnp.where(valid, sc, NEG)
        mx = jnp.max(sc, axis=1, keepdims=True)
        e = jnp.where(valid, jnp.exp(sc - mx), 0.0)
        den = jnp.sum(e, axis=1, keepdims=True)
        p = e / jnp.where(den > 0, den, 1.0)
        oc[...] = _dot(p, cm[:, HEAD_DIM:])
        psum = p[0:Q] + p[Q:2 * Q] + p[2 * Q:3 * Q] + p[3 * Q:4 * Q]
        imp = jnp.dot(psum, pool_ref[...], precision=HI, preferred_element_type=F32)
        cur = lax.shift_right_logical(q0 + qi, 6)
        forced = (lane_bs == 0) | (lane_bs == cur) | (lane_bs == cur - 1)
        score = jnp.where(forced, jnp.inf, jnp.where(lane_bs <= cur, imp, -jnp.inf))
        sels[...] = _topk_lanes(score, NSA_TOPN, lane_bs.astype(F32), nbs_p)
        tw0 = lax.shift_right_logical(q0, 7) - (WIN_TILES - 1)
        s_parts, v_parts, m_parts = [], [], []
        for w in range(WIN_TILES):
            wv = win_refs[w][...]
            back = WIN_TILES - 1 - w
            dw = back * PAGE + qi - kj
            s_parts.append(_dot_nt(qn4, wv[:, :HEAD_DIM]) + tbn_ref[min(back, 2)])
            m_parts.append((dw >= 0) & (dw < NSA_WINDOW) & (tw0 + w >= 0))
            v_parts.append(wv[:, HEAD_DIM:])
        sw = jnp.concatenate(s_parts, axis=1)
        wvalid = rep4(jnp.concatenate(m_parts, axis=1))
        sw = jnp.where(wvalid, sw, NEG)
        mx = jnp.max(sw, axis=1, keepdims=True)
        e = jnp.where(wvalid, jnp.exp(sw - mx), 0.0)
        den = jnp.sum(e, axis=1, keepdims=True)
        pw = e / jnp.where(den > 0, den, 1.0)
        o_w = _dot(pw[:, 0:PAGE], v_parts[0])
        for w in range(1, WIN_TILES):
            o_w = o_w + _dot(pw[:, PAGE * w:PAGE * (w + 1)], v_parts[w])
        ow[...] = o_w
        m_m[...] = jnp.full(m_m.shape, NEG, F32)
        l_m[...] = jnp.zeros(l_m.shape, F32)
        a_m[...] = jnp.zeros(a_m.shape, F32)
        m_n[...] = jnp.full(m_n.shape, NEG, F32)
        l_n[...] = jnp.zeros(l_n.shape, F32)
        a_n[...] = jnp.zeros(a_n.shape, F32)

    is_tail = (t0 == tail_tile) if tail_tile is not None else None

    def pick(j, cache_fn, tail_fn):
        x = cache_fn(kv_refs[j], ns_refs[j])
        if is_tail is not None and j == 0:
            x = jnp.where(is_tail, tail_fn(tkv_ref, tns_ref), x)
        return x

    if cache_tiles:
        k_of = lambda j, h: pick(j, *[lambda kv, ns: kv[0, h]] * 2)
        v_of = lambda j, h: pick(j, *[lambda kv, ns: kv[1, h]] * 2)
        ksel_of = lambda j: pick(j, *[lambda kv, ns: ns[0]] * 2)
        vsel_of = lambda j: pick(j, *[lambda kv, ns: ns[1]] * 2)
        qk, pv = _dot, _dot_nt
    else:
        k_of = lambda j, h: kv_refs[j][:, HEAD_DIM * h:HEAD_DIM * (h + 1)]
        v_of = lambda j, h: kv_refs[j][:, 256 + HEAD_DIM * h:256 + HEAD_DIM * (h + 1)]
        ksel_of = lambda j: ns_refs[j][:, 0:HEAD_DIM]
        vsel_of = lambda j: ns_refs[j][:, HEAD_DIM:2 * HEAD_DIM]
        qk, pv = _dot_nt, _dot

    causal, v_idx, blk = [], [], []
    for j in range(G):
        delta = q0 - (t0 + j) * PAGE
        causal.append(delta + qi - kj >= 0)
        v_idx.append(jnp.clip(lax.shift_right_arithmetic(delta, 7), 0, 2))
        blk.append(lax.shift_right_logical(t0 + j, 1))

    moba_sel = [[jnp.sum(jnp.where(lane_nb == blk[j], selm[h], 0.0), axis=1, keepdims=True) > 0.0
                 for h in range(MOBA_HEADS)] for j in range(G)]
    if cache_tiles:
        assert G % 2 == 0
        kv_all = lambda j, w: pick(j, *[lambda kv, ns: kv[w].reshape(MOBA_HEADS * HEAD_DIM, PAGE)] * 2)
        q_bd = qbd[...]
        s_parts, m_parts = [], []
        for j in range(0, G, 2):
            s_parts.append(_dot(q_bd, jnp.concatenate([kv_all(j, 0), kv_all(j + 1, 0)], axis=1)))
        for j in range(G):
            valid_j = jnp.concatenate([causal[j] & (moba_sel[j][h] | (blk[j] == ob)) for h in range(MOBA_HEADS)], axis=0)
            m_parts.append(valid_j)
        bias = jnp.concatenate([tbm_ref[v_idx[j]] for j in range(G)], axis=1)

        def pv_heads(p):
            acc = None
            for j in range(0, G, 2):
                o2 = _dot_nt(p[:, PAGE * j:PAGE * (j + 2)],
                             jnp.concatenate([kv_all(j, 1), kv_all(j + 1, 1)], axis=1))
                diag = jnp.concatenate([o2[h * Q:(h + 1) * Q, HEAD_DIM * h:HEAD_DIM * (h + 1)]
                                        for h in range(MOBA_HEADS)], axis=0)
                acc = diag if acc is None else acc + diag
            return acc

        _softmax_update(m_m, l_m, a_m, 0, jnp.concatenate(s_parts, axis=1) + bias, jnp.concatenate(m_parts, axis=1),
                        pv_heads)
    else:
        q_all = qms[...]
        for h in range(MOBA_HEADS):
            sl = slice(HEAD_DIM * h, HEAD_DIM * (h + 1))
            s_parts, m_parts = [], []
            for j in range(G):
                s_parts.append(qk(q_all[:, sl], k_of(j, h)) + tbm_ref[v_idx[j], h * Q:(h + 1) * Q, :])
                m_parts.append(causal[j] & (moba_sel[j][h] | (blk[j] == ob)))
            _softmax_update(m_m, l_m, a_m, h, jnp.concatenate(s_parts, axis=1), jnp.concatenate(m_parts, axis=1),
                            _pv_slots([v_of(j, h) for j in range(G)], pv))

    qn4 = qns[...]
    sel_all = sels[...]
    s_parts, m_parts = [], []
    for j in range(G):
        s_parts.append(qk(qn4, ksel_of(j)) + tbn_ref[v_idx[j]])
        lo = jnp.sum(jnp.where(lane_bs == 2 * (t0 + j), sel_all, 0.0), axis=1, keepdims=True)
        hi = jnp.sum(jnp.where(lane_bs == 2 * (t0 + j) + 1, sel_all, 0.0), axis=1, keepdims=True)
        m_parts.append(causal[j] & (jnp.where(kj < NSA_SEL_BLOCK, lo, hi) > 0.0))
    _softmax_update(m_n, l_n, a_n, 0, jnp.concatenate(s_parts, axis=1), rep4(jnp.concatenate(m_parts, axis=1)),
                    _pv_slots([vsel_of(j) for j in range(G)], pv))

    @pl.when(sl_ref[s_id] == 1)
    def _finish():
        outs = []
        for h in range(MOBA_HEADS):
            if cache_tiles:
                l = l_m[0, h * Q:(h + 1) * Q, :]
                a = a_m[0, h * Q:(h + 1) * Q, :]
            else:
                l, a = l_m[h], a_m[h]
            outs.append(a / jnp.where(l > 0, l, 1.0))
        om_ref[...] = jnp.concatenate(outs, axis=1)
        l = l_n[0]
        o_s = a_n[0] / jnp.where(l > 0, l, 1.0)
        o_w = ow[...]
        o_c = oc[...]
        gn = gn_ref[...]
        outs = []
        for h in range(NSA_HEADS):
            r = slice(h * Q, (h + 1) * Q)
            outs.append(gn[:, 3 * h:3 * h + 1] * o_c[r] + gn[:, 3 * h + 1:3 * h + 2] * o_s[r]
                        + gn[:, 3 * h + 2:3 * h + 3] * o_w[r])
        on_ref[...] = jnp.concatenate(outs, axis=1)


def bias_tiles(rel_bias, q_rows):
    Q = q_rows
    qi = np.arange(Q)[:, None]
    kj = np.arange(PAGE)[None, :]
    d = jnp.asarray(np.stack([v * PAGE + qi - kj for v in range(3)]))[..., None]
    thr = _bucket_thresholds()
    tb = jnp.broadcast_to(rel_bias[N_BUCKETS - 1], d.shape[:-1] + (rel_bias.shape[1],))
    for k in range(N_BUCKETS - 2, -1, -1):
        tb = jnp.where(d < thr[k + 1], rel_bias[k], tb)
    stack = lambda t: t.transpose(0, 3, 1, 2).reshape(3, t.shape[-1] * Q, PAGE)
    return stack(tb[..., :MOBA_HEADS]), stack(tb[..., MOBA_HEADS:])


def attention_slots(rel_bias, qm, qn, gn, q_blk0, kv_src, ns_src, page_idx, tail_kv, tail_ns, win_pages, win_w0t,
                    kmean, cmp, tiles, *, n_seq, q_rows, n_chunks, q0_base, tiles_per_seq, n_win_tiles,
                    n_sel_blocks, n_slots, cache_layer=None):
    Q, G = q_rows, n_slots
    cache_tiles = cache_layer is not None
    assert PAGE % Q == 0 and q0_base % PAGE == 0 and (n_chunks == 1 or Q == PAGE)
    tbm, tbn = tiles
    nb = kmean.shape[1]
    nc = cmp.shape[1]
    nbs_p = -(-n_sel_blocks // 128) * 128
    per = NSA_SEL_BLOCK // NSA_CMP_STRIDE
    pool = jnp.asarray((np.arange(nc)[:, None] // per == np.arange(nbs_p)[None, :]).astype(np.float32))
    has_tail = tail_kv is not None

    steps = []
    for i in range(n_chunks):
        n_tiles = (q0_base + i * Q + Q - 1) // PAGE + 1
        n_st = -(-n_tiles // G)
        for s in range(n_st):
            steps.append((i, s * G, int(s == n_st - 1)))
    steps = np.asarray(steps, np.int32)
    n_steps = steps.shape[0]
    tail_tile = tiles_per_seq if has_tail else None
    assert tail_tile is None or tail_tile % G == 0
    if cache_tiles:
        assert has_tail
        kv_blk = (None, None, 2, MOBA_HEADS, HEAD_DIM, PAGE)
        ns_blk = (None, None, 2, HEAD_DIM, PAGE)
        tail_kv_blk, tail_ns_blk = kv_blk[1:], ns_blk[1:]
    else:
        assert not has_tail
        kv_blk = tail_kv_blk = (None, PAGE, 512)
        ns_blk = tail_ns_blk = (None, PAGE, 128)
        tail_kv = jnp.zeros((1, PAGE, 512), F32)
        tail_ns = jnp.zeros((1, PAGE, 256), F32)

    def q_map(b, s, si, st, sl, pg):
        return (q_blk0 + b * n_chunks + si[s], 0)

    def o_map(b, s, si, st, sl, pg):
        return (b * n_chunks + si[s], 0)

    def page_of(b, s, st, pg, j):
        return pg[b * tiles_per_seq + jnp.minimum(st[s] + j, tiles_per_seq - 1)]

    def kv_map(j):
        if cache_tiles:
            return lambda b, s, si, st, sl, pg: (cache_layer, page_of(b, s, st, pg, j), 0, 0, 0, 0)
        return lambda b, s, si, st, sl, pg: (page_of(b, s, st, pg, j), 0, 0)

    def ns_map(j):
        if cache_tiles:
            return lambda b, s, si, st, sl, pg: (cache_layer, page_of(b, s, st, pg, j), 1, 0, 0)
        return lambda b, s, si, st, sl, pg: (page_of(b, s, st, pg, j), 0, 1)

    def tail_map(nd):
        if cache_tiles:
            return lambda b, s, si, st, sl, pg: (b,) + (0,) * (nd - 1)
        return lambda b, s, si, st, sl, pg: (0, 0, 0) if nd == 0 else (0, 0, 1)

    def win_map(w):
        def f(b, s, si, st, sl, pg):
            tw = lax.shift_right_logical(q0_base + si[s] * Q, 7) - (WIN_TILES - 1) + w - win_w0t
            return (b * n_win_tiles + jnp.clip(tw, 0, n_win_tiles - 1), 0, 0)
        return f

    const = lambda nd: (lambda b, s, si, st, sl, pg: (0,) * nd)
    moba_state = (1, MOBA_HEADS * Q) if cache_tiles else (MOBA_HEADS, Q)
    in_specs = ([pl.BlockSpec(memory_space=pltpu.SMEM),
                 pl.BlockSpec((Q, 256), q_map), pl.BlockSpec((Q, 256), q_map), pl.BlockSpec((Q, 128), q_map)]
                + [pl.BlockSpec(kv_blk, kv_map(j)) for j in range(G)]
                + [pl.BlockSpec(ns_blk, ns_map(j)) for j in range(G)]
                + [pl.BlockSpec(tail_kv_blk, tail_map(5 if cache_tiles else 0)),
                   pl.BlockSpec(tail_ns_blk, tail_map(4 if cache_tiles else 1))]
                + [pl.BlockSpec((None, PAGE, 128), win_map(w)) for w in range(WIN_TILES)]
                + [pl.BlockSpec((None, nb, 256), lambda b, s, si, st, sl, pg: (b, 0, 0)),
                   pl.BlockSpec((None, nc, 128), lambda b, s, si, st, sl, pg: (b, 0, 0)),
                   pl.BlockSpec((3, MOBA_HEADS * Q, PAGE), const(3)),
                   pl.BlockSpec((3, NSA_HEADS * Q, PAGE), const(3)),
                   pl.BlockSpec((nc, nbs_p), const(2))])
    gs = pltpu.PrefetchScalarGridSpec(
        num_scalar_prefetch=4,
        grid=(n_seq, n_steps),
        in_specs=in_specs,
        out_specs=[pl.BlockSpec((Q, 256), o_map), pl.BlockSpec((Q, 256), o_map)],
        scratch_shapes=[
            pltpu.VMEM((Q, 256), F32),
            pltpu.VMEM((NSA_HEADS * Q, HEAD_DIM), F32),
            pltpu.VMEM((MOBA_HEADS * Q, 256), F32),
            pltpu.VMEM((MOBA_HEADS, Q, nb), F32),
            pltpu.VMEM((Q, nbs_p), F32),
            pltpu.VMEM(moba_state + (1,), F32),
            pltpu.VMEM(moba_state + (1,), F32),
            pltpu.VMEM(moba_state + (HEAD_DIM,), F32),
            pltpu.VMEM((1, NSA_HEADS * Q, 1), F32),
            pltpu.VMEM((1, NSA_HEADS * Q, 1), F32),
            pltpu.VMEM((1, NSA_HEADS * Q, HEAD_DIM), F32),
            pltpu.VMEM((NSA_HEADS * Q, HEAD_DIM), F32),
            pltpu.VMEM((NSA_HEADS * Q, HEAD_DIM), F32),
        ],
    )
    n_tok = n_seq * n_chunks * Q
    return pl.pallas_call(
        functools.partial(_attn_slots_kernel, q_rows=Q, q0_base=q0_base, n_slots=G, cache_tiles=cache_tiles,
                          tail_tile=tail_tile),
        grid_spec=gs,
        out_shape=[jax.ShapeDtypeStruct((n_tok, 256), F32), jax.ShapeDtypeStruct((n_tok, 256), F32)],
        compiler_params=_cparams(("parallel", "arbitrary")),
        name="attention",
    )(jnp.asarray(steps[:, 0]), jnp.asarray(steps[:, 1]), jnp.asarray(steps[:, 2]), page_idx, rel_bias,
      qm, qn, gn, *([kv_src] * G), *([ns_src] * G), tail_kv, tail_ns, *([win_pages] * WIN_TILES),
      kmean, cmp, tbm, tbn, pool)


def _softplus(x):
    return jnp.maximum(x, 0.0) + jnp.log(1.0 + jnp.exp(-jnp.abs(x)))


def _ssd_kernel(xbc_ref, z_ref, dt_ref, dtt_ref, cprev_ref, st0_ref, cw_ref, cb_ref, dtb_ref, dtbt_ref,
                al_ref, alt_ref, dd_ref, nw_ref, y_ref, st_ref, xp, *, valid_len):
    c = pl.program_id(1)
    cl = xbc_ref.shape[0]
    gn = SSD_GROUPS * SSD_STATE

    @pl.when(c == 0)
    def _first():
        xp[0:8, :] = cprev_ref[...]
        st_ref[...] = st0_ref[...]

    xp[8:8 + cl, :] = xbc_ref[...]
    conv = cb_ref[...] + cw_ref[0:1, :] * xp[5:5 + cl, :]
    for k in range(1, SSD_CONV):
        conv = conv + cw_ref[k:k + 1, :] * xp[5 + k:5 + k + cl, :]
    xp[0:8, :] = xp[cl:cl + 8, :]
    act = _silu(conv)
    xs = act[:, :SSD_INNER]
    bm = act[:, SSD_INNER:SSD_INNER + gn]
    cm = act[:, SSD_INNER + gn:]

    row = lax.broadcasted_iota(jnp.int32, (cl, cl), 0)
    col = lax.broadcasted_iota(jnp.int32, (cl, cl), 1)
    tri = row >= col
    pos_r = c * cl + lax.broadcasted_iota(jnp.int32, (cl, 1), 0)
    pos_c = c * cl + lax.broadcasted_iota(jnp.int32, (1, cl), 1)
    dt = jnp.where(pos_r < valid_len, _softplus(dt_ref[...] + dtb_ref[...]), 0.0)
    dtt = jnp.where(pos_c < valid_len, _softplus(dtt_ref[...] + dtbt_ref[...]), 0.0)
    da = dt * (-jnp.exp(al_ref[...]))
    dat = dtt * (-jnp.exp(alt_ref[...]))
    acum = jnp.dot(jnp.where(tri, 1.0, 0.0), da, precision=HI, preferred_element_type=F32)
    acumt = jnp.dot(dat, jnp.where(row <= col, 1.0, 0.0), precision=HI, preferred_element_type=F32)

    cbs = [_dot_nt(cm[:, SSD_STATE * g:SSD_STATE * (g + 1)], bm[:, SSD_STATE * g:SSD_STATE * (g + 1)])
           for g in range(SSD_GROUPS)]
    ys = []
    for h in range(SSD_HEADS):
        g = h // (SSD_HEADS // SSD_GROUPS)
        bg = bm[:, SSD_STATE * g:SSD_STATE * (g + 1)]
        cg = cm[:, SSD_STATE * g:SSD_STATE * (g + 1)]
        a_col = acum[:, h:h + 1]
        a_row = acumt[h:h + 1, :]
        a_last = acumt[h:h + 1, cl - 1:cl]
        lmat = jnp.exp(jnp.where(tri, a_col - a_row, -jnp.inf))
        xh = xs[:, SSD_HEAD_DIM * h:SSD_HEAD_DIM * (h + 1)]
        xdt = xh * dt[:, h:h + 1]
        y_diag = _dot(cbs[g] * lmat, xdt)
        prev = st_ref[h]
        y_off = jnp.exp(a_col) * _dot_nt(cg, prev)
        decay = jnp.exp(a_last - a_col)
        upd = lax.dot_general(xdt.astype(BF16), (bg * decay).astype(BF16), (((0,), (0,)), ((), ())),
                              preferred_element_type=F32)
        st_ref[h] = prev * jnp.exp(a_last) + upd
        ys.append(y_diag + y_off + dd_ref[:, h:h + 1] * xh)
    y = jnp.concatenate(ys, axis=1) * _silu(z_ref[...])
    half = SSD_INNER // SSD_GROUPS
    outs = []
    for g in range(SSD_GROUPS):
        yg = y[:, half * g:half * (g + 1)]
        outs.append(yg * lax.rsqrt(jnp.mean(yg * yg, axis=-1, keepdims=True) + EPS))
    y_ref[...] = jnp.concatenate(outs, axis=1) * nw_ref[...]


def ssd(xbc, z, dt, dt_t, conv_prev8, state0, cw, cb, dtb, alog, dd, nw, *, n_seq, seq_rows, chunk, valid_len):
    n_ch = seq_rows // chunk

    def pad128(v):
        return jnp.pad(v.reshape(1, -1), ((0, 0), (0, 128 - v.shape[-1])))

    tok = lambda w: pl.BlockSpec((chunk, w), lambda b, c: (b * n_ch + c, 0))
    full = lambda shp: pl.BlockSpec(shp, lambda b, c: (0,) * len(shp))
    return pl.pallas_call(
        functools.partial(_ssd_kernel, valid_len=valid_len),
        grid=(n_seq, n_ch),
        in_specs=[tok(SSD_CONV_DIM), tok(SSD_INNER), tok(128),
                  pl.BlockSpec((None, SSD_HEADS, chunk), lambda b, c: (b, 0, c)),
                  pl.BlockSpec((None, 8, SSD_CONV_DIM), lambda b, c: (b, 0, 0)),
                  pl.BlockSpec((None, SSD_HEADS, SSD_HEAD_DIM, SSD_STATE), lambda b, c: (b, 0, 0, 0)),
                  full((SSD_CONV, SSD_CONV_DIM)), full((1, SSD_CONV_DIM)), full((1, 128)), full((SSD_HEADS, 1)),
                  full((1, 128)), full((SSD_HEADS, 1)), full((1, 128)), full((1, SSD_INNER))],
        out_specs=[tok(SSD_INNER),
                   pl.BlockSpec((None, SSD_HEADS, SSD_HEAD_DIM, SSD_STATE), lambda b, c: (b, 0, 0, 0))],
        out_shape=[jax.ShapeDtypeStruct((n_seq * seq_rows, SSD_INNER), F32),
                   jax.ShapeDtypeStruct((n_seq, SSD_HEADS, SSD_HEAD_DIM, SSD_STATE), F32)],
        scratch_shapes=[pltpu.VMEM((chunk + 8, SSD_CONV_DIM), F32)],
        compiler_params=_cparams(("parallel", "arbitrary")),
        name="ssd",
    )(xbc, z, dt, dt_t, conv_prev8, state0, cw, cb.reshape(1, -1), pad128(dtb), dtb.reshape(-1, 1),
      pad128(alog), alog.reshape(-1, 1), pad128(dd), nw.reshape(1, -1))


def _mix_kernel(om_ref, on_ref, ys_ref, gbr_ref, x_ref, g1_ref, sc2_ref, sh2_ref, nw1_ref, nw2_ref,
                wbm_ref, wbn_ref, wbs_ref, wo_ref, rwt_ref, x1_ref, h2_ref, h2t_ref, lg_ref):
    d = D_MODEL
    ya = _dot(om_ref[...], wbm_ref[...])
    yb = _dot(on_ref[...], wbn_ref[...])
    yc = _dot(ys_ref[...], wbs_ref[...])
    merged = gbr_ref[:, 0:d] * ya + gbr_ref[:, d:2 * d] * yb + gbr_ref[:, 2 * d:3 * d] * yc
    m = _dot(merged, wo_ref[...])
    x1 = x_ref[...] + g1_ref[...] * _rms(m, nw1_ref[...])
    h2 = _rms(x1, nw2_ref[...]) * (1.0 + sc2_ref[...]) + sh2_ref[...]
    x1_ref[...] = x1
    h2_ref[...] = h2
    for s in range(ROW_TILES):
        h2t_ref[pl.ds(s, TM, stride=ROW_TILES), :] = h2[:, 128 * s:128 * (s + 1)]
    lg_ref[...] = _dot_nt(rwt_ref[...], h2, precision=HI)


def mix_out(om, on, ys, gbr, x, modx, nw1, nw2, wbm, wbn, wbs, wo, rwt, group_of_tile):
    n, d = x.shape
    tok = lambda w: pl.BlockSpec((TM, w), lambda i: (i, 0))
    full = lambda a: pl.BlockSpec(a.shape, lambda i: (0,) * a.ndim, pipeline_mode=pl.Buffered(1))
    return pl.pallas_call(
        _mix_kernel,
        grid=(n // TM,),
        in_specs=[tok(256), tok(256), tok(SSD_INNER), tok(3 * d), tok(d),
                  _mod_spec(2, group_of_tile), _mod_spec(4, group_of_tile), _mod_spec(3, group_of_tile),
                  full(nw1), full(nw2), full(wbm), full(wbn), full(wbs), full(wo), full(rwt)],
        out_specs=[tok(d), tok(d), pl.BlockSpec((TM * ROW_TILES, 128), lambda i: (i, 0)),
                   pl.BlockSpec((N_EXPERTS, TM), lambda i: (0, i))],
        out_shape=[jax.ShapeDtypeStruct((n, d), F32), jax.ShapeDtypeStruct((n, d), F32),
                   jax.ShapeDtypeStruct((n * ROW_TILES, 128), F32), jax.ShapeDtypeStruct((N_EXPERTS, n), F32)],
        compiler_params=_cparams(("parallel",)),
        name="mix_out",
    )(om, on, ys, gbr, x, modx, modx, modx, nw1, nw2, wbm, wbn, wbs, wo, rwt)


def _router_kernel(lg_ref, rb_ref, eidx_ref, w8_ref, pos_ref, cnt_ref, carry):
    i = pl.program_id(0)
    tm = lg_ref.shape[1]
    per = N_EXPERTS // N_ROUTE_GROUPS

    @pl.when(i == 0)
    def _zero():
        carry[...] = jnp.zeros(carry.shape, F32)

    s = _sigmoid(lg_ref[...])
    sc = s + rb_ref[...]
    sub = lax.broadcasted_iota(jnp.int32, (per, tm), 0).astype(F32)
    gs_rows = []
    for g in range(N_ROUTE_GROUPS):
        x = sc[per * g:per * (g + 1), :]
        m1 = jnp.max(x, axis=0, keepdims=True)
        i1 = jnp.min(jnp.where(x == m1, sub, float(per)), axis=0, keepdims=True)
        m2 = jnp.max(jnp.where(sub == i1, -jnp.inf, x), axis=0, keepdims=True)
        gs_rows.append(m1 + m2)
    gs = jnp.concatenate(gs_rows, axis=0)
    gsel = jnp.zeros_like(gs)
    for _ in range(TOPK_GROUPS):
        m = jnp.max(gs, axis=0, keepdims=True)
        ix = jnp.min(jnp.where(gs == m, sub, float(N_ROUTE_GROUPS)), axis=0, keepdims=True)
        hit = sub == ix
        gsel = jnp.where(hit, 1.0, gsel)
        gs = jnp.where(hit, -jnp.inf, gs)
    emask = jnp.concatenate([jnp.broadcast_to(gsel[g:g + 1, :], (per, tm)) for g in range(N_ROUTE_GROUPS)], axis=0)
    msc = jnp.where(emask > 0.0, sc, -jnp.inf)
    e_io = lax.broadcasted_iota(jnp.int32, (N_EXPERTS, tm), 0).astype(F32)
    sel = jnp.zeros_like(sc)
    idxs = []
    for _ in range(TOP_K):
        m = jnp.max(msc, axis=0, keepdims=True)
        ix = jnp.min(jnp.where(msc == m, e_io, float(N_EXPERTS)), axis=0, keepdims=True)
        hit = e_io == ix
        sel = jnp.where(hit, 1.0, sel)
        msc = jnp.where(hit, -jnp.inf, msc)
        idxs.append(ix)
    w = s * sel
    wn = w / jnp.sum(w, axis=0, keepdims=True) * ROUTE_SCALE
    r = lax.broadcasted_iota(jnp.int32, (tm, tm), 0)
    cidx = lax.broadcasted_iota(jnp.int32, (tm, tm), 1)
    upper = jnp.where(r <= cidx, 1.0, 0.0)
    cum = _dot(sel, upper)
    rank = cum - sel + carry[:, 0:1]
    eidx_rows, w_rows, p_rows = [], [], []
    for k in range(TOP_K):
        hit = e_io == idxs[k]
        eidx_rows.append(idxs[k])
        w_rows.append(jnp.sum(jnp.where(hit, wn, 0.0), axis=0, keepdims=True))
        p_rows.append(jnp.sum(jnp.where(hit, rank, 0.0), axis=0, keepdims=True))
    eidx_ref[...] = jnp.concatenate(eidx_rows, axis=0).astype(jnp.int32)
    w8_ref[...] = jnp.concatenate(w_rows, axis=0)
    pos_ref[...] = jnp.concatenate(p_rows, axis=0).astype(jnp.int32)
    carry[...] = carry[...] + jnp.sum(sel, axis=1, keepdims=True)
    cnt_ref[...] = carry[...]


def router(logits_t, router_b):
    ne, n = logits_t.shape
    return pl.pallas_call(
        _router_kernel,
        grid=(n // TM,),
        in_specs=[pl.BlockSpec((ne, TM), lambda i: (0, i)), pl.BlockSpec((ne, 1), lambda i: (0, 0))],
        out_specs=[pl.BlockSpec((TOP_K, TM), lambda i: (0, i)), pl.BlockSpec((TOP_K, TM), lambda i: (0, i)),
                   pl.BlockSpec((TOP_K, TM), lambda i: (0, i)), pl.BlockSpec((ne, 128), lambda i: (0, 0))],
        out_shape=[jax.ShapeDtypeStruct((TOP_K, n), jnp.int32), jax.ShapeDtypeStruct((TOP_K, n), F32),
                   jax.ShapeDtypeStruct((TOP_K, n), jnp.int32), jax.ShapeDtypeStruct((ne, 128), F32)],
        scratch_shapes=[pltpu.VMEM((ne, 128), F32)],
        compiler_params=_cparams(("arbitrary",)),
        name="router",
    )(logits_t, router_b.reshape(ne, 1))


def _dispatch_kernel(dest_ref, h_ref, init_ref, rows_ref, sem):
    del init_ref
    n_pairs = dest_ref.shape[1]

    def copy(j):
        src = pl.multiple_of(lax.shift_right_logical(j, 3) * ROW_TILES, ROW_TILES)
        dst = pl.multiple_of(dest_ref[0, j] * ROW_TILES, ROW_TILES)
        return pltpu.make_async_copy(h_ref.at[pl.ds(src, ROW_TILES)], rows_ref.at[pl.ds(dst, ROW_TILES)], sem)

    def start(jj, carry):
        for u in range(TOP_K):
            copy(jj * TOP_K + u).start(priority=u % 2)
        return carry

    def wait(j, carry):
        copy(j).wait()
        return carry

    lax.fori_loop(0, n_pairs // TOP_K, start, 0)
    lax.fori_loop(0, n_pairs, wait, 0, unroll=8)


def dispatch(h2t, dest, n_rows):
    n = h2t.shape[0] // ROW_TILES
    n_tiles = n // TM
    rows0 = jnp.zeros((n_rows * ROW_TILES, 128), F32)
    return pl.pallas_call(
        _dispatch_kernel,
        grid=(n_tiles,),
        in_specs=[pl.BlockSpec((None, 1, TM * TOP_K), lambda i: (i, 0, 0), memory_space=pltpu.SMEM),
                  pl.BlockSpec((TM * ROW_TILES, 128), lambda i: (i, 0)),
                  pl.BlockSpec(memory_space=pl.ANY)],
        out_specs=pl.BlockSpec(memory_space=pl.ANY),
        out_shape=jax.ShapeDtypeStruct((n_rows * ROW_TILES, 128), F32),
        scratch_shapes=[pltpu.SemaphoreType.DMA(())],
        input_output_aliases={2: 0},
        compiler_params=_cparams(("arbitrary",)),
        name="dispatch",
    )(dest.reshape(n_tiles, 1, TM * TOP_K), h2t, rows0)


def _expert_kernel(be_ref, nu_ref, x_ref, wg_ref, wu_ref, wd_ref, y_ref):
    @pl.when(pl.program_id(0) < nu_ref[0])
    def _():
        de = wg_ref.shape[1]
        g = jnp.zeros((EXPERT_ROWS, de), F32)
        u = jnp.zeros((EXPERT_ROWS, de), F32)
        for s in range(0, ROW_TILES, 2):
            xs = jnp.concatenate([x_ref[pl.ds(s, EXPERT_ROWS, stride=ROW_TILES), :],
                                  x_ref[pl.ds(s + 1, EXPERT_ROWS, stride=ROW_TILES), :]], axis=1).astype(BF16)
            g = g + jnp.dot(xs, wg_ref[128 * s:128 * (s + 2), :].astype(BF16), preferred_element_type=F32)
            u = u + jnp.dot(xs, wu_ref[128 * s:128 * (s + 2), :].astype(BF16), preferred_element_type=F32)
        y = _dot(_silu(g) * u, wd_ref[...])
        for s in range(ROW_TILES):
            y_ref[pl.ds(s, EXPERT_ROWS, stride=ROW_TILES), :] = y[:, 128 * s:128 * (s + 1)]

    @pl.when(pl.program_id(0) >= nu_ref[0])
    def _():
        y_ref[...] = jnp.zeros(y_ref.shape, F32)


def experts(x_rows, blk_e, n_used, wg, wu, wd, layer):
    n_rows = x_rows.shape[0] // ROW_TILES
    n_blk = n_rows // EXPERT_ROWS
    ne, d, de = wg.shape[1:]
    wg2 = wg.reshape(-1, d, de)
    wu2 = wu.reshape(-1, d, de)
    wd2 = wd.reshape(-1, de, d)
    gs = pltpu.PrefetchScalarGridSpec(
        num_scalar_prefetch=2,
        grid=(n_blk,),
        in_specs=[pl.BlockSpec((EXPERT_ROWS * ROW_TILES, 128), lambda i, be, nu: (i, 0)),
                  pl.BlockSpec((None, d, de), lambda i, be, nu: (layer * ne + be[i], 0, 0)),
                  pl.BlockSpec((None, d, de), lambda i, be, nu: (layer * ne + be[i], 0, 0)),
                  pl.BlockSpec((None, de, d), lambda i, be, nu: (layer * ne + be[i], 0, 0))],
        out_specs=pl.BlockSpec((EXPERT_ROWS * ROW_TILES, 128), lambda i, be, nu: (i, 0)),
    )
    return pl.pallas_call(
        _expert_kernel,
        grid_spec=gs,
        out_shape=jax.ShapeDtypeStruct((n_rows * ROW_TILES, 128), F32),
        compiler_params=_cparams(("arbitrary",)),
        name="experts",
    )(blk_e, n_used, x_rows, wg2, wu2, wd2)


def _combine_kernel(dest_ref, w8_ref, x1_ref, h2_ref, g2_ref, nw_ref, wsg_ref, wsu_ref, wsd_ref, yrows_ref,
                    out_ref, buf, sem):
    n_pairs = dest_ref.shape[1]

    def copy(j):
        src = pl.multiple_of(dest_ref[0, j] * ROW_TILES, ROW_TILES)
        dst = pl.multiple_of(lax.shift_right_logical(j, 3) * ROW_TILES, ROW_TILES)
        k = jnp.bitwise_and(j, TOP_K - 1)
        return pltpu.make_async_copy(yrows_ref.at[pl.ds(src, ROW_TILES)], buf.at[k, pl.ds(dst, ROW_TILES)], sem)

    def start(jj, carry):
        for u in range(TOP_K):
            copy(jj * TOP_K + u).start(priority=u % 2)
        return carry

    def wait(j, carry):
        copy(j).wait()
        return carry

    lax.fori_loop(0, n_pairs // TOP_K, start, 0)
    h2 = h2_ref[...]
    shared = _dot(_silu(_dot(h2, wsg_ref[...])) * _dot(h2, wsu_ref[...]), wsd_ref[...])
    lax.fori_loop(0, n_pairs, wait, 0, unroll=8)
    w8 = w8_ref[...]
    pieces = []
    for s in range(ROW_TILES):
        acc = w8[:, 0:1] * buf[0, pl.ds(s, TM, stride=ROW_TILES), :]
        for k in range(1, TOP_K):
            acc = acc + w8[:, k:k + 1] * buf[k, pl.ds(s, TM, stride=ROW_TILES), :]
        pieces.append(acc)
    routed = jnp.concatenate(pieces, axis=1)
    out_ref[...] = x1_ref[...] + g2_ref[...] * _rms(routed + shared, nw_ref[...])


def combine(dest, w8, x1, h2, modx, nw3, wsg, wsu, wsd, y_rows, group_of_tile):
    n, d = x1.shape
    n_tiles = n // TM
    tok = lambda w: pl.BlockSpec((TM, w), lambda i: (i, 0))
    full = lambda a: pl.BlockSpec(a.shape, lambda i: (0,) * a.ndim)
    return pl.pallas_call(
        _combine_kernel,
        grid=(n_tiles,),
        in_specs=[pl.BlockSpec((None, 1, TM * TOP_K), lambda i: (i, 0, 0), memory_space=pltpu.SMEM),
                  tok(TOP_K), tok(d), tok(d), _mod_spec(5, group_of_tile),
                  full(nw3), full(wsg), full(wsu), full(wsd),
                  pl.BlockSpec(memory_space=pl.ANY)],
        out_specs=tok(d),
        out_shape=jax.ShapeDtypeStruct((n, d), F32),
        scratch_shapes=[pltpu.VMEM((TOP_K, TM * ROW_TILES, 128), F32), pltpu.SemaphoreType.DMA(())],
        compiler_params=_cparams(("arbitrary",)),
        name="combine",
    )(dest.reshape(n_tiles, 1, TM * TOP_K), w8, x1, h2, modx, nw3, wsg, wsu, wsd, y_rows)


def _pack_w_in(w):
    d = w.shape[0]
    z = lambda n: jnp.zeros((d, n), w.dtype)
    parts = [w[:, 0:1408], w[:, 1408:1420], z(116), w[:, 1420:1932], w[:, 1932:2956], w[:, 2956:2964], z(120),
             w[:, 2964:6036]]
    return jnp.concatenate(parts, axis=1).astype(BF16)


def _expand_w1(w1):
    half = NSA_CMP_STRIDE * HEAD_DIM
    w = jnp.zeros((NSA_CMP_STRIDE, 4, HEAD_DIM, 256), w1.dtype)
    for kv in range(2):
        w = w.at[:, kv, :, 128 * kv:128 * kv + 64].set(w1[kv, :half].reshape(NSA_CMP_STRIDE, HEAD_DIM, HEAD_DIM))
        w = w.at[:, kv, :, 128 * kv + 64:128 * kv + 128].set(w1[kv, half:].reshape(NSA_CMP_STRIDE, HEAD_DIM, HEAD_DIM))
    return w.reshape(NSA_CMP_STRIDE * 256, 256).astype(BF16)


def _cmp_w1_rows(w1):
    half = NSA_CMP_STRIDE * HEAD_DIM
    w = jnp.zeros((NSA_CMP_STRIDE, 2, HEAD_DIM, 256), w1.dtype)
    for kv in range(2):
        w = w.at[:, kv, :, 128 * kv:128 * kv + 64].set(w1[kv, :half].reshape(NSA_CMP_STRIDE, HEAD_DIM, HEAD_DIM))
        w = w.at[:, kv, :, 128 * kv + 64:128 * kv + 128].set(w1[kv, half:].reshape(NSA_CMP_STRIDE, HEAD_DIM, HEAD_DIM))
    return w.reshape(NSA_CMP_STRIDE * 128, 256).astype(BF16)


PROMPT_SLOTS = 8
SAMPLE_SLOTS = 16


def kernel(x_prompt, x_sample, c_prompt, c_sample, cache_moba_kv, cache_nsa_kv, cache_nsa_win, state_ssd_conv, state_ssd, page_table, rel_bias, ada_w, ada_b, norm_w, w_in, nsa_cmp_w1, nsa_cmp_b1, nsa_cmp_w2, nsa_cmp_b2, nsa_cmp_pos, ssd_conv_w, ssd_conv_b, ssd_dt_bias, ssd_a_log, ssd_d, ssd_norm_w, w_branch_moba, w_branch_nsa, w_branch_ssd, w_out, router_w, router_b, exp_w_gate, exp_w_up, exp_w_down, shared_w_gate, shared_w_up, shared_w_down):
    depth = w_in.shape[0]
    bp, lp, d = x_prompt.shape
    bs, ls, _ = x_sample.shape
    n_p = bp * lp
    n_s = bs * ls
    n = n_p + n_s
    assert n_s == TM and lp % TM == 0 and lp % SSD_CHUNK == 0
    n_pages = page_table.shape[1]
    past = n_pages * PAGE
    n_pool = cache_moba_kv.shape[1]
    tiles_p = lp // TM
    n_ptiles = n_p // TM

    def group_of_tile(i):
        return jnp.where(i < n_ptiles, i // tiles_p, bp)

    x = jnp.concatenate([x_prompt.reshape(n_p, d), x_sample.reshape(n_s, d)], axis=0)
    c_all = jnp.concatenate([c_prompt, c_sample, jnp.zeros((4, d), F32)], axis=0)
    kv_cache = jnp.transpose(cache_moba_kv, (0, 1, 3, 4, 5, 2))
    ns_cache = jnp.transpose(cache_nsa_kv, (0, 1, 3, 4, 2))
    pt_flat = page_table.reshape(-1).astype(jnp.int32)
    prompt_pages = jnp.arange(bp * (lp // PAGE), dtype=jnp.int32)

    q_chunk = 128
    tiles_p_bias = bias_tiles(rel_bias, q_chunk)
    tiles_s_bias = bias_tiles(rel_bias, ls)

    n_blk = -(-(n * TOP_K + N_EXPERTS * (EXPERT_ROWS - 1)) // EXPERT_ROWS)
    n_rows = n_blk * EXPERT_ROWS

    sample_pad = 128
    outs = [[] for _ in range(10)]
    for l in range(depth):
        mod = modulation(c_all, ada_w, ada_b, l)
        modx = jnp.concatenate([jnp.broadcast_to(mod[:bp, None, :], (bp, TM, 6 * d)),
                                jnp.repeat(mod[bp:bp + bs], ls, axis=0)[None]], axis=0)
        nw = norm_w[l]
        qm, kvm, qn, nsa, win, gn, z, xbc, dt, gbr = proj_in(x, modx, nw[0:1], _pack_w_in(w_in[l]), group_of_tile)
        w_exp = _expand_w1(nsa_cmp_w1[l])

        kvm_pages = kvm.reshape(n // PAGE, PAGE, 512)
        nsa_pages = nsa.reshape(n // PAGE, PAGE, 256)
        win_pages = win.reshape(n // PAGE, PAGE, 128)
        ks_p, pab_p = page_ctx(kvm_pages, nsa_pages, w_exp, 0, n_p // PAGE)
        km_p, cmp_p = ctx_final(ks_p.reshape(bp, lp // PAGE, 256), pab_p.reshape(bp, lp // NSA_CMP_STRIDE, 256),
                                nsa_cmp_pos, nsa_cmp_w1, nsa_cmp_b1, nsa_cmp_w2, nsa_cmp_b2, l, lp // PAGE)
        om_p, on_p = attention_slots(rel_bias, qm, qn, gn, 0, kvm_pages, nsa_pages, prompt_pages, None, None,
                                     win_pages, 0, km_p, cmp_p, tiles_p_bias,
                                     n_seq=bp, q_rows=q_chunk, n_chunks=lp // q_chunk, q0_base=0,
                                     tiles_per_seq=lp // PAGE, n_win_tiles=lp // PAGE,
                                     n_sel_blocks=lp // NSA_SEL_BLOCK, n_slots=PROMPT_SLOTS)

        ks_c, pab_c = page_ctx_cache(kv_cache, ns_cache, _cmp_w1_rows(nsa_cmp_w1[l]), l)
        ks_s, pab_s = ctx_gather(pt_flat, ks_c, pab_c, bs, n_pages)
        km_s, cmp_s = ctx_final(ks_s, pab_s, nsa_cmp_pos, nsa_cmp_w1, nsa_cmp_b1, nsa_cmp_w2, nsa_cmp_b2, l, n_pages)
        pad_rows = lambda a: jnp.pad(a[n_p:].reshape(bs, ls, -1), ((0, 0), (0, PAGE - ls), (0, 0)))
        tail_kv = jnp.transpose(pad_rows(kvm).reshape(bs, PAGE, 2, MOBA_HEADS, HEAD_DIM), (0, 2, 3, 4, 1))
        tail_ns = jnp.transpose(pad_rows(nsa).reshape(bs, PAGE, 4, HEAD_DIM)[:, :, 2:], (0, 2, 3, 1))
        win_s = jnp.concatenate([cache_nsa_win[l].reshape(bs, -1, 128), pad_rows(win)], axis=1)
        win_s = win_s.reshape(bs * (win_s.shape[1] // PAGE), PAGE, 128)
        lf_s = -(-(past + ls) // MOBA_BLOCK) * MOBA_BLOCK
        om_s, on_s = attention_slots(rel_bias, qm, qn, gn, n_p // ls, kv_cache, ns_cache, pt_flat,
                                     tail_kv, tail_ns, win_s, (past - NSA_WINDOW) // PAGE, km_s, cmp_s, tiles_s_bias,
                                     n_seq=bs, q_rows=ls, n_chunks=1, q0_base=past,
                                     tiles_per_seq=n_pages, n_win_tiles=win_s.shape[0] // bs,
                                     n_sel_blocks=lf_s // NSA_SEL_BLOCK, n_slots=SAMPLE_SLOTS, cache_layer=l)
        om = jnp.concatenate([om_p, om_s], axis=0)
        on = jnp.concatenate([on_p, on_s], axis=0)

        dt_t = dt[:, :SSD_HEADS].T
        ssd_par = (ssd_conv_w[l], ssd_conv_b[l], ssd_dt_bias[l], ssd_a_log[l], ssd_d[l], ssd_norm_w[l])
        y_p, st_p = ssd(xbc, z, dt, dt_t[:, :n_p].reshape(SSD_HEADS, bp, lp).transpose(1, 0, 2),
                        jnp.zeros((bp, 8, SSD_CONV_DIM), F32),
                        jnp.zeros((bp, SSD_HEADS, SSD_HEAD_DIM, SSD_STATE), F32), *ssd_par,
                        n_seq=bp, seq_rows=lp, chunk=SSD_CHUNK, valid_len=lp)
        pad_s = lambda a: jnp.pad(a[n_p:].reshape(bs, ls, -1), ((0, 0), (0, sample_pad - ls), (0, 0))).reshape(bs * sample_pad, -1)
        dt_t_s = jnp.pad(dt_t[:, n_p:].reshape(SSD_HEADS, bs, ls).transpose(1, 0, 2), ((0, 0), (0, 0), (0, sample_pad - ls)))
        conv_prev = jnp.pad(state_ssd_conv[l], ((0, 0), (8 - (SSD_CONV - 1), 0), (0, 0)))
        y_s, st_s = ssd(pad_s(xbc), pad_s(z), pad_s(dt), dt_t_s, conv_prev, state_ssd[l], *ssd_par,
                        n_seq=bs, seq_rows=sample_pad, chunk=sample_pad, valid_len=ls)
        ys = jnp.concatenate([y_p[:n_p], y_s.reshape(bs, sample_pad, -1)[:, :ls].reshape(n_s, -1)], axis=0)

        x1, h2, h2t, logits_t = mix_out(om, on, ys, gbr, x, modx, nw[1:2], nw[2:3],
                                        w_branch_moba[l].astype(BF16), w_branch_nsa[l].astype(BF16),
                                        w_branch_ssd[l].astype(BF16), w_out[l].astype(BF16), router_w[l].T,
                                        group_of_tile)
        eidx, w8, pos8, cnt = router(logits_t, router_b[l])
        cnt = cnt[:, 0].astype(jnp.int32)
        padded = (cnt + EXPERT_ROWS - 1) // EXPERT_ROWS * EXPERT_ROWS
        ends = jnp.cumsum(padded)
        off = ends - padded
        e_ids = jnp.arange(N_EXPERTS, dtype=jnp.int32)
        off_of = jnp.sum(jnp.where(eidx[:, :, None] == e_ids, off, 0), axis=-1)
        dest = (off_of + pos8).T.reshape(-1)
        blk_start = jnp.arange(n_blk, dtype=jnp.int32) * EXPERT_ROWS
        blk_e = jnp.minimum(jnp.sum((ends[None, :] <= blk_start[:, None]).astype(jnp.int32), axis=1), N_EXPERTS - 1)
        n_used = (ends[-1] // EXPERT_ROWS).astype(jnp.int32).reshape(1)
        x_rows = dispatch(h2t, dest, n_rows)
        y_rows = experts(x_rows, blk_e, n_used, exp_w_gate, exp_w_up, exp_w_down, l)
        x = combine(dest, w8.T, x1, h2, modx, nw[3:4], shared_w_gate[l].astype(BF16), shared_w_up[l].astype(BF16),
                    shared_w_down[l].astype(BF16), y_rows, group_of_tile)

        keep = min(NSA_WINDOW, lp)
        outs[0].append(kvm[:n_p].reshape(bp, lp, 2, MOBA_HEADS, HEAD_DIM))
        outs[1].append(kvm[n_p:].reshape(bs, ls, 2, MOBA_HEADS, HEAD_DIM))
        outs[2].append(nsa[:n_p].reshape(bp, lp, 4, HEAD_DIM))
        outs[3].append(nsa[n_p:].reshape(bs, ls, 4, HEAD_DIM))
        outs[4].append(win[:n_p].reshape(bp, lp, 2, HEAD_DIM)[:, lp - keep:])
        outs[5].append(win[n_p:].reshape(bs, ls, 2, HEAD_DIM))
        outs[6].append(xbc[:n_p].reshape(bp, lp, -1)[:, lp - (SSD_CONV - 1):])
        outs[7].append(xbc[n_p:].reshape(bs, ls, -1)[:, ls - (SSD_CONV - 1):])
        outs[8].append(st_p)
        outs[9].append(st_s)

    y_prompt = x[:n_p].reshape(bp, lp, d)
    y_sample = x[n_p:].reshape(bs, ls, d)
    return (y_prompt, y_sample) + tuple(jnp.stack(o) for o in outs)
```

```python
import functools
import math
from typing import NamedTuple

import numpy as np
import jax
import jax.numpy as jnp
from jax import lax
from jax.experimental import pallas as pl
from jax.experimental.pallas import tpu as pltpu

F32 = jnp.float32
BF16 = jnp.bfloat16
HI = lax.Precision.HIGHEST

D_MODEL = 1024
PAGE = 128
HEAD_DIM = 64
MOBA_HEADS = 4
MOBA_BLOCK = 256
MOBA_TOPK = 3
NSA_HEADS = 4
NSA_CMP_STRIDE = 16
NSA_SEL_BLOCK = 64
NSA_TOPN = 16
NSA_WINDOW = 512
SSD_HEADS = 8
SSD_HEAD_DIM = 64
SSD_INNER = 512
SSD_GROUPS = 2
SSD_STATE = 128
SSD_CONV = 4
SSD_CHUNK = 256
SSD_CONV_DIM = 1024
N_BUCKETS = 32
MAX_DISTANCE = 128
N_EXPERTS = 64
TOP_K = 8
N_ROUTE_GROUPS = 8
TOPK_GROUPS = 4
D_EXPERT = 256
ROUTE_SCALE = 2.5
EPS = 1e-6

TM = 256
ROW_TILES = D_MODEL // 128
EXPERT_ROWS = 512
VMEM_LIMIT = 56 * 1024 * 1024
NEG = -1e30

_SEG = dict(qm=(0, 256), kvm=(256, 512), qn=(768, 256), nsa=(1024, 256), win=(1280, 128),
            gn=(1408, 128), z=(1536, 512), xbc=(2048, 1024), dt=(3072, 128), gbr=(3200, 3072))
PROJ_W = 6272


def _sigmoid(x):
    return 1.0 / (1.0 + jnp.exp(-x))


def _silu(x):
    return x * _sigmoid(x)


def _rms(x, w):
    return x * lax.rsqrt(jnp.mean(x * x, axis=-1, keepdims=True) + EPS) * w


def _dot(a, b):
    return jnp.dot(a.astype(BF16), b.astype(BF16), preferred_element_type=F32)


def _dot_nt(a, b, precision=None):
    if precision is None:
        a, b = a.astype(BF16), b.astype(BF16)
    return lax.dot_general(a, b, (((1,), (1,)), ((), ())), precision=precision,
                           preferred_element_type=F32)


def _cparams(sem, vmem=None):
    return pltpu.CompilerParams(dimension_semantics=sem, vmem_limit_bytes=vmem or VMEM_LIMIT)


def _mod_kernel(c_ref, w_ref, b_ref, o_ref):
    o_ref[...] = _dot(_silu(c_ref[...]), w_ref[...]) + b_ref[...]


def modulation(c_all, ada_w, ada_b, layer):
    rows, d = c_all.shape
    n = ada_w.shape[-1]
    tn = 512
    return pl.pallas_call(
        _mod_kernel,
        grid=(n // tn,),
        in_specs=[pl.BlockSpec((rows, d), lambda j: (0, 0)),
                  pl.BlockSpec((None, d, tn), lambda j: (layer, 0, j)),
                  pl.BlockSpec((None, 1, tn), lambda j: (layer, 0, j))],
        out_specs=pl.BlockSpec((rows, tn), lambda j: (0, j)),
        out_shape=jax.ShapeDtypeStruct((rows, n), F32),
        compiler_params=_cparams(("parallel",)),
        name="modulation",
    )(c_all, ada_w, ada_b.reshape(ada_b.shape[0], 1, n))


class Mods(NamedTuple):
    prompt: jax.Array
    sample: jax.Array
    n_prompt_tiles: int
    tiles_per_seq: int


def _mod_specs(mods, which):
    last = mods.prompt.shape[0] - 1
    tps = mods.tiles_per_seq
    return [pl.BlockSpec((None, 8, D_MODEL), lambda i: (jnp.minimum(i // tps, last), 0, which)),
            pl.BlockSpec((TM, D_MODEL), lambda i: (0, which))]


def _mod(p_ref, s_ref, n_prompt_tiles):
    return jnp.where(pl.program_id(0) >= n_prompt_tiles, s_ref[...], p_ref[0:1, :])


def _proj_kernel(x_ref, shp_ref, shs_ref, scp_ref, scs_ref, nw_ref, w_ref, qm_ref, kvm_ref, qn_ref, nsa_ref,
                 win_ref, gn_ref, z_ref, xbc_ref, dt_ref, gbr_ref, *, n_prompt_tiles):
    sh = _mod(shp_ref, shs_ref, n_prompt_tiles)
    sc = _mod(scp_ref, scs_ref, n_prompt_tiles)
    h = _rms(x_ref[...], nw_ref[...]) * (1.0 + sc) + sh
    hb = h.astype(BF16)

    def seg(name):
        o, w = _SEG[name]
        return jnp.dot(hb, w_ref[:, o:o + w], preferred_element_type=F32)

    qm_ref[...] = seg("qm")
    kvm_ref[...] = seg("kvm")
    qn_ref[...] = seg("qn")
    nsa_ref[...] = seg("nsa")
    win_ref[...] = seg("win")
    gn_ref[...] = _sigmoid(seg("gn"))
    z_ref[...] = seg("z")
    xbc_ref[...] = seg("xbc")
    dt_ref[...] = seg("dt")
    gbr_ref[...] = _sigmoid(seg("gbr"))


def proj_in(x, mods, nw, w_packed):
    n, d = x.shape
    names = ["qm", "kvm", "qn", "nsa", "win", "gn", "z", "xbc", "dt", "gbr"]
    return pl.pallas_call(
        functools.partial(_proj_kernel, n_prompt_tiles=mods.n_prompt_tiles),
        grid=(n // TM,),
        in_specs=[pl.BlockSpec((TM, d), lambda i: (i, 0)),
                  *_mod_specs(mods, 0), *_mod_specs(mods, 1),
                  pl.BlockSpec((1, d), lambda i: (0, 0)),
                  pl.BlockSpec((d, PROJ_W), lambda i: (0, 0), pipeline_mode=pl.Buffered(1))],
        out_specs=[pl.BlockSpec((TM, _SEG[k][1]), lambda i: (i, 0)) for k in names],
        out_shape=[jax.ShapeDtypeStruct((n, _SEG[k][1]), F32) for k in names],
        compiler_params=_cparams(("parallel",)),
        name="proj_in",
    )(x, mods.prompt, mods.sample, mods.prompt, mods.sample, nw, w_packed)


CTX_PAGES = 16


def _page_ctx_kernel(k_ref, g_ref, w_ref, ks_ref, pab_ref):
    ks_ref[...] = jnp.sum(k_ref[...], axis=1)
    pab_ref[...] = _dot(g_ref[...], w_ref[...])


def page_ctx(kv_pages, nsa_pages, w_exp, page0, n_pages):
    groups = PAGE // NSA_CMP_STRIDE
    g_view = nsa_pages.reshape(nsa_pages.shape[0] * groups, NSA_CMP_STRIDE * 256)
    blk0 = page0 // CTX_PAGES
    return pl.pallas_call(
        _page_ctx_kernel,
        grid=(n_pages // CTX_PAGES,),
        in_specs=[pl.BlockSpec((CTX_PAGES, PAGE, 256), lambda i: (blk0 + i, 0, 0)),
                  pl.BlockSpec((CTX_PAGES * groups, NSA_CMP_STRIDE * 256), lambda i: (blk0 + i, 0)),
                  pl.BlockSpec((NSA_CMP_STRIDE * 256, 256), lambda i: (0, 0))],
        out_specs=[pl.BlockSpec((CTX_PAGES, 256), lambda i: (i, 0)),
                   pl.BlockSpec((CTX_PAGES * groups, 256), lambda i: (i, 0))],
        out_shape=[jax.ShapeDtypeStruct((n_pages, 256), F32),
                   jax.ShapeDtypeStruct((n_pages * groups, 256), F32)],
        compiler_params=_cparams(("parallel",)),
        name="page_ctx",
    )(kv_pages, g_view, w_exp)


def _page_ctx_cache_kernel(k_ref, c_ref, w_ref, ks_ref, pab_ref, tok):
    n_tok = CTX_PAGES * PAGE
    kt = jnp.concatenate([k_ref[p].reshape(MOBA_HEADS * HEAD_DIM, PAGE) for p in range(CTX_PAGES)], axis=1)
    page_of_lane = lax.shift_right_logical(lax.broadcasted_iota(jnp.int32, (CTX_PAGES, n_tok), 1), 7)
    ind = jnp.where(page_of_lane == lax.broadcasted_iota(jnp.int32, (CTX_PAGES, n_tok), 0), 1.0, 0.0)
    ks_ref[...] = _dot_nt(ind, kt)
    for p in range(CTX_PAGES):
        for c in range(2):
            tok[p * PAGE:(p + 1) * PAGE, HEAD_DIM * c:HEAD_DIM * (c + 1)] = c_ref[p, c].T
    groups = n_tok // NSA_CMP_STRIDE
    acc = jnp.zeros((groups, 256), F32)
    for r in range(NSA_CMP_STRIDE):
        acc = acc + _dot(tok[pl.ds(r, groups, stride=NSA_CMP_STRIDE), :], w_ref[128 * r:128 * (r + 1), :])
    pab_ref[...] = acc


def page_ctx_cache(kv_tiles, ns_tiles, w_cmp, layer):
    n_pages = kv_tiles.shape[1]
    groups = PAGE // NSA_CMP_STRIDE
    return pl.pallas_call(
        _page_ctx_cache_kernel,
        grid=(n_pages // CTX_PAGES,),
        in_specs=[pl.BlockSpec((None, CTX_PAGES, None, MOBA_HEADS, HEAD_DIM, PAGE), lambda i: (layer, i, 0, 0, 0, 0)),
                  pl.BlockSpec((None, CTX_PAGES, 2, HEAD_DIM, PAGE), lambda i: (layer, i, 0, 0, 0)),
                  pl.BlockSpec(w_cmp.shape, lambda i: (0, 0))],
        out_specs=[pl.BlockSpec((CTX_PAGES, 256), lambda i: (i, 0)),
                   pl.BlockSpec((CTX_PAGES * groups, 256), lambda i: (i, 0))],
        out_shape=[jax.ShapeDtypeStruct((n_pages, 256), F32),
                   jax.ShapeDtypeStruct((n_pages * groups, 256), F32)],
        scratch_shapes=[pltpu.VMEM((CTX_PAGES * PAGE, 128), F32)],
        compiler_params=_cparams(("parallel",)),
        name="page_ctx_cache",
    )(kv_tiles, ns_tiles, w_cmp)


def _ctx_gather_kernel(pt_ref, ks_hbm, pab_hbm, ks_out, pab_out, sem):
    b = pl.program_id(0)
    n_pages = ks_out.shape[1]
    groups = pab_out.shape[1] // n_pages

    def copies(p):
        page = pt_ref[b * n_pages + p]
        c1 = pltpu.make_async_copy(ks_hbm.at[pl.ds(page, 1)], ks_out.at[0, pl.ds(p, 1)], sem.at[0])
        c2 = pltpu.make_async_copy(pab_hbm.at[pl.ds(page * groups, groups)],
                                   pab_out.at[0, pl.ds(p * groups, groups)], sem.at[1])
        return c1, c2

    def start(p, carry):
        c1, c2 = copies(p)
        c1.start()
        c2.start()
        return carry

    def wait(p, carry):
        c1, c2 = copies(p)
        c1.wait()
        c2.wait()
        return carry

    lax.fori_loop(0, n_pages, start, 0)
    lax.fori_loop(0, n_pages, wait, 0)


def ctx_gather(page_table_flat, ksum, pab, n_seq, n_pages):
    groups = PAGE // NSA_CMP_STRIDE
    gs = pltpu.PrefetchScalarGridSpec(
        num_scalar_prefetch=1,
        grid=(n_seq,),
        in_specs=[pl.BlockSpec(memory_space=pl.ANY), pl.BlockSpec(memory_space=pl.ANY)],
        out_specs=[pl.BlockSpec((1, n_pages) + ksum.shape[1:], lambda b, pt: (b,) + (0,) * ksum.ndim),
                   pl.BlockSpec((1, n_pages * groups, 256), lambda b, pt: (b, 0, 0))],
        scratch_shapes=[pltpu.SemaphoreType.DMA((2,))],
    )
    return pl.pallas_call(
        _ctx_gather_kernel,
        grid_spec=gs,
        out_shape=[jax.ShapeDtypeStruct((n_seq, n_pages) + ksum.shape[1:], F32),
                   jax.ShapeDtypeStruct((n_seq, n_pages * groups, 256), F32)],
        compiler_params=_cparams(("arbitrary",)),
        name="ctx_gather",
    )(page_table_flat, ksum, pab)


def _gelu_tanh(x):
    return 0.5 * x * (1.0 + jnp.tanh(math.sqrt(2.0 / math.pi) * (x + 0.044715 * (x * x * x))))


def _ctx_final_kernel(ks_ref, pab_ref, pos_ref, w1_ref, b1_ref, w2_ref, b2_ref, km_ref, cmp_ref, *, head_rows):
    nb = km_ref.shape[0]
    n_pages = 2 * nb
    r = lax.broadcasted_iota(jnp.int32, (nb, n_pages), 0)
    c = lax.broadcasted_iota(jnp.int32, (nb, n_pages), 1)
    pair = jnp.where((c == 2 * r) | (c == 2 * r + 1), 1.0, 0.0).astype(F32)
    if head_rows:
        ks = jnp.concatenate([ks_ref[pl.ds(h, n_pages, stride=2 * MOBA_HEADS), :] for h in range(MOBA_HEADS)], axis=1)
    else:
        ks = ks_ref[...]
    km_ref[...] = jnp.dot(pair, ks, precision=HI, preferred_element_type=F32) * (1.0 / MOBA_BLOCK)

    pab = pab_ref[...]
    ng = pab.shape[0]
    row = lax.broadcasted_iota(jnp.int32, (ng, 1), 0)
    outs = []
    for kv in range(2):
        pa = pab[:, 128 * kv:128 * kv + 64]
        pb = pab[:, 128 * kv + 64:128 * kv + 128]
        pb_next = jnp.where(row == ng - 1, 0.0, pltpu.roll(pb, ng - 1, 0))
        pos_term = _dot(pos_ref[kv], w1_ref[kv]) + b1_ref[kv]
        hid = _gelu_tanh(pa + pb_next + pos_term)
        outs.append(_dot(hid, w2_ref[kv]) + b2_ref[kv])
    cmp_ref[...] = jnp.concatenate(outs, axis=1)


def ctx_final(ksum, pab, pos, w1, b1, w2, b2, layer, n_pages):
    nb = ksum.shape[0]
    head_rows = ksum.shape[-1] == HEAD_DIM
    ng = pab.shape[1]
    lf = pos.shape[2] * pos.shape[3]
    pos2 = pos.reshape(pos.shape[0], 2, 1, lf)
    return pl.pallas_call(
        functools.partial(_ctx_final_kernel, head_rows=head_rows),
        grid=(nb,),
        in_specs=[pl.BlockSpec((None,) + ksum.shape[1:], lambda b: (b, 0, 0)),
                  pl.BlockSpec((None, ng, 256), lambda b: (b, 0, 0)),
                  pl.BlockSpec((None, 2, 1, lf), lambda b: (layer, 0, 0, 0)),
                  pl.BlockSpec((None, 2, lf, HEAD_DIM), lambda b: (layer, 0, 0, 0)),
                  pl.BlockSpec((None, 2, 1, HEAD_DIM), lambda b: (layer, 0, 0, 0)),
                  pl.BlockSpec((None, 2, HEAD_DIM, HEAD_DIM), lambda b: (layer, 0, 0, 0)),
                  pl.BlockSpec((None, 2, 1, HEAD_DIM), lambda b: (layer, 0, 0, 0))],
        out_specs=[pl.BlockSpec((None, n_pages // 2, 256), lambda b: (b, 0, 0)),
                   pl.BlockSpec((None, ng, 128), lambda b: (b, 0, 0))],
        out_shape=[jax.ShapeDtypeStruct((nb, n_pages // 2, 256), F32),
                   jax.ShapeDtypeStruct((nb, ng, 128), F32)],
        compiler_params=_cparams(("parallel",)),
        name="ctx_final",
    )(ksum, pab, pos2, w1, b1.reshape(b1.shape[0], 2, 1, HEAD_DIM), w2, b2.reshape(b2.shape[0], 2, 1, HEAD_DIM))


def _topk_lanes(score, k, lane_f, n_lanes):
    sel = jnp.zeros_like(score)
    for _ in range(k):
        m = jnp.max(score, axis=1, keepdims=True)
        idx = jnp.min(jnp.where(score == m, lane_f, float(n_lanes)), axis=1, keepdims=True)
        hit = lane_f == idx
        sel = jnp.where(hit & (m > -jnp.inf), 1.0, sel)
        score = jnp.where(hit, -jnp.inf, score)
    return sel


def _rel_bucket_np(dist):
    n = np.maximum(dist, 0)
    exact = N_BUCKETS // 2
    nf = np.maximum(n, 1).astype(np.float32)
    large = exact + (np.log(nf / np.float32(exact)) / np.float32(math.log(MAX_DISTANCE / exact))
                     * np.float32(N_BUCKETS - exact)).astype(np.int32)
    return np.where(n < exact, n, np.minimum(large, N_BUCKETS - 1)).astype(np.int32)


WIN_TILES = NSA_WINDOW // PAGE + 1


def _bucket_thresholds():
    b = _rel_bucket_np(np.arange(4 * MAX_DISTANCE))
    return [int(np.argmax(b >= k)) for k in range(N_BUCKETS)]


def _softmax_update(m_ref, l_ref, a_ref, idx, s, valid, pv_fn):
    s = jnp.where(valid, s, NEG)
    m_old = m_ref[idx]
    m_new = jnp.maximum(m_old, jnp.max(s, axis=1, keepdims=True))
    p = jnp.exp(s - m_new)
    alpha = jnp.exp(m_old - m_new)
    l_ref[idx] = alpha * l_ref[idx] + jnp.sum(p, axis=1, keepdims=True)
    a_ref[idx] = alpha * a_ref[idx] + pv_fn(p)
    m_ref[idx] = m_new


def _pv_slots(vs, pv_dot):
    def f(p):
        acc = pv_dot(p[:, 0:PAGE], vs[0])
        for j in range(1, len(vs)):
            acc = acc + pv_dot(p[:, PAGE * j:PAGE * (j + 1)], vs[j])
        return acc
    return f


def _attn_slots_kernel(si_ref, st_ref, sl_ref, pg_ref, *refs, q_rows, q0_base, n_slots, cache_tiles, tail_tile):
    Q, G = q_rows, n_slots
    rb_ref, qm_ref, qn_ref, gn_ref = refs[0:4]
    kv_refs = refs[4:4 + G]
    ns_refs = refs[4 + G:4 + 2 * G]
    tkv_ref, tns_ref = refs[4 + 2 * G:6 + 2 * G]
    win_refs = refs[6 + 2 * G:6 + 2 * G + WIN_TILES]
    base = 6 + 2 * G + WIN_TILES
    km_ref, cmp_ref, tbm_ref, tbn_ref, pool_ref, om_ref, on_ref = refs[base:base + 7]
    qms, qns, qbd, selm, sels, m_m, l_m, a_m, m_n, l_n, a_n, oc, ow = refs[base + 7:]

    s_id = pl.program_id(1)
    i = si_ref[s_id]
    t0 = st_ref[s_id]
    q0 = q0_base + i * Q
    ob = lax.shift_right_logical(q0, 8)
    nb = km_ref.shape[0]
    nbs_p = pool_ref.shape[1]
    nc = cmp_ref.shape[0]
    R4 = NSA_HEADS * Q

    qi = lax.broadcasted_iota(jnp.int32, (Q, 1), 0)
    qi4 = jnp.concatenate([qi] * NSA_HEADS, axis=0)
    kj = lax.broadcasted_iota(jnp.int32, (Q, PAGE), 1)
    lane_nb = lax.broadcasted_iota(jnp.int32, (Q, nb), 1)
    lane_bs = lax.broadcasted_iota(jnp.int32, (Q, nbs_p), 1)
    rep4 = lambda x: jnp.concatenate([x] * NSA_HEADS, axis=0)

    @pl.when(t0 == 0)
    def _init():
        qm = qm_ref[...]
        qn = qn_ref[...]
        qms[...] = qm * (HEAD_DIM ** -0.5)
        qbd[...] = jnp.zeros(qbd.shape, F32)
        for h in range(NSA_HEADS):
            sl = slice(HEAD_DIM * h, HEAD_DIM * (h + 1))
            qns[h * Q:(h + 1) * Q, :] = qn[:, sl] * (HEAD_DIM ** -0.5)
            qbd[h * Q:(h + 1) * Q, sl] = qm[:, sl] * (HEAD_DIM ** -0.5)
        qn4 = qns[...]
        km = km_ref[...]
        lane_f = lane_nb.astype(F32)
        for h in range(MOBA_HEADS):
            sl = slice(HEAD_DIM * h, HEAD_DIM * (h + 1))
            g = _dot_nt(qm[:, sl], km[:, sl], precision=HI)
            g = jnp.where(lane_nb < ob, g, -jnp.inf)
            selm[h] = _topk_lanes(g, min(MOBA_TOPK, nb), lane_f, nb)
        cm = cmp_ref[...]
        epos = lax.broadcasted_iota(jnp.int32, (Q, nc), 1) * NSA_CMP_STRIDE + (2 * NSA_CMP_STRIDE - 1)
        dc = q0 + qi - epos
        thr = _bucket_thresholds()
        bias = [jnp.full((Q, nc), rb_ref[N_BUCKETS - 1, MOBA_HEADS + h], F32) for h in range(NSA_HEADS)]
        for k in range(N_BUCKETS - 2, -1, -1):
            below = dc < thr[k + 1]
            bias = [jnp.where(below, rb_ref[k, MOBA_HEADS + h], bias[h]) for h in range(NSA_HEADS)]
        sc = _dot_nt(qn4, cm[:, :HEAD_DIM]) + jnp.concatenate(bias, axis=0)
        valid = rep4(dc >= 0)
        sc = jnp.where(valid, sc, NEG)
        mx = jnp.max(sc, axis=1, keepdims=True)
        e = jnp.where(valid, jnp.exp(sc - mx), 0.0)
        den = jnp.sum(e, axis=1, keepdims=True)
        p = e / jnp.where(den > 0, den, 1.0)
        oc[...] = _dot(p, cm[:, HEAD_DIM:])
        psum = p[0:Q] + p[Q:2 * Q] + p[2 * Q:3 * Q] + p[3 * Q:4 * Q]
        imp = jnp.dot(psum, pool_ref[...], precision=HI, preferred_element_type=F32)
        cur = lax.shift_right_logical(q0 + qi, 6)
        forced = (lane_bs == 0) | (lane_bs == cur) | (lane_bs == cur - 1)
        score = jnp.where(forced, jnp.inf, jnp.where(lane_bs <= cur, imp, -jnp.inf))
        sels[...] = _topk_lanes(score, NSA_TOPN, lane_bs.astype(F32), nbs_p)
        tw0 = lax.shift_right_logical(q0, 7) - (WIN_TILES - 1)
        s_parts, v_parts, m_parts = [], [], []
        for w in range(WIN_TILES):
            wv = win_refs[w][...]
            back = WIN_TILES - 1 - w
            dw = back * PAGE + qi - kj
            s_parts.append(_dot_nt(qn4, wv[:, :HEAD_DIM]) + tbn_ref[min(back, 2)])
            m_parts.append((dw >= 0) & (dw < NSA_WINDOW) & (tw0 + w >= 0))
            v_parts.append(wv[:, HEAD_DIM:])
        sw = jnp.concatenate(s_parts, axis=1)
        wvalid = rep4(jnp.concatenate(m_parts, axis=1))
        sw = jnp.where(wvalid, sw, NEG)
        mx = jnp.max(sw, axis=1, keepdims=True)
        e = jnp.where(wvalid, jnp.exp(sw - mx), 0.0)
        den = jnp.sum(e, axis=1, keepdims=True)
        pw = e / jnp.where(den > 0, den, 1.0)
        o_w = _dot(pw[:, 0:PAGE], v_parts[0])
        for w in range(1, WIN_TILES):
            o_w = o_w + _dot(pw[:, PAGE * w:PAGE * (w + 1)], v_parts[w])
        ow[...] = o_w
        m_m[...] = jnp.full(m_m.shape, NEG, F32)
        l_m[...] = jnp.zeros(l_m.shape, F32)
        a_m[...] = jnp.zeros(a_m.shape, F32)
        m_n[...] = jnp.full(m_n.shape, NEG, F32)
        l_n[...] = jnp.zeros(l_n.shape, F32)
        a_n[...] = jnp.zeros(a_n.shape, F32)

    is_tail = (t0 == tail_tile) if tail_tile is not None else None

    def pick(j, cache_fn, tail_fn):
        x = cache_fn(kv_refs[j], ns_refs[j])
        if is_tail is not None and j == 0:
            x = jnp.where(is_tail, tail_fn(tkv_ref, tns_ref), x)
        return x

    if cache_tiles:
        k_of = lambda j, h: pick(j, *[lambda kv, ns: kv[0, h]] * 2)
        v_of = lambda j, h: pick(j, *[lambda kv, ns: kv[1, h]] * 2)
        ksel_of = lambda j: pick(j, *[lambda kv, ns: ns[0]] * 2)
        vsel_of = lambda j: pick(j, *[lambda kv, ns: ns[1]] * 2)
        qk, pv = _dot, _dot_nt
    else:
        k_of = lambda j, h: kv_refs[j][:, HEAD_DIM * h:HEAD_DIM * (h + 1)]
        v_of = lambda j, h: kv_refs[j][:, 256 + HEAD_DIM * h:256 + HEAD_DIM * (h + 1)]
        ksel_of = lambda j: ns_refs[j][:, 0:HEAD_DIM]
        vsel_of = lambda j: ns_refs[j][:, HEAD_DIM:2 * HEAD_DIM]
        qk, pv = _dot_nt, _dot

    causal, v_idx, blk = [], [], []
    for j in range(G):
        delta = q0 - (t0 + j) * PAGE
        causal.append(delta + qi - kj >= 0)
        v_idx.append(jnp.clip(lax.shift_right_arithmetic(delta, 7), 0, 2))
        blk.append(lax.shift_right_logical(t0 + j, 1))

    moba_sel = [[jnp.sum(jnp.where(lane_nb == blk[j], selm[h], 0.0), axis=1, keepdims=True) > 0.0
                 for h in range(MOBA_HEADS)] for j in range(G)]
    if cache_tiles:
        assert G % 2 == 0
        kv_all = lambda j, w: pick(j, *[lambda kv, ns: kv[w].reshape(MOBA_HEADS * HEAD_DIM, PAGE)] * 2)
        q_bd = qbd[...]
        s_parts, m_parts = [], []
        for j in range(0, G, 2):
            s_parts.append(_dot(q_bd, jnp.concatenate([kv_all(j, 0), kv_all(j + 1, 0)], axis=1)))
        for j in range(G):
            valid_j = jnp.concatenate([causal[j] & (moba_sel[j][h] | (blk[j] == ob)) for h in range(MOBA_HEADS)], axis=0)
            m_parts.append(valid_j)
        bias = jnp.concatenate([tbm_ref[v_idx[j]] for j in range(G)], axis=1)

        def pv_heads(p):
            acc = None
            for j in range(0, G, 2):
                o2 = _dot_nt(p[:, PAGE * j:PAGE * (j + 2)],
                             jnp.concatenate([kv_all(j, 1), kv_all(j + 1, 1)], axis=1))
                diag = jnp.concatenate([o2[h * Q:(h + 1) * Q, HEAD_DIM * h:HEAD_DIM * (h + 1)]
                                        for h in range(MOBA_HEADS)], axis=0)
                acc = diag if acc is None else acc + diag
            return acc

        _softmax_update(m_m, l_m, a_m, 0, jnp.concatenate(s_parts, axis=1) + bias, jnp.concatenate(m_parts, axis=1),
                        pv_heads)
    else:
        q_all = qms[...]
        for h in range(MOBA_HEADS):
            sl = slice(HEAD_DIM * h, HEAD_DIM * (h + 1))
            s_parts, m_parts = [], []
            for j in range(G):
                s_parts.append(qk(q_all[:, sl], k_of(j, h)) + tbm_ref[v_idx[j], h * Q:(h + 1) * Q, :])
                m_parts.append(causal[j] & (moba_sel[j][h] | (blk[j] == ob)))
            _softmax_update(m_m, l_m, a_m, h, jnp.concatenate(s_parts, axis=1), jnp.concatenate(m_parts, axis=1),
                            _pv_slots([v_of(j, h) for j in range(G)], pv))

    qn4 = qns[...]
    sel_all = sels[...]
    s_parts, m_parts = [], []
    for j in range(G):
        s_parts.append(qk(qn4, ksel_of(j)) + tbn_ref[v_idx[j]])
        lo = jnp.sum(jnp.where(lane_bs == 2 * (t0 + j), sel_all, 0.0), axis=1, keepdims=True)
        hi = jnp.sum(jnp.where(lane_bs == 2 * (t0 + j) + 1, sel_all, 0.0), axis=1, keepdims=True)
        m_parts.append(causal[j] & (jnp.where(kj < NSA_SEL_BLOCK, lo, hi) > 0.0))
    _softmax_update(m_n, l_n, a_n, 0, jnp.concatenate(s_parts, axis=1), rep4(jnp.concatenate(m_parts, axis=1)),
                    _pv_slots([vsel_of(j) for j in range(G)], pv))

    @pl.when(sl_ref[s_id] == 1)
    def _finish():
        outs = []
        for h in range(MOBA_HEADS):
            if cache_tiles:
                l = l_m[0, h * Q:(h + 1) * Q, :]
                a = a_m[0, h * Q:(h + 1) * Q, :]
            else:
                l, a = l_m[h], a_m[h]
            outs.append(a / jnp.where(l > 0, l, 1.0))
        om_ref[...] = jnp.concatenate(outs, axis=1)
        l = l_n[0]
        o_s = a_n[0] / jnp.where(l > 0, l, 1.0)
        o_w = ow[...]
        o_c = oc[...]
        gn = gn_ref[...]
        outs = []
        for h in range(NSA_HEADS):
            r = slice(h * Q, (h + 1) * Q)
            outs.append(gn[:, 3 * h:3 * h + 1] * o_c[r] + gn[:, 3 * h + 1:3 * h + 2] * o_s[r]
                        + gn[:, 3 * h + 2:3 * h + 3] * o_w[r])
        on_ref[...] = jnp.concatenate(outs, axis=1)


def bias_tiles(rel_bias, q_rows):
    Q = q_rows
    qi = np.arange(Q)[:, None]
    kj = np.arange(PAGE)[None, :]
    d = jnp.asarray(np.stack([v * PAGE + qi - kj for v in range(3)]))[..., None]
    thr = _bucket_thresholds()
    tb = jnp.broadcast_to(rel_bias[N_BUCKETS - 1], d.shape[:-1] + (rel_bias.shape[1],))
    for k in range(N_BUCKETS - 2, -1, -1):
        tb = jnp.where(d < thr[k + 1], rel_bias[k], tb)
    stack = lambda t: t.transpose(0, 3, 1, 2).reshape(3, t.shape[-1] * Q, PAGE)
    return stack(tb[..., :MOBA_HEADS]), stack(tb[..., MOBA_HEADS:])


def attention_slots(rel_bias, qm, qn, gn, q_blk0, kv_src, ns_src, page_idx, tail_kv, tail_ns, win_pages, win_w0t,
                    kmean, cmp, tiles, *, n_seq, q_rows, n_chunks, q0_base, tiles_per_seq, n_win_tiles,
                    n_sel_blocks, n_slots, cache_layer=None):
    Q, G = q_rows, n_slots
    cache_tiles = cache_layer is not None
    assert PAGE % Q == 0 and q0_base % PAGE == 0 and (n_chunks == 1 or Q == PAGE)
    tbm, tbn = tiles
    nb = kmean.shape[1]
    nc = cmp.shape[1]
    nbs_p = -(-n_sel_blocks // 128) * 128
    per = NSA_SEL_BLOCK // NSA_CMP_STRIDE
    pool = jnp.asarray((np.arange(nc)[:, None] // per == np.arange(nbs_p)[None, :]).astype(np.float32))
    has_tail = tail_kv is not None

    steps = []
    for i in range(n_chunks):
        n_tiles = (q0_base + i * Q + Q - 1) // PAGE + 1
        n_st = -(-n_tiles // G)
        for s in range(n_st):
            steps.append((i, s * G, int(s == n_st - 1)))
    steps = np.asarray(steps, np.int32)
    n_steps = steps.shape[0]
    tail_tile = tiles_per_seq if has_tail else None
    assert tail_tile is None or tail_tile % G == 0
    if cache_tiles:
        assert has_tail
        kv_blk = (None, None, 2, MOBA_HEADS, HEAD_DIM, PAGE)
        ns_blk = (None, None, 2, HEAD_DIM, PAGE)
        tail_kv_blk, tail_ns_blk = kv_blk[1:], ns_blk[1:]
    else:
        assert not has_tail
        kv_blk = tail_kv_blk = (None, PAGE, 512)
        ns_blk = tail_ns_blk = (None, PAGE, 128)
        tail_kv = jnp.zeros((1, PAGE, 512), F32)
        tail_ns = jnp.zeros((1, PAGE, 256), F32)

    def q_map(b, s, si, st, sl, pg):
        return (q_blk0 + b * n_chunks + si[s], 0)

    def o_map(b, s, si, st, sl, pg):
        return (b * n_chunks + si[s], 0)

    def page_of(b, s, st, pg, j):
        return pg[b * tiles_per_seq + jnp.minimum(st[s] + j, tiles_per_seq - 1)]

    def kv_map(j):
        if cache_tiles:
            return lambda b, s, si, st, sl, pg: (cache_layer, page_of(b, s, st, pg, j), 0, 0, 0, 0)
        return lambda b, s, si, st, sl, pg: (page_of(b, s, st, pg, j), 0, 0)

    def ns_map(j):
        if cache_tiles:
            return lambda b, s, si, st, sl, pg: (cache_layer, page_of(b, s, st, pg, j), 1, 0, 0)
        return lambda b, s, si, st, sl, pg: (page_of(b, s, st, pg, j), 0, 1)

    def tail_map(nd):
        if cache_tiles:
            return lambda b, s, si, st, sl, pg: (b,) + (0,) * (nd - 1)
        return lambda b, s, si, st, sl, pg: (0, 0, 0) if nd == 0 else (0, 0, 1)

    def win_map(w):
        def f(b, s, si, st, sl, pg):
            tw = lax.shift_right_logical(q0_base + si[s] * Q, 7) - (WIN_TILES - 1) + w - win_w0t
            return (b * n_win_tiles + jnp.clip(tw, 0, n_win_tiles - 1), 0, 0)
        return f

    const = lambda nd: (lambda b, s, si, st, sl, pg: (0,) * nd)
    moba_state = (1, MOBA_HEADS * Q) if cache_tiles else (MOBA_HEADS, Q)
    in_specs = ([pl.BlockSpec(memory_space=pltpu.SMEM),
                 pl.BlockSpec((Q, 256), q_map), pl.BlockSpec((Q, 256), q_map), pl.BlockSpec((Q, 128), q_map)]
                + [pl.BlockSpec(kv_blk, kv_map(j)) for j in range(G)]
                + [pl.BlockSpec(ns_blk, ns_map(j)) for j in range(G)]
                + [pl.BlockSpec(tail_kv_blk, tail_map(5 if cache_tiles else 0)),
                   pl.BlockSpec(tail_ns_blk, tail_map(4 if cache_tiles else 1))]
                + [pl.BlockSpec((None, PAGE, 128), win_map(w)) for w in range(WIN_TILES)]
                + [pl.BlockSpec((None, nb, 256), lambda b, s, si, st, sl, pg: (b, 0, 0)),
                   pl.BlockSpec((None, nc, 128), lambda b, s, si, st, sl, pg: (b, 0, 0)),
                   pl.BlockSpec((3, MOBA_HEADS * Q, PAGE), const(3)),
                   pl.BlockSpec((3, NSA_HEADS * Q, PAGE), const(3)),
                   pl.BlockSpec((nc, nbs_p), const(2))])
    gs = pltpu.PrefetchScalarGridSpec(
        num_scalar_prefetch=4,
        grid=(n_seq, n_steps),
        in_specs=in_specs,
        out_specs=[pl.BlockSpec((Q, 256), o_map), pl.BlockSpec((Q, 256), o_map)],
        scratch_shapes=[
            pltpu.VMEM((Q, 256), F32),
            pltpu.VMEM((NSA_HEADS * Q, HEAD_DIM), F32),
            pltpu.VMEM((MOBA_HEADS * Q, 256), F32),
            pltpu.VMEM((MOBA_HEADS, Q, nb), F32),
            pltpu.VMEM((Q, nbs_p), F32),
            pltpu.VMEM(moba_state + (1,), F32),
            pltpu.VMEM(moba_state + (1,), F32),
            pltpu.VMEM(moba_state + (HEAD_DIM,), F32),
            pltpu.VMEM((1, NSA_HEADS * Q, 1), F32),
            pltpu.VMEM((1, NSA_HEADS * Q, 1), F32),
            pltpu.VMEM((1, NSA_HEADS * Q, HEAD_DIM), F32),
            pltpu.VMEM((NSA_HEADS * Q, HEAD_DIM), F32),
            pltpu.VMEM((NSA_HEADS * Q, HEAD_DIM), F32),
        ],
    )
    n_tok = n_seq * n_chunks * Q
    return pl.pallas_call(
        functools.partial(_attn_slots_kernel, q_rows=Q, q0_base=q0_base, n_slots=G, cache_tiles=cache_tiles,
                          tail_tile=tail_tile),
        grid_spec=gs,
        out_shape=[jax.ShapeDtypeStruct((n_tok, 256), F32), jax.ShapeDtypeStruct((n_tok, 256), F32)],
        compiler_params=_cparams(("parallel", "arbitrary")),
        name="attention",
    )(jnp.asarray(steps[:, 0]), jnp.asarray(steps[:, 1]), jnp.asarray(steps[:, 2]), page_idx, rel_bias,
      qm, qn, gn, *([kv_src] * G), *([ns_src] * G), tail_kv, tail_ns, *([win_pages] * WIN_TILES),
      kmean, cmp, tbm, tbn, pool)


def _softplus(x):
    return jnp.maximum(x, 0.0) + jnp.log(1.0 + jnp.exp(-jnp.abs(x)))


def _ssd_kernel(xbc_ref, z_ref, dt_ref, dtt_ref, cprev_ref, st0_ref, cw_ref, cb_ref, dtb_ref, dtbt_ref,
                al_ref, alt_ref, dd_ref, nw_ref, y_ref, st_ref, xp, *, valid_len):
    c = pl.program_id(1)
    cl = xbc_ref.shape[0]
    gn = SSD_GROUPS * SSD_STATE

    @pl.when(c == 0)
    def _first():
        xp[0:8, :] = cprev_ref[...]
        st_ref[...] = st0_ref[...]

    xp[8:8 + cl, :] = xbc_ref[...]
    conv = cb_ref[...] + cw_ref[0:1, :] * xp[5:5 + cl, :]
    for k in range(1, SSD_CONV):
        conv = conv + cw_ref[k:k + 1, :] * xp[5 + k:5 + k + cl, :]
    xp[0:8, :] = xp[cl:cl + 8, :]
    act = _silu(conv)
    xs = act[:, :SSD_INNER]
    bm = act[:, SSD_INNER:SSD_INNER + gn]
    cm = act[:, SSD_INNER + gn:]

    row = lax.broadcasted_iota(jnp.int32, (cl, cl), 0)
    col = lax.broadcasted_iota(jnp.int32, (cl, cl), 1)
    tri = row >= col
    pos_r = c * cl + lax.broadcasted_iota(jnp.int32, (cl, 1), 0)
    pos_c = c * cl + lax.broadcasted_iota(jnp.int32, (1, cl), 1)
    dt = jnp.where(pos_r < valid_len, _softplus(dt_ref[...] + dtb_ref[...]), 0.0)
    dtt = jnp.where(pos_c < valid_len, _softplus(dtt_ref[...] + dtbt_ref[...]), 0.0)
    da = dt * (-jnp.exp(al_ref[...]))
    dat = dtt * (-jnp.exp(alt_ref[...]))
    acum = jnp.dot(jnp.where(tri, 1.0, 0.0), da, precision=HI, preferred_element_type=F32)
    acumt = jnp.dot(dat, jnp.where(row <= col, 1.0, 0.0), precision=HI, preferred_element_type=F32)

    cbs = [_dot_nt(cm[:, SSD_STATE * g:SSD_STATE * (g + 1)], bm[:, SSD_STATE * g:SSD_STATE * (g + 1)])
           for g in range(SSD_GROUPS)]
    ys = []
    for h in range(SSD_HEADS):
        g = h // (SSD_HEADS // SSD_GROUPS)
        bg = bm[:, SSD_STATE * g:SSD_STATE * (g + 1)]
        cg = cm[:, SSD_STATE * g:SSD_STATE * (g + 1)]
        a_col = acum[:, h:h + 1]
        a_row = acumt[h:h + 1, :]
        a_last = acumt[h:h + 1, cl - 1:cl]
        lmat = jnp.exp(jnp.where(tri, a_col - a_row, -jnp.inf))
        xh = xs[:, SSD_HEAD_DIM * h:SSD_HEAD_DIM * (h + 1)]
        xdt = xh * dt[:, h:h + 1]
        y_diag = _dot(cbs[g] * lmat, xdt)
        prev = st_ref[h]
        y_off = jnp.exp(a_col) * _dot_nt(cg, prev)
        decay = jnp.exp(a_last - a_col)
        upd = lax.dot_general(xdt.astype(BF16), (bg * decay).astype(BF16), (((0,), (0,)), ((), ())),
                              preferred_element_type=F32)
        st_ref[h] = prev * jnp.exp(a_last) + upd
        ys.append(y_diag + y_off + dd_ref[:, h:h + 1] * xh)
    y = jnp.concatenate(ys, axis=1) * _silu(z_ref[...])
    half = SSD_INNER // SSD_GROUPS
    outs = []
    for g in range(SSD_GROUPS):
        yg = y[:, half * g:half * (g + 1)]
        outs.append(yg * lax.rsqrt(jnp.mean(yg * yg, axis=-1, keepdims=True) + EPS))
    y_ref[...] = jnp.concatenate(outs, axis=1) * nw_ref[...]


def ssd(xbc, z, dt, dt_t, conv_prev8, state0, cw, cb, dtb, alog, dd, nw, *, n_seq, seq_rows, chunk, valid_len):
    n_ch = seq_rows // chunk

    def pad128(v):
        return jnp.pad(v.reshape(1, -1), ((0, 0), (0, 128 - v.shape[-1])))

    tok = lambda w: pl.BlockSpec((chunk, w), lambda b, c: (b * n_ch + c, 0))
    full = lambda shp: pl.BlockSpec(shp, lambda b, c: (0,) * len(shp))
    return pl.pallas_call(
        functools.partial(_ssd_kernel, valid_len=valid_len),
        grid=(n_seq, n_ch),
        in_specs=[tok(SSD_CONV_DIM), tok(SSD_INNER), tok(128),
                  pl.BlockSpec((None, SSD_HEADS, chunk), lambda b, c: (b, 0, c)),
                  pl.BlockSpec((None, 8, SSD_CONV_DIM), lambda b, c: (b, 0, 0)),
                  pl.BlockSpec((None, SSD_HEADS, SSD_HEAD_DIM, SSD_STATE), lambda b, c: (b, 0, 0, 0)),
                  full((SSD_CONV, SSD_CONV_DIM)), full((1, SSD_CONV_DIM)), full((1, 128)), full((SSD_HEADS, 1)),
                  full((1, 128)), full((SSD_HEADS, 1)), full((1, 128)), full((1, SSD_INNER))],
        out_specs=[tok(SSD_INNER),
                   pl.BlockSpec((None, SSD_HEADS, SSD_HEAD_DIM, SSD_STATE), lambda b, c: (b, 0, 0, 0))],
        out_shape=[jax.ShapeDtypeStruct((n_seq * seq_rows, SSD_INNER), F32),
                   jax.ShapeDtypeStruct((n_seq, SSD_HEADS, SSD_HEAD_DIM, SSD_STATE), F32)],
        scratch_shapes=[pltpu.VMEM((chunk + 8, SSD_CONV_DIM), F32)],
        compiler_params=_cparams(("parallel", "arbitrary")),
        name="ssd",
    )(xbc, z, dt, dt_t, conv_prev8, state0, cw, cb.reshape(1, -1), pad128(dtb), dtb.reshape(-1, 1),
      pad128(alog), alog.reshape(-1, 1), pad128(dd), nw.reshape(1, -1))


def _mix_kernel(om_ref, on_ref, ys_ref, gbr_ref, x_ref, g1p_ref, g1s_ref, sc2p_ref, sc2s_ref, sh2p_ref, sh2s_ref,
                nw1_ref, nw2_ref, wbm_ref, wbn_ref, wbs_ref, wo_ref, rwt_ref, x1_ref, h2_ref, h2t_ref, lg_ref,
                *, n_prompt_tiles):
    d = D_MODEL
    g1 = _mod(g1p_ref, g1s_ref, n_prompt_tiles)
    sc2 = _mod(sc2p_ref, sc2s_ref, n_prompt_tiles)
    sh2 = _mod(sh2p_ref, sh2s_ref, n_prompt_tiles)
    ya = _dot(om_ref[...], wbm_ref[...])
    yb = _dot(on_ref[...], wbn_ref[...])
    yc = _dot(ys_ref[...], wbs_ref[...])
    merged = gbr_ref[:, 0:d] * ya + gbr_ref[:, d:2 * d] * yb + gbr_ref[:, 2 * d:3 * d] * yc
    m = _dot(merged, wo_ref[...])
    x1 = x_ref[...] + g1 * _rms(m, nw1_ref[...])
    h2 = _rms(x1, nw2_ref[...]) * (1.0 + sc2) + sh2
    x1_ref[...] = x1
    h2_ref[...] = h2
    for s in range(ROW_TILES):
        h2t_ref[pl.ds(s, TM, stride=ROW_TILES), :] = h2[:, 128 * s:128 * (s + 1)]
    lg_ref[...] = _dot_nt(rwt_ref[...], h2, precision=HI)


def mix_out(om, on, ys, gbr, x, mods, nw1, nw2, wbm, wbn, wbs, wo, rwt):
    n, d = x.shape
    tok = lambda w: pl.BlockSpec((TM, w), lambda i: (i, 0))
    full = lambda a: pl.BlockSpec(a.shape, lambda i: (0,) * a.ndim, pipeline_mode=pl.Buffered(1))
    mp, ms = mods.prompt, mods.sample
    return pl.pallas_call(
        functools.partial(_mix_kernel, n_prompt_tiles=mods.n_prompt_tiles),
        grid=(n // TM,),
        in_specs=[tok(256), tok(256), tok(SSD_INNER), tok(3 * d), tok(d),
                  *_mod_specs(mods, 2), *_mod_specs(mods, 4), *_mod_specs(mods, 3),
                  full(nw1), full(nw2), full(wbm), full(wbn), full(wbs), full(wo), full(rwt)],
        out_specs=[tok(d), tok(d), pl.BlockSpec((TM * ROW_TILES, 128), lambda i: (i, 0)),
                   pl.BlockSpec((N_EXPERTS, TM), lambda i: (0, i))],
        out_shape=[jax.ShapeDtypeStruct((n, d), F32), jax.ShapeDtypeStruct((n, d), F32),
                   jax.ShapeDtypeStruct((n * ROW_TILES, 128), F32), jax.ShapeDtypeStruct((N_EXPERTS, n), F32)],
        compiler_params=_cparams(("parallel",)),
        name="mix_out",
    )(om, on, ys, gbr, x, mp, ms, mp, ms, mp, ms, nw1, nw2, wbm, wbn, wbs, wo, rwt)


def _router_kernel(lg_ref, rb_ref, eidx_ref, w8_ref, pos_ref, cnt_ref, carry):
    i = pl.program_id(0)
    tm = lg_ref.shape[1]
    per = N_EXPERTS // N_ROUTE_GROUPS

    @pl.when(i == 0)
    def _zero():
        carry[...] = jnp.zeros(carry.shape, F32)

    s = _sigmoid(lg_ref[...])
    sc = s + rb_ref[...]
    sub = lax.broadcasted_iota(jnp.int32, (per, tm), 0).astype(F32)
    gs_rows = []
    for g in range(N_ROUTE_GROUPS):
        x = sc[per * g:per * (g + 1), :]
        m1 = jnp.max(x, axis=0, keepdims=True)
        i1 = jnp.min(jnp.where(x == m1, sub, float(per)), axis=0, keepdims=True)
        m2 = jnp.max(jnp.where(sub == i1, -jnp.inf, x), axis=0, keepdims=True)
        gs_rows.append(m1 + m2)
    gs = jnp.concatenate(gs_rows, axis=0)
    gsel = jnp.zeros_like(gs)
    for _ in range(TOPK_GROUPS):
        m = jnp.max(gs, axis=0, keepdims=True)
        ix = jnp.min(jnp.where(gs == m, sub, float(N_ROUTE_GROUPS)), axis=0, keepdims=True)
        hit = sub == ix
        gsel = jnp.where(hit, 1.0, gsel)
        gs = jnp.where(hit, -jnp.inf, gs)
    emask = jnp.concatenate([jnp.broadcast_to(gsel[g:g + 1, :], (per, tm)) for g in range(N_ROUTE_GROUPS)], axis=0)
    msc = jnp.where(emask > 0.0, sc, -jnp.inf)
    e_io = lax.broadcasted_iota(jnp.int32, (N_EXPERTS, tm), 0).astype(F32)
    sel = jnp.zeros_like(sc)
    idxs = []
    for _ in range(TOP_K):
        m = jnp.max(msc, axis=0, keepdims=True)
        ix = jnp.min(jnp.where(msc == m, e_io, float(N_EXPERTS)), axis=0, keepdims=True)
        hit = e_io == ix
        sel = jnp.where(hit, 1.0, sel)
        msc = jnp.where(hit, -jnp.inf, msc)
        idxs.append(ix)
    w = s * sel
    wn = w / jnp.sum(w, axis=0, keepdims=True) * ROUTE_SCALE
    r = lax.broadcasted_iota(jnp.int32, (tm, tm), 0)
    cidx = lax.broadcasted_iota(jnp.int32, (tm, tm), 1)
    upper = jnp.where(r <= cidx, 1.0, 0.0)
    cum = _dot(sel, upper)
    rank = cum - sel + carry[:, 0:1]
    eidx_rows, w_rows, p_rows = [], [], []
    for k in range(TOP_K):
        hit = e_io == idxs[k]
        eidx_rows.append(idxs[k])
        w_rows.append(jnp.sum(jnp.where(hit, wn, 0.0), axis=0, keepdims=True))
        p_rows.append(jnp.sum(jnp.where(hit, rank, 0.0), axis=0, keepdims=True))
    eidx_ref[...] = jnp.concatenate(eidx_rows, axis=0).astype(jnp.int32)
    w8_ref[...] = jnp.concatenate(w_rows, axis=0)
    pos_ref[...] = jnp.concatenate(p_rows, axis=0).astype(jnp.int32)
    carry[...] = carry[...] + jnp.sum(sel, axis=1, keepdims=True)
    cnt_ref[...] = carry[...]


def router(logits_t, router_b):
    ne, n = logits_t.shape
    return pl.pallas_call(
        _router_kernel,
        grid=(n // TM,),
        in_specs=[pl.BlockSpec((ne, TM), lambda i: (0, i)), pl.BlockSpec((ne, 1), lambda i: (0, 0))],
        out_specs=[pl.BlockSpec((TOP_K, TM), lambda i: (0, i)), pl.BlockSpec((TOP_K, TM), lambda i: (0, i)),
                   pl.BlockSpec((TOP_K, TM), lambda i: (0, i)), pl.BlockSpec((ne, 128), lambda i: (0, 0))],
        out_shape=[jax.ShapeDtypeStruct((TOP_K, n), jnp.int32), jax.ShapeDtypeStruct((TOP_K, n), F32),
                   jax.ShapeDtypeStruct((TOP_K, n), jnp.int32), jax.ShapeDtypeStruct((ne, 128), F32)],
        scratch_shapes=[pltpu.VMEM((ne, 128), F32)],
        compiler_params=_cparams(("arbitrary",)),
        name="router",
    )(logits_t, router_b.reshape(ne, 1))


def _dispatch_kernel(dest_ref, h_ref, init_ref, rows_ref, sem):
    del init_ref
    n_pairs = dest_ref.shape[1]

    def copy(j):
        src = pl.multiple_of(lax.shift_right_logical(j, 3) * ROW_TILES, ROW_TILES)
        dst = pl.multiple_of(dest_ref[0, j] * ROW_TILES, ROW_TILES)
        return pltpu.make_async_copy(h_ref.at[pl.ds(src, ROW_TILES)], rows_ref.at[pl.ds(dst, ROW_TILES)], sem)

    def start(jj, carry):
        for u in range(TOP_K):
            copy(jj * TOP_K + u).start(priority=u % 2)
        return carry

    def wait(j, carry):
        copy(j).wait()
        return carry

    lax.fori_loop(0, n_pairs // TOP_K, start, 0)
    lax.fori_loop(0, n_pairs, wait, 0, unroll=8)


def dispatch(h2t, dest, n_rows):
    n = h2t.shape[0] // ROW_TILES
    n_tiles = n // TM
    rows0 = jnp.zeros((n_rows * ROW_TILES, 128), F32)
    return pl.pallas_call(
        _dispatch_kernel,
        grid=(n_tiles,),
        in_specs=[pl.BlockSpec((None, 1, TM * TOP_K), lambda i: (i, 0, 0), memory_space=pltpu.SMEM),
                  pl.BlockSpec((TM * ROW_TILES, 128), lambda i: (i, 0)),
                  pl.BlockSpec(memory_space=pl.ANY)],
        out_specs=pl.BlockSpec(memory_space=pl.ANY),
        out_shape=jax.ShapeDtypeStruct((n_rows * ROW_TILES, 128), F32),
        scratch_shapes=[pltpu.SemaphoreType.DMA(())],
        input_output_aliases={2: 0},
        compiler_params=_cparams(("arbitrary",)),
        name="dispatch",
    )(dest.reshape(n_tiles, 1, TM * TOP_K), h2t, rows0)


def _expert_kernel(be_ref, nu_ref, x_ref, wg_ref, wu_ref, wd_ref, y_ref):
    @pl.when(pl.program_id(0) < nu_ref[0])
    def _():
        de = wg_ref.shape[1]
        g = jnp.zeros((EXPERT_ROWS, de), F32)
        u = jnp.zeros((EXPERT_ROWS, de), F32)
        for s in range(0, ROW_TILES, 2):
            xs = jnp.concatenate([x_ref[pl.ds(s, EXPERT_ROWS, stride=ROW_TILES), :],
                                  x_ref[pl.ds(s + 1, EXPERT_ROWS, stride=ROW_TILES), :]], axis=1).astype(BF16)
            g = g + jnp.dot(xs, wg_ref[128 * s:128 * (s + 2), :].astype(BF16), preferred_element_type=F32)
            u = u + jnp.dot(xs, wu_ref[128 * s:128 * (s + 2), :].astype(BF16), preferred_element_type=F32)
        y = _dot(_silu(g) * u, wd_ref[...])
        for s in range(ROW_TILES):
            y_ref[pl.ds(s, EXPERT_ROWS, stride=ROW_TILES), :] = y[:, 128 * s:128 * (s + 1)]

    @pl.when(pl.program_id(0) >= nu_ref[0])
    def _():
        y_ref[...] = jnp.zeros(y_ref.shape, F32)


def experts(x_rows, blk_e, n_used, wg, wu, wd, layer):
    n_rows = x_rows.shape[0] // ROW_TILES
    n_blk = n_rows // EXPERT_ROWS
    ne, d, de = wg.shape[1:]
    wg2 = wg.reshape(-1, d, de)
    wu2 = wu.reshape(-1, d, de)
    wd2 = wd.reshape(-1, de, d)
    gs = pltpu.PrefetchScalarGridSpec(
        num_scalar_prefetch=2,
        grid=(n_blk,),
        in_specs=[pl.BlockSpec((EXPERT_ROWS * ROW_TILES, 128), lambda i, be, nu: (i, 0)),
                  pl.BlockSpec((None, d, de), lambda i, be, nu: (layer * ne + be[i], 0, 0)),
                  pl.BlockSpec((None, d, de), lambda i, be, nu: (layer * ne + be[i], 0, 0)),
                  pl.BlockSpec((None, de, d), lambda i, be, nu: (layer * ne + be[i], 0, 0))],
        out_specs=pl.BlockSpec((EXPERT_ROWS * ROW_TILES, 128), lambda i, be, nu: (i, 0)),
    )
    return pl.pallas_call(
        _expert_kernel,
        grid_spec=gs,
        out_shape=jax.ShapeDtypeStruct((n_rows * ROW_TILES, 128), F32),
        compiler_params=_cparams(("arbitrary",)),
        name="experts",
    )(blk_e, n_used, x_rows, wg2, wu2, wd2)


def _combine_kernel(dest_ref, w8_ref, x1_ref, h2_ref, g2p_ref, g2s_ref, nw_ref, wsg_ref, wsu_ref, wsd_ref, yrows_ref,
                    out_ref, buf, sem, *, n_prompt_tiles):
    n_pairs = dest_ref.shape[1]

    def copy(j):
        src = pl.multiple_of(dest_ref[0, j] * ROW_TILES, ROW_TILES)
        dst = pl.multiple_of(lax.shift_right_logical(j, 3) * ROW_TILES, ROW_TILES)
        k = jnp.bitwise_and(j, TOP_K - 1)
        return pltpu.make_async_copy(yrows_ref.at[pl.ds(src, ROW_TILES)], buf.at[k, pl.ds(dst, ROW_TILES)], sem)

    def start(jj, carry):
        for u in range(TOP_K):
            copy(jj * TOP_K + u).start(priority=u % 2)
        return carry

    def wait(j, carry):
        copy(j).wait()
        return carry

    lax.fori_loop(0, n_pairs // TOP_K, start, 0)
    h2 = h2_ref[...]
    shared = _dot(_silu(_dot(h2, wsg_ref[...])) * _dot(h2, wsu_ref[...]), wsd_ref[...])
    lax.fori_loop(0, n_pairs, wait, 0, unroll=8)
    w8 = w8_ref[...]
    pieces = []
    for s in range(ROW_TILES):
        acc = w8[:, 0:1] * buf[0, pl.ds(s, TM, stride=ROW_TILES), :]
        for k in range(1, TOP_K):
            acc = acc + w8[:, k:k + 1] * buf[k, pl.ds(s, TM, stride=ROW_TILES), :]
        pieces.append(acc)
    routed = jnp.concatenate(pieces, axis=1)
    g2 = _mod(g2p_ref, g2s_ref, n_prompt_tiles)
    out_ref[...] = x1_ref[...] + g2 * _rms(routed + shared, nw_ref[...])


def combine(dest, w8, x1, h2, mods, nw3, wsg, wsu, wsd, y_rows):
    n, d = x1.shape
    n_tiles = n // TM
    tok = lambda w: pl.BlockSpec((TM, w), lambda i: (i, 0))
    full = lambda a: pl.BlockSpec(a.shape, lambda i: (0,) * a.ndim)
    return pl.pallas_call(
        functools.partial(_combine_kernel, n_prompt_tiles=mods.n_prompt_tiles),
        grid=(n_tiles,),
        in_specs=[pl.BlockSpec((None, 1, TM * TOP_K), lambda i: (i, 0, 0), memory_space=pltpu.SMEM),
                  tok(TOP_K), tok(d), tok(d), *_mod_specs(mods, 5),
                  full(nw3), full(wsg), full(wsu), full(wsd),
                  pl.BlockSpec(memory_space=pl.ANY)],
        out_specs=tok(d),
        out_shape=jax.ShapeDtypeStruct((n, d), F32),
        scratch_shapes=[pltpu.VMEM((TOP_K, TM * ROW_TILES, 128), F32), pltpu.SemaphoreType.DMA(())],
        compiler_params=_cparams(("arbitrary",)),
        name="combine",
    )(dest.reshape(n_tiles, 1, TM * TOP_K), w8, x1, h2, mods.prompt, mods.sample, nw3, wsg, wsu, wsd, y_rows)


def _pack_w_in(w):
    d = w.shape[0]
    z = lambda n: jnp.zeros((d, n), w.dtype)
    parts = [w[:, 0:1408], w[:, 1408:1420], z(116), w[:, 1420:1932], w[:, 1932:2956], w[:, 2956:2964], z(120),
             w[:, 2964:6036]]
    return jnp.concatenate(parts, axis=1).astype(BF16)


def _expand_w1(w1):
    half = NSA_CMP_STRIDE * HEAD_DIM
    w = jnp.zeros((NSA_CMP_STRIDE, 4, HEAD_DIM, 256), w1.dtype)
    for kv in range(2):
        w = w.at[:, kv, :, 128 * kv:128 * kv + 64].set(w1[kv, :half].reshape(NSA_CMP_STRIDE, HEAD_DIM, HEAD_DIM))
        w = w.at[:, kv, :, 128 * kv + 64:128 * kv + 128].set(w1[kv, half:].reshape(NSA_CMP_STRIDE, HEAD_DIM, HEAD_DIM))
    return w.reshape(NSA_CMP_STRIDE * 256, 256).astype(BF16)


def _cmp_w1_rows(w1):
    half = NSA_CMP_STRIDE * HEAD_DIM
    w = jnp.zeros((NSA_CMP_STRIDE, 2, HEAD_DIM, 256), w1.dtype)
    for kv in range(2):
        w = w.at[:, kv, :, 128 * kv:128 * kv + 64].set(w1[kv, :half].reshape(NSA_CMP_STRIDE, HEAD_DIM, HEAD_DIM))
        w = w.at[:, kv, :, 128 * kv + 64:128 * kv + 128].set(w1[kv, half:].reshape(NSA_CMP_STRIDE, HEAD_DIM, HEAD_DIM))
    return w.reshape(NSA_CMP_STRIDE * 128, 256).astype(BF16)


PROMPT_SLOTS = 8
SAMPLE_SLOTS = 16


def kernel(x_prompt, x_sample, c_prompt, c_sample, cache_moba_kv, cache_nsa_kv, cache_nsa_win, state_ssd_conv, state_ssd, page_table, rel_bias, ada_w, ada_b, norm_w, w_in, nsa_cmp_w1, nsa_cmp_b1, nsa_cmp_w2, nsa_cmp_b2, nsa_cmp_pos, ssd_conv_w, ssd_conv_b, ssd_dt_bias, ssd_a_log, ssd_d, ssd_norm_w, w_branch_moba, w_branch_nsa, w_branch_ssd, w_out, router_w, router_b, exp_w_gate, exp_w_up, exp_w_down, shared_w_gate, shared_w_up, shared_w_down):
    depth = w_in.shape[0]
    bp, lp, d = x_prompt.shape
    bs, ls, _ = x_sample.shape
    n_p = bp * lp
    n_s = bs * ls
    n = n_p + n_s
    assert n_s == TM and lp % TM == 0 and lp % SSD_CHUNK == 0
    n_pages = page_table.shape[1]
    past = n_pages * PAGE
    n_pool = cache_moba_kv.shape[1]
    tiles_p = lp // TM
    n_ptiles = n_p // TM

    x = jnp.concatenate([x_prompt.reshape(n_p, d), x_sample.reshape(n_s, d)], axis=0)
    c_all = jnp.concatenate([c_prompt, c_sample, jnp.zeros((4, d), F32)], axis=0)
    kv_cache = jnp.transpose(cache_moba_kv, (0, 1, 3, 4, 5, 2))
    ns_cache = jnp.transpose(cache_nsa_kv, (0, 1, 3, 4, 2))
    pt_flat = page_table.reshape(-1).astype(jnp.int32)
    prompt_pages = jnp.arange(bp * (lp // PAGE), dtype=jnp.int32)

    q_chunk = 128
    tiles_p_bias = bias_tiles(rel_bias, q_chunk)
    tiles_s_bias = bias_tiles(rel_bias, ls)

    n_blk = -(-(n * TOP_K + N_EXPERTS * (EXPERT_ROWS - 1)) // EXPERT_ROWS)
    n_rows = n_blk * EXPERT_ROWS

    sample_pad = 128
    outs = [[] for _ in range(10)]
    for l in range(depth):
        mod = modulation(c_all, ada_w, ada_b, l)
        mods = Mods(jnp.broadcast_to(mod[:bp, None, :], (bp, 8, 6 * d)), jnp.repeat(mod[bp:bp + bs], ls, axis=0),
                    n_ptiles, tiles_p)
        nw = norm_w[l]
        qm, kvm, qn, nsa, win, gn, z, xbc, dt, gbr = proj_in(x, mods, nw[0:1], _pack_w_in(w_in[l]))
        w_exp = _expand_w1(nsa_cmp_w1[l])

        kvm_pages = kvm.reshape(n // PAGE, PAGE, 512)
        nsa_pages = nsa.reshape(n // PAGE, PAGE, 256)
        win_pages = win.reshape(n // PAGE, PAGE, 128)
        ks_p, pab_p = page_ctx(kvm_pages, nsa_pages, w_exp, 0, n_p // PAGE)
        km_p, cmp_p = ctx_final(ks_p.reshape(bp, lp // PAGE, 256), pab_p.reshape(bp, lp // NSA_CMP_STRIDE, 256),
                                nsa_cmp_pos, nsa_cmp_w1, nsa_cmp_b1, nsa_cmp_w2, nsa_cmp_b2, l, lp // PAGE)
        om_p, on_p = attention_slots(rel_bias, qm, qn, gn, 0, kvm_pages, nsa_pages, prompt_pages, None, None,
                                     win_pages, 0, km_p, cmp_p, tiles_p_bias,
                                     n_seq=bp, q_rows=q_chunk, n_chunks=lp // q_chunk, q0_base=0,
                                     tiles_per_seq=lp // PAGE, n_win_tiles=lp // PAGE,
                                     n_sel_blocks=lp // NSA_SEL_BLOCK, n_slots=PROMPT_SLOTS)

        ks_c, pab_c = page_ctx_cache(kv_cache, ns_cache, _cmp_w1_rows(nsa_cmp_w1[l]), l)
        ks_s, pab_s = ctx_gather(pt_flat, ks_c, pab_c, bs, n_pages)
        km_s, cmp_s = ctx_final(ks_s, pab_s, nsa_cmp_pos, nsa_cmp_w1, nsa_cmp_b1, nsa_cmp_w2, nsa_cmp_b2, l, n_pages)
        pad_rows = lambda a: jnp.pad(a[n_p:].reshape(bs, ls, -1), ((0, 0), (0, PAGE - ls), (0, 0)))
        tail_kv = jnp.transpose(pad_rows(kvm).reshape(bs, PAGE, 2, MOBA_HEADS, HEAD_DIM), (0, 2, 3, 4, 1))
        tail_ns = jnp.transpose(pad_rows(nsa).reshape(bs, PAGE, 4, HEAD_DIM)[:, :, 2:], (0, 2, 3, 1))
        win_s = jnp.concatenate([cache_nsa_win[l].reshape(bs, -1, 128), pad_rows(win)], axis=1)
        win_s = win_s.reshape(bs * (win_s.shape[1] // PAGE), PAGE, 128)
        lf_s = -(-(past + ls) // MOBA_BLOCK) * MOBA_BLOCK
        om_s, on_s = attention_slots(rel_bias, qm, qn, gn, n_p // ls, kv_cache, ns_cache, pt_flat,
                                     tail_kv, tail_ns, win_s, (past - NSA_WINDOW) // PAGE, km_s, cmp_s, tiles_s_bias,
                                     n_seq=bs, q_rows=ls, n_chunks=1, q0_base=past,
                                     tiles_per_seq=n_pages, n_win_tiles=win_s.shape[0] // bs,
                                     n_sel_blocks=lf_s // NSA_SEL_BLOCK, n_slots=SAMPLE_SLOTS, cache_layer=l)
        om = jnp.concatenate([om_p, om_s], axis=0)
        on = jnp.concatenate([on_p, on_s], axis=0)

        dt_t = dt[:, :SSD_HEADS].T
        ssd_par = (ssd_conv_w[l], ssd_conv_b[l], ssd_dt_bias[l], ssd_a_log[l], ssd_d[l], ssd_norm_w[l])
        y_p, st_p = ssd(xbc, z, dt, dt_t[:, :n_p].reshape(SSD_HEADS, bp, lp).transpose(1, 0, 2),
                        jnp.zeros((bp, 8, SSD_CONV_DIM), F32),
                        jnp.zeros((bp, SSD_HEADS, SSD_HEAD_DIM, SSD_STATE), F32), *ssd_par,
                        n_seq=bp, seq_rows=lp, chunk=SSD_CHUNK, valid_len=lp)
        pad_s = lambda a: jnp.pad(a[n_p:].reshape(bs, ls, -1), ((0, 0), (0, sample_pad - ls), (0, 0))).reshape(bs * sample_pad, -1)
        dt_t_s = jnp.pad(dt_t[:, n_p:].reshape(SSD_HEADS, bs, ls).transpose(1, 0, 2), ((0, 0), (0, 0), (0, sample_pad - ls)))
        conv_prev = jnp.pad(state_ssd_conv[l], ((0, 0), (8 - (SSD_CONV - 1), 0), (0, 0)))
        y_s, st_s = ssd(pad_s(xbc), pad_s(z), pad_s(dt), dt_t_s, conv_prev, state_ssd[l], *ssd_par,
                        n_seq=bs, seq_rows=sample_pad, chunk=sample_pad, valid_len=ls)
        ys = jnp.concatenate([y_p[:n_p], y_s.reshape(bs, sample_pad, -1)[:, :ls].reshape(n_s, -1)], axis=0)

        x1, h2, h2t, logits_t = mix_out(om, on, ys, gbr, x, mods, nw[1:2], nw[2:3],
                                        w_branch_moba[l].astype(BF16), w_branch_nsa[l].astype(BF16),
                                        w_branch_ssd[l].astype(BF16), w_out[l].astype(BF16), router_w[l].T)
        eidx, w8, pos8, cnt = router(logits_t, router_b[l])
        cnt = cnt[:, 0].astype(jnp.int32)
        padded = (cnt + EXPERT_ROWS - 1) // EXPERT_ROWS * EXPERT_ROWS
        ends = jnp.cumsum(padded)
        off = ends - padded
        e_ids = jnp.arange(N_EXPERTS, dtype=jnp.int32)
        off_of = jnp.sum(jnp.where(eidx[:, :, None] == e_ids, off, 0), axis=-1)
        dest = (off_of + pos8).T.reshape(-1)
        blk_start = jnp.arange(n_blk, dtype=jnp.int32) * EXPERT_ROWS
        blk_e = jnp.minimum(jnp.sum((ends[None, :] <= blk_start[:, None]).astype(jnp.int32), axis=1), N_EXPERTS - 1)
        n_used = (ends[-1] // EXPERT_ROWS).astype(jnp.int32).reshape(1)
        x_rows = dispatch(h2t, dest, n_rows)
        y_rows = experts(x_rows, blk_e, n_used, exp_w_gate, exp_w_up, exp_w_down, l)
        x = combine(dest, w8.T, x1, h2, mods, nw[3:4], shared_w_gate[l].astype(BF16), shared_w_up[l].astype(BF16),
                    shared_w_down[l].astype(BF16), y_rows)

        keep = min(NSA_WINDOW, lp)
        outs[0].append(kvm[:n_p].reshape(bp, lp, 2, MOBA_HEADS, HEAD_DIM))
        outs[1].append(kvm[n_p:].reshape(bs, ls, 2, MOBA_HEADS, HEAD_DIM))
        outs[2].append(nsa[:n_p].reshape(bp, lp, 4, HEAD_DIM))
        outs[3].append(nsa[n_p:].reshape(bs, ls, 4, HEAD_DIM))
        outs[4].append(win[:n_p].reshape(bp, lp, 2, HEAD_DIM)[:, lp - keep:])
        outs[5].append(win[n_p:].reshape(bs, ls, 2, HEAD_DIM))
        outs[6].append(xbc[:n_p].reshape(bp, lp, -1)[:, lp - (SSD_CONV - 1):])
        outs[7].append(xbc[n_p:].reshape(bs, ls, -1)[:, ls - (SSD_CONV - 1):])
        outs[8].append(st_p)
        outs[9].append(st_s)

    y_prompt = x[:n_p].reshape(bp, lp, d)
    y_sample = x[n_p:].reshape(bs, ls, d)
    return (y_prompt, y_sample) + tuple(jnp.stack(o) for o in outs)
```

```python
import functools
import math

import numpy as np
import jax
import jax.numpy as jnp
from jax import lax
from jax.experimental import pallas as pl
from jax.experimental.pallas import tpu as pltpu

F32 = jnp.float32
BF16 = jnp.bfloat16
HI = lax.Precision.HIGHEST

D_MODEL = 1024
PAGE = 128
HEAD_DIM = 64
MOBA_HEADS = 4
MOBA_BLOCK = 256
MOBA_TOPK = 3
NSA_HEADS = 4
NSA_CMP_STRIDE = 16
NSA_SEL_BLOCK = 64
NSA_TOPN = 16
NSA_WINDOW = 512
SSD_HEADS = 8
SSD_HEAD_DIM = 64
SSD_INNER = 512
SSD_GROUPS = 2
SSD_STATE = 128
SSD_CONV = 4
SSD_CHUNK = 256
SSD_CONV_DIM = 1024
N_BUCKETS = 32
MAX_DISTANCE = 128
N_EXPERTS = 64
TOP_K = 8
N_ROUTE_GROUPS = 8
TOPK_GROUPS = 4
D_EXPERT = 256
ROUTE_SCALE = 2.5
EPS = 1e-6

TM = 256
ROW_TILES = D_MODEL // 128
EXPERT_ROWS = 512
VMEM_LIMIT = 56 * 1024 * 1024
NEG = -1e30

_SEG = dict(qm=(0, 256), kvm=(256, 512), qn=(768, 256), nsa=(1024, 256), win=(1280, 128),
            gn=(1408, 128), z=(1536, 512), xbc=(2048, 1024), dt=(3072, 128), gbr=(3200, 3072))
PROJ_W = 6272


def _sigmoid(x):
    return 1.0 / (1.0 + jnp.exp(-x))


def _silu(x):
    return x * _sigmoid(x)


def _rms(x, w):
    return x * lax.rsqrt(jnp.mean(x * x, axis=-1, keepdims=True) + EPS) * w


def _dot(a, b):
    return jnp.dot(a.astype(BF16), b.astype(BF16), preferred_element_type=F32)


def _dot_nt(a, b, precision=None):
    if precision is None:
        a, b = a.astype(BF16), b.astype(BF16)
    return lax.dot_general(a, b, (((1,), (1,)), ((), ())), precision=precision,
                           preferred_element_type=F32)


def _cparams(sem, vmem=None):
    return pltpu.CompilerParams(dimension_semantics=sem, vmem_limit_bytes=vmem or VMEM_LIMIT)


def _mod_kernel(c_ref, w_ref, b_ref, o_ref):
    o_ref[...] = _dot(_silu(c_ref[...]), w_ref[...]) + b_ref[...]


def modulation(c_all, ada_w, ada_b, layer):
    rows, d = c_all.shape
    n = ada_w.shape[-1]
    tn = 512
    return pl.pallas_call(
        _mod_kernel,
        grid=(n // tn,),
        in_specs=[pl.BlockSpec((rows, d), lambda j: (0, 0)),
                  pl.BlockSpec((None, d, tn), lambda j: (layer, 0, j)),
                  pl.BlockSpec((None, 1, tn), lambda j: (layer, 0, j))],
        out_specs=pl.BlockSpec((rows, tn), lambda j: (0, j)),
        out_shape=jax.ShapeDtypeStruct((rows, n), F32),
        compiler_params=_cparams(("parallel",)),
        name="modulation",
    )(c_all, ada_w, ada_b.reshape(ada_b.shape[0], 1, n))


def _proj_kernel(x_ref, sh_ref, sc_ref, nw_ref, w_ref, qm_ref, kvm_ref, qn_ref, nsa_ref, win_ref,
                 gn_ref, z_ref, xbc_ref, dt_ref, gbr_ref):
    h = _rms(x_ref[...], nw_ref[...]) * (1.0 + sc_ref[...]) + sh_ref[...]
    hb = h.astype(BF16)

    def seg(name):
        o, w = _SEG[name]
        return jnp.dot(hb, w_ref[:, o:o + w], preferred_element_type=F32)

    qm_ref[...] = seg("qm")
    kvm_ref[...] = seg("kvm")
    qn_ref[...] = seg("qn")
    nsa_ref[...] = seg("nsa")
    win_ref[...] = seg("win")
    gn_ref[...] = _sigmoid(seg("gn"))
    z_ref[...] = seg("z")
    xbc_ref[...] = seg("xbc")
    dt_ref[...] = seg("dt")
    gbr_ref[...] = _sigmoid(seg("gbr"))


def _mod_spec(which, group_of_tile):
    return pl.BlockSpec((None, TM, D_MODEL), lambda i: (group_of_tile(i), 0, which))


def proj_in(x, modx, nw, w_packed, group_of_tile):
    n, d = x.shape
    names = ["qm", "kvm", "qn", "nsa", "win", "gn", "z", "xbc", "dt", "gbr"]
    return pl.pallas_call(
        _proj_kernel,
        grid=(n // TM,),
        in_specs=[pl.BlockSpec((TM, d), lambda i: (i, 0)),
                  _mod_spec(0, group_of_tile), _mod_spec(1, group_of_tile),
                  pl.BlockSpec((1, d), lambda i: (0, 0)),
                  pl.BlockSpec((d, PROJ_W), lambda i: (0, 0), pipeline_mode=pl.Buffered(1))],
        out_specs=[pl.BlockSpec((TM, _SEG[k][1]), lambda i: (i, 0)) for k in names],
        out_shape=[jax.ShapeDtypeStruct((n, _SEG[k][1]), F32) for k in names],
        compiler_params=_cparams(("parallel",)),
        name="proj_in",
    )(x, modx, modx, nw, w_packed)


CTX_PAGES = 16


def _page_ctx_kernel(k_ref, g_ref, w_ref, ks_ref, pab_ref):
    ks_ref[...] = jnp.sum(k_ref[...], axis=1)
    pab_ref[...] = _dot(g_ref[...], w_ref[...])


def page_ctx(kv_pages, nsa_pages, w_exp, page0, n_pages):
    groups = PAGE // NSA_CMP_STRIDE
    g_view = nsa_pages.reshape(nsa_pages.shape[0] * groups, NSA_CMP_STRIDE * 256)
    blk0 = page0 // CTX_PAGES
    return pl.pallas_call(
        _page_ctx_kernel,
        grid=(n_pages // CTX_PAGES,),
        in_specs=[pl.BlockSpec((CTX_PAGES, PAGE, 256), lambda i: (blk0 + i, 0, 0)),
                  pl.BlockSpec((CTX_PAGES * groups, NSA_CMP_STRIDE * 256), lambda i: (blk0 + i, 0)),
                  pl.BlockSpec((NSA_CMP_STRIDE * 256, 256), lambda i: (0, 0))],
        out_specs=[pl.BlockSpec((CTX_PAGES, 256), lambda i: (i, 0)),
                   pl.BlockSpec((CTX_PAGES * groups, 256), lambda i: (i, 0))],
        out_shape=[jax.ShapeDtypeStruct((n_pages, 256), F32),
                   jax.ShapeDtypeStruct((n_pages * groups, 256), F32)],
        compiler_params=_cparams(("parallel",)),
        name="page_ctx",
    )(kv_pages, g_view, w_exp)


def _page_ctx_cache_kernel(k_ref, c_ref, w_ref, ks_ref, pab_ref, tok):
    n_tok = CTX_PAGES * PAGE
    kt = jnp.concatenate([k_ref[p].reshape(MOBA_HEADS * HEAD_DIM, PAGE) for p in range(CTX_PAGES)], axis=1)
    page_of_lane = lax.shift_right_logical(lax.broadcasted_iota(jnp.int32, (CTX_PAGES, n_tok), 1), 7)
    ind = jnp.where(page_of_lane == lax.broadcasted_iota(jnp.int32, (CTX_PAGES, n_tok), 0), 1.0, 0.0)
    ks_ref[...] = _dot_nt(ind, kt)
    for p in range(CTX_PAGES):
        for c in range(2):
            tok[p * PAGE:(p + 1) * PAGE, HEAD_DIM * c:HEAD_DIM * (c + 1)] = c_ref[p, c].T
    groups = n_tok // NSA_CMP_STRIDE
    acc = jnp.zeros((groups, 256), F32)
    for r in range(NSA_CMP_STRIDE):
        acc = acc + _dot(tok[pl.ds(r, groups, stride=NSA_CMP_STRIDE), :], w_ref[128 * r:128 * (r + 1), :])
    pab_ref[...] = acc


def page_ctx_cache(kv_tiles, ns_tiles, w_cmp, layer):
    n_pages = kv_tiles.shape[1]
    groups = PAGE // NSA_CMP_STRIDE
    return pl.pallas_call(
        _page_ctx_cache_kernel,
        grid=(n_pages // CTX_PAGES,),
        in_specs=[pl.BlockSpec((None, CTX_PAGES, None, MOBA_HEADS, HEAD_DIM, PAGE), lambda i: (layer, i, 0, 0, 0, 0)),
                  pl.BlockSpec((None, CTX_PAGES, 2, HEAD_DIM, PAGE), lambda i: (layer, i, 0, 0, 0)),
                  pl.BlockSpec(w_cmp.shape, lambda i: (0, 0))],
        out_specs=[pl.BlockSpec((CTX_PAGES, 256), lambda i: (i, 0)),
                   pl.BlockSpec((CTX_PAGES * groups, 256), lambda i: (i, 0))],
        out_shape=[jax.ShapeDtypeStruct((n_pages, 256), F32),
                   jax.ShapeDtypeStruct((n_pages * groups, 256), F32)],
        scratch_shapes=[pltpu.VMEM((CTX_PAGES * PAGE, 128), F32)],
        compiler_params=_cparams(("parallel",)),
        name="page_ctx_cache",
    )(kv_tiles, ns_tiles, w_cmp)


def _ctx_gather_kernel(pt_ref, ks_hbm, pab_hbm, ks_out, pab_out, sem):
    b = pl.program_id(0)
    n_pages = ks_out.shape[1]
    groups = pab_out.shape[1] // n_pages

    def copies(p):
        page = pt_ref[b * n_pages + p]
        c1 = pltpu.make_async_copy(ks_hbm.at[pl.ds(page, 1)], ks_out.at[0, pl.ds(p, 1)], sem.at[0])
        c2 = pltpu.make_async_copy(pab_hbm.at[pl.ds(page * groups, groups)],
                                   pab_out.at[0, pl.ds(p * groups, groups)], sem.at[1])
        return c1, c2

    def start(p, carry):
        c1, c2 = copies(p)
        c1.start()
        c2.start()
        return carry

    def wait(p, carry):
        c1, c2 = copies(p)
        c1.wait()
        c2.wait()
        return carry

    lax.fori_loop(0, n_pages, start, 0)
    lax.fori_loop(0, n_pages, wait, 0)


def ctx_gather(page_table_flat, ksum, pab, n_seq, n_pages):
    groups = PAGE // NSA_CMP_STRIDE
    gs = pltpu.PrefetchScalarGridSpec(
        num_scalar_prefetch=1,
        grid=(n_seq,),
        in_specs=[pl.BlockSpec(memory_space=pl.ANY), pl.BlockSpec(memory_space=pl.ANY)],
        out_specs=[pl.BlockSpec((1, n_pages) + ksum.shape[1:], lambda b, pt: (b,) + (0,) * ksum.ndim),
                   pl.BlockSpec((1, n_pages * groups, 256), lambda b, pt: (b, 0, 0))],
        scratch_shapes=[pltpu.SemaphoreType.DMA((2,))],
    )
    return pl.pallas_call(
        _ctx_gather_kernel,
        grid_spec=gs,
        out_shape=[jax.ShapeDtypeStruct((n_seq, n_pages) + ksum.shape[1:], F32),
                   jax.ShapeDtypeStruct((n_seq, n_pages * groups, 256), F32)],
        compiler_params=_cparams(("arbitrary",)),
        name="ctx_gather",
    )(page_table_flat, ksum, pab)


def _gelu_tanh(x):
    return 0.5 * x * (1.0 + jnp.tanh(math.sqrt(2.0 / math.pi) * (x + 0.044715 * (x * x * x))))


def _ctx_final_kernel(ks_ref, pab_ref, pos_ref, w1_ref, b1_ref, w2_ref, b2_ref, km_ref, cmp_ref, *, head_rows):
    nb = km_ref.shape[0]
    n_pages = 2 * nb
    r = lax.broadcasted_iota(jnp.int32, (nb, n_pages), 0)
    c = lax.broadcasted_iota(jnp.int32, (nb, n_pages), 1)
    pair = jnp.where((c == 2 * r) | (c == 2 * r + 1), 1.0, 0.0).astype(F32)
    if head_rows:
        ks = jnp.concatenate([ks_ref[pl.ds(h, n_pages, stride=2 * MOBA_HEADS), :] for h in range(MOBA_HEADS)], axis=1)
    else:
        ks = ks_ref[...]
    km_ref[...] = jnp.dot(pair, ks, precision=HI, preferred_element_type=F32) * (1.0 / MOBA_BLOCK)

    pab = pab_ref[...]
    ng = pab.shape[0]
    row = lax.broadcasted_iota(jnp.int32, (ng, 1), 0)
    outs = []
    for kv in range(2):
        pa = pab[:, 128 * kv:128 * kv + 64]
        pb = pab[:, 128 * kv + 64:128 * kv + 128]
        pb_next = jnp.where(row == ng - 1, 0.0, pltpu.roll(pb, ng - 1, 0))
        pos_term = _dot(pos_ref[kv], w1_ref[kv]) + b1_ref[kv]
        hid = _gelu_tanh(pa + pb_next + pos_term)
        outs.append(_dot(hid, w2_ref[kv]) + b2_ref[kv])
    cmp_ref[...] = jnp.concatenate(outs, axis=1)


def ctx_final(ksum, pab, pos, w1, b1, w2, b2, layer, n_pages):
    nb = ksum.shape[0]
    head_rows = ksum.shape[-1] == HEAD_DIM
    ng = pab.shape[1]
    lf = pos.shape[2] * pos.shape[3]
    pos2 = pos.reshape(pos.shape[0], 2, 1, lf)
    return pl.pallas_call(
        functools.partial(_ctx_final_kernel, head_rows=head_rows),
        grid=(nb,),
        in_specs=[pl.BlockSpec((None,) + ksum.shape[1:], lambda b: (b, 0, 0)),
                  pl.BlockSpec((None, ng, 256), lambda b: (b, 0, 0)),
                  pl.BlockSpec((None, 2, 1, lf), lambda b: (layer, 0, 0, 0)),
                  pl.BlockSpec((None, 2, lf, HEAD_DIM), lambda b: (layer, 0, 0, 0)),
                  pl.BlockSpec((None, 2, 1, HEAD_DIM), lambda b: (layer, 0, 0, 0)),
                  pl.BlockSpec((None, 2, HEAD_DIM, HEAD_DIM), lambda b: (layer, 0, 0, 0)),
                  pl.BlockSpec((None, 2, 1, HEAD_DIM), lambda b: (layer, 0, 0, 0))],
        out_specs=[pl.BlockSpec((None, n_pages // 2, 256), lambda b: (b, 0, 0)),
                   pl.BlockSpec((None, ng, 128), lambda b: (b, 0, 0))],
        out_shape=[jax.ShapeDtypeStruct((nb, n_pages // 2, 256), F32),
                   jax.ShapeDtypeStruct((nb, ng, 128), F32)],
        compiler_params=_cparams(("parallel",)),
        name="ctx_final",
    )(ksum, pab, pos2, w1, b1.reshape(b1.shape[0], 2, 1, HEAD_DIM), w2, b2.reshape(b2.shape[0], 2, 1, HEAD_DIM))


def _topk_lanes(score, k, lane_f, n_lanes):
    sel = jnp.zeros_like(score)
    for _ in range(k):
        m = jnp.max(score, axis=1, keepdims=True)
        idx = jnp.min(jnp.where(score == m, lane_f, float(n_lanes)), axis=1, keepdims=True)
        hit = lane_f == idx
        sel = jnp.where(hit & (m > -jnp.inf), 1.0, sel)
        score = jnp.where(hit, -jnp.inf, score)
    return sel


def _rel_bucket_np(dist):
    n = np.maximum(dist, 0)
    exact = N_BUCKETS // 2
    nf = np.maximum(n, 1).astype(np.float32)
    large = exact + (np.log(nf / np.float32(exact)) / np.float32(math.log(MAX_DISTANCE / exact))
                     * np.float32(N_BUCKETS - exact)).astype(np.int32)
    return np.where(n < exact, n, np.minimum(large, N_BUCKETS - 1)).astype(np.int32)


WIN_TILES = NSA_WINDOW // PAGE + 1


def _bucket_thresholds():
    b = _rel_bucket_np(np.arange(4 * MAX_DISTANCE))
    return [int(np.argmax(b >= k)) for k in range(N_BUCKETS)]


def _softmax_update(m_ref, l_ref, a_ref, idx, s, valid, pv_fn):
    s = jnp.where(valid, s, NEG)
    m_old = m_ref[idx]
    m_new = jnp.maximum(m_old, jnp.max(s, axis=1, keepdims=True))
    p = jnp.exp(s - m_new)
    alpha = jnp.exp(m_old - m_new)
    l_ref[idx] = alpha * l_ref[idx] + jnp.sum(p, axis=1, keepdims=True)
    a_ref[idx] = alpha * a_ref[idx] + pv_fn(p)
    m_ref[idx] = m_new


def _pv_slots(vs, pv_dot):
    def f(p):
        acc = pv_dot(p[:, 0:PAGE], vs[0])
        for j in range(1, len(vs)):
            acc = acc + pv_dot(p[:, PAGE * j:PAGE * (j + 1)], vs[j])
        return acc
    return f


def _attn_slots_kernel(si_ref, st_ref, sl_ref, pg_ref, *refs, q_rows, q0_base, n_slots, cache_tiles, tail_tile):
    Q, G = q_rows, n_slots
    rb_ref, qm_ref, qn_ref, gn_ref = refs[0:4]
    kv_refs = refs[4:4 + G]
    ns_refs = refs[4 + G:4 + 2 * G]
    tkv_ref, tns_ref = refs[4 + 2 * G:6 + 2 * G]
    win_refs = refs[6 + 2 * G:6 + 2 * G + WIN_TILES]
    base = 6 + 2 * G + WIN_TILES
    km_ref, cmp_ref, tbm_ref, tbn_ref, cb_ref, pool_ref, om_ref, on_ref = refs[base:base + 8]
    qms, qns, qbd, selm, sels, m_m, l_m, a_m, m_n, l_n, a_n, oc, ow = refs[base + 8:]

    s_id = pl.program_id(1)
    i = si_ref[s_id]
    t0 = st_ref[s_id]
    q0 = q0_base + i * Q
    ob = lax.shift_right_logical(q0, 8)
    nb = km_ref.shape[0]
    nbs_p = pool_ref.shape[1]
    nc = cmp_ref.shape[0]
    R4 = NSA_HEADS * Q

    qi = lax.broadcasted_iota(jnp.int32, (Q, 1), 0)
    qi4 = jnp.concatenate([qi] * NSA_HEADS, axis=0)
    kj = lax.broadcasted_iota(jnp.int32, (Q, PAGE), 1)
    lane_nb = lax.broadcasted_iota(jnp.int32, (Q, nb), 1)
    lane_bs = lax.broadcasted_iota(jnp.int32, (Q, nbs_p), 1)
    rep4 = lambda x: jnp.concatenate([x] * NSA_HEADS, axis=0)

    @pl.when(t0 == 0)
    def _init():
        qm = qm_ref[...]
        qn = qn_ref[...]
        qms[...] = qm * (HEAD_DIM ** -0.5)
        qbd[...] = jnp.zeros(qbd.shape, F32)
        for h in range(NSA_HEADS):
            sl = slice(HEAD_DIM * h, HEAD_DIM * (h + 1))
            qns[h * Q:(h + 1) * Q, :] = qn[:, sl] * (HEAD_DIM ** -0.5)
            qbd[h * Q:(h + 1) * Q, sl] = qm[:, sl] * (HEAD_DIM ** -0.5)
        qn4 = qns[...]
        km = km_ref[...]
        lane_f = lane_nb.astype(F32)
        for h in range(MOBA_HEADS):
            sl = slice(HEAD_DIM * h, HEAD_DIM * (h + 1))
            g = _dot_nt(qm[:, sl], km[:, sl], precision=HI)
            g = jnp.where(lane_nb < ob, g, -jnp.inf)
            selm[h] = _topk_lanes(g, min(MOBA_TOPK, nb), lane_f, nb)
        cm = cmp_ref[...]
        epos = lax.broadcasted_iota(jnp.int32, (Q, nc), 1) * NSA_CMP_STRIDE + (2 * NSA_CMP_STRIDE - 1)
        dc = q0 + qi - epos
        sc = _dot_nt(qn4, cm[:, :HEAD_DIM]) + cb_ref[...]
        valid = rep4(dc >= 0)
        sc = jnp.where(valid, sc, NEG)
        mx = jnp.max(sc, axis=1, keepdims=True)
        e = jnp.where(valid, jnp.exp(sc - mx), 0.0)
        den = jnp.sum(e, axis=1, keepdims=True)
        p = e / jnp.where(den > 0, den, 1.0)
        oc[...] = _dot(p, cm[:, HEAD_DIM:])
        psum = p[0:Q] + p[Q:2 * Q] + p[2 * Q:3 * Q] + p[3 * Q:4 * Q]
        imp = jnp.dot(psum, pool_ref[...], precision=HI, preferred_element_type=F32)
        cur = lax.shift_right_logical(q0 + qi, 6)
        forced = (lane_bs == 0) | (lane_bs == cur) | (lane_bs == cur - 1)
        score = jnp.where(forced, jnp.inf, jnp.where(lane_bs <= cur, imp, -jnp.inf))
        sels[...] = _topk_lanes(score, NSA_TOPN, lane_bs.astype(F32), nbs_p)
        tw0 = lax.shift_right_logical(q0, 7) - (WIN_TILES - 1)
        s_parts, v_parts, m_parts = [], [], []
        for w in range(WIN_TILES):
            wv = win_refs[w][...]
            back = WIN_TILES - 1 - w
            dw = back * PAGE + qi - kj
            s_parts.append(_dot_nt(qn4, wv[:, :HEAD_DIM]) + tbn_ref[min(back, 2)])
            m_parts.append((dw >= 0) & (dw < NSA_WINDOW) & (tw0 + w >= 0))
            v_parts.append(wv[:, HEAD_DIM:])
        sw = jnp.concatenate(s_parts, axis=1)
        wvalid = rep4(jnp.concatenate(m_parts, axis=1))
        sw = jnp.where(wvalid, sw, NEG)
        mx = jnp.max(sw, axis=1, keepdims=True)
        e = jnp.where(wvalid, jnp.exp(sw - mx), 0.0)
        den = jnp.sum(e, axis=1, keepdims=True)
        pw = e / jnp.where(den > 0, den, 1.0)
        o_w = _dot(pw[:, 0:PAGE], v_parts[0])
        for w in range(1, WIN_TILES):
            o_w = o_w + _dot(pw[:, PAGE * w:PAGE * (w + 1)], v_parts[w])
        ow[...] = o_w
        m_m[...] = jnp.full(m_m.shape, NEG, F32)
        l_m[...] = jnp.zeros(l_m.shape, F32)
        a_m[...] = jnp.zeros(a_m.shape, F32)
        m_n[...] = jnp.full(m_n.shape, NEG, F32)
        l_n[...] = jnp.zeros(l_n.shape, F32)
        a_n[...] = jnp.zeros(a_n.shape, F32)

    is_tail = (t0 == tail_tile) if tail_tile is not None else None

    def pick(j, cache_fn, tail_fn):
        x = cache_fn(kv_refs[j], ns_refs[j])
        if is_tail is not None and j == 0:
            x = jnp.where(is_tail, tail_fn(tkv_ref, tns_ref), x)
        return x

    if cache_tiles:
        k_of = lambda j, h: pick(j, *[lambda kv, ns: kv[0, h]] * 2)
        v_of = lambda j, h: pick(j, *[lambda kv, ns: kv[1, h]] * 2)
        ksel_of = lambda j: pick(j, *[lambda kv, ns: ns[0]] * 2)
        vsel_of = lambda j: pick(j, *[lambda kv, ns: ns[1]] * 2)
        qk, pv = _dot, _dot_nt
    else:
        k_of = lambda j, h: kv_refs[j][:, HEAD_DIM * h:HEAD_DIM * (h + 1)]
        v_of = lambda j, h: kv_refs[j][:, 256 + HEAD_DIM * h:256 + HEAD_DIM * (h + 1)]
        ksel_of = lambda j: ns_refs[j][:, 0:HEAD_DIM]
        vsel_of = lambda j: ns_refs[j][:, HEAD_DIM:2 * HEAD_DIM]
        qk, pv = _dot_nt, _dot

    causal, v_idx, blk = [], [], []
    for j in range(G):
        delta = q0 - (t0 + j) * PAGE
        causal.append(delta + qi - kj >= 0)
        v_idx.append(jnp.clip(lax.shift_right_arithmetic(delta, 7), 0, 2))
        blk.append(lax.shift_right_logical(t0 + j, 1))

    moba_sel = [[jnp.sum(jnp.where(lane_nb == blk[j], selm[h], 0.0), axis=1, keepdims=True) > 0.0
                 for h in range(MOBA_HEADS)] for j in range(G)]
    if cache_tiles:
        assert G % 2 == 0
        kv_all = lambda j, w: pick(j, *[lambda kv, ns: kv[w].reshape(MOBA_HEADS * HEAD_DIM, PAGE)] * 2)
        q_bd = qbd[...]
        s_parts, m_parts = [], []
        for j in range(0, G, 2):
            s_parts.append(_dot(q_bd, jnp.concatenate([kv_all(j, 0), kv_all(j + 1, 0)], axis=1)))
        for j in range(G):
            valid_j = jnp.concatenate([causal[j] & (moba_sel[j][h] | (blk[j] == ob)) for h in range(MOBA_HEADS)], axis=0)
            m_parts.append(valid_j)
        bias = jnp.concatenate([tbm_ref[v_idx[j]] for j in range(G)], axis=1)

        def pv_heads(p):
            acc = None
            for j in range(0, G, 2):
                o2 = _dot_nt(p[:, PAGE * j:PAGE * (j + 2)],
                             jnp.concatenate([kv_all(j, 1), kv_all(j + 1, 1)], axis=1))
                diag = jnp.concatenate([o2[h * Q:(h + 1) * Q, HEAD_DIM * h:HEAD_DIM * (h + 1)]
                                        for h in range(MOBA_HEADS)], axis=0)
                acc = diag if acc is None else acc + diag
            return acc

        _softmax_update(m_m, l_m, a_m, 0, jnp.concatenate(s_parts, axis=1) + bias, jnp.concatenate(m_parts, axis=1),
                        pv_heads)
    else:
        q_all = qms[...]
        for h in range(MOBA_HEADS):
            sl = slice(HEAD_DIM * h, HEAD_DIM * (h + 1))
            s_parts, m_parts = [], []
            for j in range(G):
                s_parts.append(qk(q_all[:, sl], k_of(j, h)) + tbm_ref[v_idx[j], h * Q:(h + 1) * Q, :])
                m_parts.append(causal[j] & (moba_sel[j][h] | (blk[j] == ob)))
            _softmax_update(m_m, l_m, a_m, h, jnp.concatenate(s_parts, axis=1), jnp.concatenate(m_parts, axis=1),
                            _pv_slots([v_of(j, h) for j in range(G)], pv))

    qn4 = qns[...]
    sel_all = sels[...]
    s_parts, m_parts = [], []
    for j in range(G):
        s_parts.append(qk(qn4, ksel_of(j)) + tbn_ref[v_idx[j]])
        lo = jnp.sum(jnp.where(lane_bs == 2 * (t0 + j), sel_all, 0.0), axis=1, keepdims=True)
        hi = jnp.sum(jnp.where(lane_bs == 2 * (t0 + j) + 1, sel_all, 0.0), axis=1, keepdims=True)
        m_parts.append(causal[j] & (jnp.where(kj < NSA_SEL_BLOCK, lo, hi) > 0.0))
    _softmax_update(m_n, l_n, a_n, 0, jnp.concatenate(s_parts, axis=1), rep4(jnp.concatenate(m_parts, axis=1)),
                    _pv_slots([vsel_of(j) for j in range(G)], pv))

    @pl.when(sl_ref[s_id] == 1)
    def _finish():
        outs = []
        for h in range(MOBA_HEADS):
            if cache_tiles:
                l = l_m[0, h * Q:(h + 1) * Q, :]
                a = a_m[0, h * Q:(h + 1) * Q, :]
            else:
                l, a = l_m[h], a_m[h]
            outs.append(a / jnp.where(l > 0, l, 1.0))
        om_ref[...] = jnp.concatenate(outs, axis=1)
        l = l_n[0]
        o_s = a_n[0] / jnp.where(l > 0, l, 1.0)
        o_w = ow[...]
        o_c = oc[...]
        gn = gn_ref[...]
        outs = []
        for h in range(NSA_HEADS):
            r = slice(h * Q, (h + 1) * Q)
            outs.append(gn[:, 3 * h:3 * h + 1] * o_c[r] + gn[:, 3 * h + 1:3 * h + 2] * o_s[r]
                        + gn[:, 3 * h + 2:3 * h + 3] * o_w[r])
        on_ref[...] = jnp.concatenate(outs, axis=1)


def bias_tiles(rel_bias, q_rows):
    Q = q_rows
    qi = np.arange(Q)[:, None]
    kj = np.arange(PAGE)[None, :]
    d = jnp.asarray(np.stack([v * PAGE + qi - kj for v in range(3)]))[..., None]
    thr = _bucket_thresholds()
    tb = jnp.broadcast_to(rel_bias[N_BUCKETS - 1], d.shape[:-1] + (rel_bias.shape[1],))
    for k in range(N_BUCKETS - 2, -1, -1):
        tb = jnp.where(d < thr[k + 1], rel_bias[k], tb)
    stack = lambda t: t.transpose(0, 3, 1, 2).reshape(3, t.shape[-1] * Q, PAGE)
    return stack(tb[..., :MOBA_HEADS]), stack(tb[..., MOBA_HEADS:])


def cmp_bias_table(rel_bias, q_rows, n_chunks, q0_base, nc):
    qpos = q0_base + np.arange(n_chunks)[:, None, None] * q_rows + np.arange(q_rows)[None, :, None]
    epos = (np.arange(nc) * NSA_CMP_STRIDE + 2 * NSA_CMP_STRIDE - 1)[None, None, :]
    dc = jnp.asarray((qpos - epos).astype(np.int32))[..., None]
    thr = _bucket_thresholds()
    rb = rel_bias[:, MOBA_HEADS:]
    tb = jnp.broadcast_to(rb[N_BUCKETS - 1], dc.shape[:-1] + (NSA_HEADS,))
    for k in range(N_BUCKETS - 2, -1, -1):
        tb = jnp.where(dc < thr[k + 1], rb[k], tb)
    return tb.transpose(0, 3, 1, 2).reshape(n_chunks, NSA_HEADS * q_rows, nc)


def attention_slots(rel_bias, qm, qn, gn, q_blk0, kv_src, ns_src, page_idx, tail_kv, tail_ns, win_pages, win_w0t,
                    kmean, cmp, tiles, *, n_seq, q_rows, n_chunks, q0_base, tiles_per_seq, n_win_tiles,
                    n_sel_blocks, n_slots, cache_layer=None):
    Q, G = q_rows, n_slots
    cache_tiles = cache_layer is not None
    assert PAGE % Q == 0 and q0_base % PAGE == 0 and (n_chunks == 1 or Q == PAGE)
    tbm, tbn = tiles
    nb = kmean.shape[1]
    nc = cmp.shape[1]
    nbs_p = -(-n_sel_blocks // 128) * 128
    per = NSA_SEL_BLOCK // NSA_CMP_STRIDE
    pool = jnp.asarray((np.arange(nc)[:, None] // per == np.arange(nbs_p)[None, :]).astype(np.float32))
    has_tail = tail_kv is not None

    steps = []
    for i in range(n_chunks):
        n_tiles = (q0_base + i * Q + Q - 1) // PAGE + 1
        n_st = -(-n_tiles // G)
        for s in range(n_st):
            steps.append((i, s * G, int(s == n_st - 1)))
    steps = np.asarray(steps, np.int32)
    n_steps = steps.shape[0]
    tail_tile = tiles_per_seq if has_tail else None
    assert tail_tile is None or tail_tile % G == 0
    if cache_tiles:
        assert has_tail
        kv_blk = (None, None, 2, MOBA_HEADS, HEAD_DIM, PAGE)
        ns_blk = (None, None, 2, HEAD_DIM, PAGE)
        tail_kv_blk, tail_ns_blk = kv_blk[1:], ns_blk[1:]
    else:
        assert not has_tail
        kv_blk = tail_kv_blk = (None, PAGE, 512)
        ns_blk = tail_ns_blk = (None, PAGE, 128)
        tail_kv = jnp.zeros((1, PAGE, 512), F32)
        tail_ns = jnp.zeros((1, PAGE, 256), F32)

    def q_map(b, s, si, st, sl, pg):
        return (q_blk0 + b * n_chunks + si[s], 0)

    def o_map(b, s, si, st, sl, pg):
        return (b * n_chunks + si[s], 0)

    def page_of(b, s, st, pg, j):
        return pg[b * tiles_per_seq + jnp.minimum(st[s] + j, tiles_per_seq - 1)]

    def kv_map(j):
        if cache_tiles:
            return lambda b, s, si, st, sl, pg: (cache_layer, page_of(b, s, st, pg, j), 0, 0, 0, 0)
        return lambda b, s, si, st, sl, pg: (page_of(b, s, st, pg, j), 0, 0)

    def ns_map(j):
        if cache_tiles:
            return lambda b, s, si, st, sl, pg: (cache_layer, page_of(b, s, st, pg, j), 1, 0, 0)
        return lambda b, s, si, st, sl, pg: (page_of(b, s, st, pg, j), 0, 1)

    def tail_map(nd):
        if cache_tiles:
            return lambda b, s, si, st, sl, pg: (b,) + (0,) * (nd - 1)
        return lambda b, s, si, st, sl, pg: (0, 0, 0) if nd == 0 else (0, 0, 1)

    def win_map(w):
        def f(b, s, si, st, sl, pg):
            tw = lax.shift_right_logical(q0_base + si[s] * Q, 7) - (WIN_TILES - 1) + w - win_w0t
            return (b * n_win_tiles + jnp.clip(tw, 0, n_win_tiles - 1), 0, 0)
        return f

    const = lambda nd: (lambda b, s, si, st, sl, pg: (0,) * nd)
    moba_state = (1, MOBA_HEADS * Q) if cache_tiles else (MOBA_HEADS, Q)
    in_specs = ([pl.BlockSpec(memory_space=pltpu.SMEM),
                 pl.BlockSpec((Q, 256), q_map), pl.BlockSpec((Q, 256), q_map), pl.BlockSpec((Q, 128), q_map)]
                + [pl.BlockSpec(kv_blk, kv_map(j)) for j in range(G)]
                + [pl.BlockSpec(ns_blk, ns_map(j)) for j in range(G)]
                + [pl.BlockSpec(tail_kv_blk, tail_map(5 if cache_tiles else 0)),
                   pl.BlockSpec(tail_ns_blk, tail_map(4 if cache_tiles else 1))]
                + [pl.BlockSpec((None, PAGE, 128), win_map(w)) for w in range(WIN_TILES)]
                + [pl.BlockSpec((None, nb, 256), lambda b, s, si, st, sl, pg: (b, 0, 0)),
                   pl.BlockSpec((None, nc, 128), lambda b, s, si, st, sl, pg: (b, 0, 0)),
                   pl.BlockSpec((3, MOBA_HEADS * Q, PAGE), const(3)),
                   pl.BlockSpec((3, NSA_HEADS * Q, PAGE), const(3)),
                   pl.BlockSpec((None, NSA_HEADS * Q, nc), lambda b, s, si, st, sl, pg: (si[s], 0, 0)),
                   pl.BlockSpec((nc, nbs_p), const(2))])
    gs = pltpu.PrefetchScalarGridSpec(
        num_scalar_prefetch=4,
        grid=(n_seq, n_steps),
        in_specs=in_specs,
        out_specs=[pl.BlockSpec((Q, 256), o_map), pl.BlockSpec((Q, 256), o_map)],
        scratch_shapes=[
            pltpu.VMEM((Q, 256), F32),
            pltpu.VMEM((NSA_HEADS * Q, HEAD_DIM), F32),
            pltpu.VMEM((MOBA_HEADS * Q, 256), F32),
            pltpu.VMEM((MOBA_HEADS, Q, nb), F32),
            pltpu.VMEM((Q, nbs_p), F32),
            pltpu.VMEM(moba_state + (1,), F32),
            pltpu.VMEM(moba_state + (1,), F32),
            pltpu.VMEM(moba_state + (HEAD_DIM,), F32),
            pltpu.VMEM((1, NSA_HEADS * Q, 1), F32),
            pltpu.VMEM((1, NSA_HEADS * Q, 1), F32),
            pltpu.VMEM((1, NSA_HEADS * Q, HEAD_DIM), F32),
            pltpu.VMEM((NSA_HEADS * Q, HEAD_DIM), F32),
            pltpu.VMEM((NSA_HEADS * Q, HEAD_DIM), F32),
        ],
    )
    n_tok = n_seq * n_chunks * Q
    return pl.pallas_call(
        functools.partial(_attn_slots_kernel, q_rows=Q, q0_base=q0_base, n_slots=G, cache_tiles=cache_tiles,
                          tail_tile=tail_tile),
        grid_spec=gs,
        out_shape=[jax.ShapeDtypeStruct((n_tok, 256), F32), jax.ShapeDtypeStruct((n_tok, 256), F32)],
        compiler_params=_cparams(("parallel", "arbitrary")),
        name="attention",
    )(jnp.asarray(steps[:, 0]), jnp.asarray(steps[:, 1]), jnp.asarray(steps[:, 2]), page_idx, rel_bias,
      qm, qn, gn, *([kv_src] * G), *([ns_src] * G), tail_kv, tail_ns, *([win_pages] * WIN_TILES),
      kmean, cmp, tbm, tbn, cmp_bias_table(rel_bias, Q, n_chunks, q0_base, nc), pool)


def _softplus(x):
    return jnp.maximum(x, 0.0) + jnp.log(1.0 + jnp.exp(-jnp.abs(x)))


def _ssd_kernel(xbc_ref, z_ref, dt_ref, dtt_ref, cprev_ref, st0_ref, cw_ref, cb_ref, dtb_ref, dtbt_ref,
                al_ref, alt_ref, dd_ref, nw_ref, y_ref, st_ref, xp, *, valid_len):
    c = pl.program_id(1)
    cl = xbc_ref.shape[0]
    gn = SSD_GROUPS * SSD_STATE

    @pl.when(c == 0)
    def _first():
        xp[0:8, :] = cprev_ref[...]
        st_ref[...] = st0_ref[...]

    xp[8:8 + cl, :] = xbc_ref[...]
    conv = cb_ref[...] + cw_ref[0:1, :] * xp[5:5 + cl, :]
    for k in range(1, SSD_CONV):
        conv = conv + cw_ref[k:k + 1, :] * xp[5 + k:5 + k + cl, :]
    xp[0:8, :] = xp[cl:cl + 8, :]
    act = _silu(conv)
    xs = act[:, :SSD_INNER]
    bm = act[:, SSD_INNER:SSD_INNER + gn]
    cm = act[:, SSD_INNER + gn:]

    row = lax.broadcasted_iota(jnp.int32, (cl, cl), 0)
    col = lax.broadcasted_iota(jnp.int32, (cl, cl), 1)
    tri = row >= col
    pos_r = c * cl + lax.broadcasted_iota(jnp.int32, (cl, 1), 0)
    pos_c = c * cl + lax.broadcasted_iota(jnp.int32, (1, cl), 1)
    dt = jnp.where(pos_r < valid_len, _softplus(dt_ref[...] + dtb_ref[...]), 0.0)
    dtt = jnp.where(pos_c < valid_len, _softplus(dtt_ref[...] + dtbt_ref[...]), 0.0)
    da = dt * (-jnp.exp(al_ref[...]))
    dat = dtt * (-jnp.exp(alt_ref[...]))
    acum = jnp.dot(jnp.where(tri, 1.0, 0.0), da, precision=HI, preferred_element_type=F32)
    acumt = jnp.dot(dat, jnp.where(row <= col, 1.0, 0.0), precision=HI, preferred_element_type=F32)

    cbs = [_dot_nt(cm[:, SSD_STATE * g:SSD_STATE * (g + 1)], bm[:, SSD_STATE * g:SSD_STATE * (g + 1)])
           for g in range(SSD_GROUPS)]
    ys = []
    for h in range(SSD_HEADS):
        g = h // (SSD_HEADS // SSD_GROUPS)
        bg = bm[:, SSD_STATE * g:SSD_STATE * (g + 1)]
        cg = cm[:, SSD_STATE * g:SSD_STATE * (g + 1)]
        a_col = acum[:, h:h + 1]
        a_row = acumt[h:h + 1, :]
        a_last = acumt[h:h + 1, cl - 1:cl]
        lmat = jnp.exp(jnp.where(tri, a_col - a_row, -jnp.inf))
        xh = xs[:, SSD_HEAD_DIM * h:SSD_HEAD_DIM * (h + 1)]
        xdt = xh * dt[:, h:h + 1]
        y_diag = _dot(cbs[g] * lmat, xdt)
        prev = st_ref[h]
        y_off = jnp.exp(a_col) * _dot_nt(cg, prev)
        decay = jnp.exp(a_last - a_col)
        upd = lax.dot_general(xdt.astype(BF16), (bg * decay).astype(BF16), (((0,), (0,)), ((), ())),
                              preferred_element_type=F32)
        st_ref[h] = prev * jnp.exp(a_last) + upd
        ys.append(y_diag + y_off + dd_ref[:, h:h + 1] * xh)
    y = jnp.concatenate(ys, axis=1) * _silu(z_ref[...])
    half = SSD_INNER // SSD_GROUPS
    outs = []
    for g in range(SSD_GROUPS):
        yg = y[:, half * g:half * (g + 1)]
        outs.append(yg * lax.rsqrt(jnp.mean(yg * yg, axis=-1, keepdims=True) + EPS))
    y_ref[...] = jnp.concatenate(outs, axis=1) * nw_ref[...]


def ssd(xbc, z, dt, dt_t, conv_prev8, state0, cw, cb, dtb, alog, dd, nw, *, n_seq, seq_rows, chunk, valid_len):
    n_ch = seq_rows // chunk

    def pad128(v):
        return jnp.pad(v.reshape(1, -1), ((0, 0), (0, 128 - v.shape[-1])))

    tok = lambda w: pl.BlockSpec((chunk, w), lambda b, c: (b * n_ch + c, 0))
    full = lambda shp: pl.BlockSpec(shp, lambda b, c: (0,) * len(shp))
    return pl.pallas_call(
        functools.partial(_ssd_kernel, valid_len=valid_len),
        grid=(n_seq, n_ch),
        in_specs=[tok(SSD_CONV_DIM), tok(SSD_INNER), tok(128),
                  pl.BlockSpec((None, SSD_HEADS, chunk), lambda b, c: (b, 0, c)),
                  pl.BlockSpec((None, 8, SSD_CONV_DIM), lambda b, c: (b, 0, 0)),
                  pl.BlockSpec((None, SSD_HEADS, SSD_HEAD_DIM, SSD_STATE), lambda b, c: (b, 0, 0, 0)),
                  full((SSD_CONV, SSD_CONV_DIM)), full((1, SSD_CONV_DIM)), full((1, 128)), full((SSD_HEADS, 1)),
                  full((1, 128)), full((SSD_HEADS, 1)), full((1, 128)), full((1, SSD_INNER))],
        out_specs=[tok(SSD_INNER),
                   pl.BlockSpec((None, SSD_HEADS, SSD_HEAD_DIM, SSD_STATE), lambda b, c: (b, 0, 0, 0))],
        out_shape=[jax.ShapeDtypeStruct((n_seq * seq_rows, SSD_INNER), F32),
                   jax.ShapeDtypeStruct((n_seq, SSD_HEADS, SSD_HEAD_DIM, SSD_STATE), F32)],
        scratch_shapes=[pltpu.VMEM((chunk + 8, SSD_CONV_DIM), F32)],
        compiler_params=_cparams(("parallel", "arbitrary")),
        name="ssd",
    )(xbc, z, dt, dt_t, conv_prev8, state0, cw, cb.reshape(1, -1), pad128(dtb), dtb.reshape(-1, 1),
      pad128(alog), alog.reshape(-1, 1), pad128(dd), nw.reshape(1, -1))


def _mix_kernel(om_ref, on_ref, ys_ref, gbr_ref, x_ref, g1_ref, sc2_ref, sh2_ref, nw1_ref, nw2_ref,
                wbm_ref, wbn_ref, wbs_ref, wo_ref, rwt_ref, x1_ref, h2_ref, h2t_ref, lg_ref):
    d = D_MODEL
    ya = _dot(om_ref[...], wbm_ref[...])
    yb = _dot(on_ref[...], wbn_ref[...])
    yc = _dot(ys_ref[...], wbs_ref[...])
    merged = gbr_ref[:, 0:d] * ya + gbr_ref[:, d:2 * d] * yb + gbr_ref[:, 2 * d:3 * d] * yc
    m = _dot(merged, wo_ref[...])
    x1 = x_ref[...] + g1_ref[...] * _rms(m, nw1_ref[...])
    h2 = _rms(x1, nw2_ref[...]) * (1.0 + sc2_ref[...]) + sh2_ref[...]
    x1_ref[...] = x1
    h2_ref[...] = h2
    for s in range(ROW_TILES):
        h2t_ref[pl.ds(s, TM, stride=ROW_TILES), :] = h2[:, 128 * s:128 * (s + 1)]
    lg_ref[...] = _dot_nt(rwt_ref[...], h2, precision=HI)


def mix_out(om, on, ys, gbr, x, modx, nw1, nw2, wbm, wbn, wbs, wo, rwt, group_of_tile):
    n, d = x.shape
    tok = lambda w: pl.BlockSpec((TM, w), lambda i: (i, 0))
    full = lambda a: pl.BlockSpec(a.shape, lambda i: (0,) * a.ndim, pipeline_mode=pl.Buffered(1))
    return pl.pallas_call(
        _mix_kernel,
        grid=(n // TM,),
        in_specs=[tok(256), tok(256), tok(SSD_INNER), tok(3 * d), tok(d),
                  _mod_spec(2, group_of_tile), _mod_spec(4, group_of_tile), _mod_spec(3, group_of_tile),
                  full(nw1), full(nw2), full(wbm), full(wbn), full(wbs), full(wo), full(rwt)],
        out_specs=[tok(d), tok(d), pl.BlockSpec((TM * ROW_TILES, 128), lambda i: (i, 0)),
                   pl.BlockSpec((N_EXPERTS, TM), lambda i: (0, i))],
        out_shape=[jax.ShapeDtypeStruct((n, d), F32), jax.ShapeDtypeStruct((n, d), F32),
                   jax.ShapeDtypeStruct((n * ROW_TILES, 128), F32), jax.ShapeDtypeStruct((N_EXPERTS, n), F32)],
        compiler_params=_cparams(("parallel",)),
        name="mix_out",
    )(om, on, ys, gbr, x, modx, modx, modx, nw1, nw2, wbm, wbn, wbs, wo, rwt)


def _router_kernel(lg_ref, rb_ref, eidx_ref, w8_ref, pos_ref, cnt_ref, carry):
    i = pl.program_id(0)
    tm = lg_ref.shape[1]
    per = N_EXPERTS // N_ROUTE_GROUPS

    @pl.when(i == 0)
    def _zero():
        carry[...] = jnp.zeros(carry.shape, F32)

    s = _sigmoid(lg_ref[...])
    sc = s + rb_ref[...]
    sub = lax.broadcasted_iota(jnp.int32, (per, tm), 0).astype(F32)
    gs_rows = []
    for g in range(N_ROUTE_GROUPS):
        x = sc[per * g:per * (g + 1), :]
        m1 = jnp.max(x, axis=0, keepdims=True)
        i1 = jnp.min(jnp.where(x == m1, sub, float(per)), axis=0, keepdims=True)
        m2 = jnp.max(jnp.where(sub == i1, -jnp.inf, x), axis=0, keepdims=True)
        gs_rows.append(m1 + m2)
    gs = jnp.concatenate(gs_rows, axis=0)
    gsel = jnp.zeros_like(gs)
    for _ in range(TOPK_GROUPS):
        m = jnp.max(gs, axis=0, keepdims=True)
        ix = jnp.min(jnp.where(gs == m, sub, float(N_ROUTE_GROUPS)), axis=0, keepdims=True)
        hit = sub == ix
        gsel = jnp.where(hit, 1.0, gsel)
        gs = jnp.where(hit, -jnp.inf, gs)
    emask = jnp.concatenate([jnp.broadcast_to(gsel[g:g + 1, :], (per, tm)) for g in range(N_ROUTE_GROUPS)], axis=0)
    msc = jnp.where(emask > 0.0, sc, -jnp.inf)
    e_io = lax.broadcasted_iota(jnp.int32, (N_EXPERTS, tm), 0).astype(F32)
    sel = jnp.zeros_like(sc)
    idxs = []
    for _ in range(TOP_K):
        m = jnp.max(msc, axis=0, keepdims=True)
        ix = jnp.min(jnp.where(msc == m, e_io, float(N_EXPERTS)), axis=0, keepdims=True)
        hit = e_io == ix
        sel = jnp.where(hit, 1.0, sel)
        msc = jnp.where(hit, -jnp.inf, msc)
        idxs.append(ix)
    w = s * sel
    wn = w / jnp.sum(w, axis=0, keepdims=True) * ROUTE_SCALE
    r = lax.broadcasted_iota(jnp.int32, (tm, tm), 0)
    cidx = lax.broadcasted_iota(jnp.int32, (tm, tm), 1)
    upper = jnp.where(r <= cidx, 1.0, 0.0)
    cum = _dot(sel, upper)
    rank = cum - sel + carry[:, 0:1]
    eidx_rows, w_rows, p_rows = [], [], []
    for k in range(TOP_K):
        hit = e_io == idxs[k]
        eidx_rows.append(idxs[k])
        w_rows.append(jnp.sum(jnp.where(hit, wn, 0.0), axis=0, keepdims=True))
        p_rows.append(jnp.sum(jnp.where(hit, rank, 0.0), axis=0, keepdims=True))
    eidx_ref[...] = jnp.concatenate(eidx_rows, axis=0).astype(jnp.int32)
    w8_ref[...] = jnp.concatenate(w_rows, axis=0)
    pos_ref[...] = jnp.concatenate(p_rows, axis=0).astype(jnp.int32)
    carry[...] = carry[...] + jnp.sum(sel, axis=1, keepdims=True)
    cnt_ref[...] = carry[...]


def router(logits_t, router_b):
    ne, n = logits_t.shape
    return pl.pallas_call(
        _router_kernel,
        grid=(n // TM,),
        in_specs=[pl.BlockSpec((ne, TM), lambda i: (0, i)), pl.BlockSpec((ne, 1), lambda i: (0, 0))],
        out_specs=[pl.BlockSpec((TOP_K, TM), lambda i: (0, i)), pl.BlockSpec((TOP_K, TM), lambda i: (0, i)),
                   pl.BlockSpec((TOP_K, TM), lambda i: (0, i)), pl.BlockSpec((ne, 128), lambda i: (0, 0))],
        out_shape=[jax.ShapeDtypeStruct((TOP_K, n), jnp.int32), jax.ShapeDtypeStruct((TOP_K, n), F32),
                   jax.ShapeDtypeStruct((TOP_K, n), jnp.int32), jax.ShapeDtypeStruct((ne, 128), F32)],
        scratch_shapes=[pltpu.VMEM((ne, 128), F32)],
        compiler_params=_cparams(("arbitrary",)),
        name="router",
    )(logits_t, router_b.reshape(ne, 1))


def _dispatch_kernel(dest_ref, h_ref, init_ref, rows_ref, sem):
    del init_ref
    n_pairs = dest_ref.shape[1]

    def copy(j):
        src = pl.multiple_of(lax.shift_right_logical(j, 3) * ROW_TILES, ROW_TILES)
        dst = pl.multiple_of(dest_ref[0, j] * ROW_TILES, ROW_TILES)
        return pltpu.make_async_copy(h_ref.at[pl.ds(src, ROW_TILES)], rows_ref.at[pl.ds(dst, ROW_TILES)], sem)

    def start(jj, carry):
        for u in range(TOP_K):
            copy(jj * TOP_K + u).start(priority=u % 2)
        return carry

    def wait(j, carry):
        copy(j).wait()
        return carry

    lax.fori_loop(0, n_pairs // TOP_K, start, 0)
    lax.fori_loop(0, n_pairs, wait, 0, unroll=8)


def dispatch(h2t, dest, n_rows):
    n = h2t.shape[0] // ROW_TILES
    n_tiles = n // TM
    rows0 = jnp.zeros((n_rows * ROW_TILES, 128), F32)
    return pl.pallas_call(
        _dispatch_kernel,
        grid=(n_tiles,),
        in_specs=[pl.BlockSpec((None, 1, TM * TOP_K), lambda i: (i, 0, 0), memory_space=pltpu.SMEM),
                  pl.BlockSpec((TM * ROW_TILES, 128), lambda i: (i, 0)),
                  pl.BlockSpec(memory_space=pl.ANY)],
        out_specs=pl.BlockSpec(memory_space=pl.ANY),
        out_shape=jax.ShapeDtypeStruct((n_rows * ROW_TILES, 128), F32),
        scratch_shapes=[pltpu.SemaphoreType.DMA(())],
        input_output_aliases={2: 0},
        compiler_params=_cparams(("arbitrary",)),
        name="dispatch",
    )(dest.reshape(n_tiles, 1, TM * TOP_K), h2t, rows0)


def _expert_kernel(be_ref, nu_ref, x_ref, wg_ref, wu_ref, wd_ref, y_ref):
    @pl.when(pl.program_id(0) < nu_ref[0])
    def _():
        de = wg_ref.shape[1]
        g = jnp.zeros((EXPERT_ROWS, de), F32)
        u = jnp.zeros((EXPERT_ROWS, de), F32)
        for s in range(0, ROW_TILES, 2):
            xs = jnp.concatenate([x_ref[pl.ds(s, EXPERT_ROWS, stride=ROW_TILES), :],
                                  x_ref[pl.ds(s + 1, EXPERT_ROWS, stride=ROW_TILES), :]], axis=1).astype(BF16)
            g = g + jnp.dot(xs, wg_ref[128 * s:128 * (s + 2), :].astype(BF16), preferred_element_type=F32)
            u = u + jnp.dot(xs, wu_ref[128 * s:128 * (s + 2), :].astype(BF16), preferred_element_type=F32)
        y = _dot(_silu(g) * u, wd_ref[...])
        for s in range(ROW_TILES):
            y_ref[pl.ds(s, EXPERT_ROWS, stride=ROW_TILES), :] = y[:, 128 * s:128 * (s + 1)]

    @pl.when(pl.program_id(0) >= nu_ref[0])
    def _():
        y_ref[...] = jnp.zeros(y_ref.shape, F32)


def experts(x_rows, blk_e, n_used, wg, wu, wd, layer):
    n_rows = x_rows.shape[0] // ROW_TILES
    n_blk = n_rows // EXPERT_ROWS
    ne, d, de = wg.shape[1:]
    wg2 = wg.reshape(-1, d, de)
    wu2 = wu.reshape(-1, d, de)
    wd2 = wd.reshape(-1, de, d)
    gs = pltpu.PrefetchScalarGridSpec(
        num_scalar_prefetch=2,
        grid=(n_blk,),
        in_specs=[pl.BlockSpec((EXPERT_ROWS * ROW_TILES, 128), lambda i, be, nu: (i, 0)),
                  pl.BlockSpec((None, d, de), lambda i, be, nu: (layer * ne + be[i], 0, 0)),
                  pl.BlockSpec((None, d, de), lambda i, be, nu: (layer * ne + be[i], 0, 0)),
                  pl.BlockSpec((None, de, d), lambda i, be, nu: (layer * ne + be[i], 0, 0))],
        out_specs=pl.BlockSpec((EXPERT_ROWS * ROW_TILES, 128), lambda i, be, nu: (i, 0)),
    )
    return pl.pallas_call(
        _expert_kernel,
        grid_spec=gs,
        out_shape=jax.ShapeDtypeStruct((n_rows * ROW_TILES, 128), F32),
        compiler_params=_cparams(("arbitrary",)),
        name="experts",
    )(blk_e, n_used, x_rows, wg2, wu2, wd2)


def _combine_kernel(dest_ref, w8_ref, x1_ref, h2_ref, g2_ref, nw_ref, wsg_ref, wsu_ref, wsd_ref, yrows_ref,
                    out_ref, buf, sem):
    n_pairs = dest_ref.shape[1]

    def copy(j):
        src = pl.multiple_of(dest_ref[0, j] * ROW_TILES, ROW_TILES)
        dst = pl.multiple_of(lax.shift_right_logical(j, 3) * ROW_TILES, ROW_TILES)
        k = jnp.bitwise_and(j, TOP_K - 1)
        return pltpu.make_async_copy(yrows_ref.at[pl.ds(src, ROW_TILES)], buf.at[k, pl.ds(dst, ROW_TILES)], sem)

    def start(jj, carry):
        for u in range(TOP_K):
            copy(jj * TOP_K + u).start(priority=u % 2)
        return carry

    def wait(j, carry):
        copy(j).wait()
        return carry

    lax.fori_loop(0, n_pairs // TOP_K, start, 0)
    h2 = h2_ref[...]
    shared = _dot(_silu(_dot(h2, wsg_ref[...])) * _dot(h2, wsu_ref[...]), wsd_ref[...])
    lax.fori_loop(0, n_pairs, wait, 0, unroll=8)
    w8 = w8_ref[...]
    pieces = []
    for s in range(ROW_TILES):
        acc = w8[:, 0:1] * buf[0, pl.ds(s, TM, stride=ROW_TILES), :]
        for k in range(1, TOP_K):
            acc = acc + w8[:, k:k + 1] * buf[k, pl.ds(s, TM, stride=ROW_TILES), :]
        pieces.append(acc)
    routed = jnp.concatenate(pieces, axis=1)
    out_ref[...] = x1_ref[...] + g2_ref[...] * _rms(routed + shared, nw_ref[...])


def combine(dest, w8, x1, h2, modx, nw3, wsg, wsu, wsd, y_rows, group_of_tile):
    n, d = x1.shape
    n_tiles = n // TM
    tok = lambda w: pl.BlockSpec((TM, w), lambda i: (i, 0))
    full = lambda a: pl.BlockSpec(a.shape, lambda i: (0,) * a.ndim)
    return pl.pallas_call(
        _combine_kernel,
        grid=(n_tiles,),
        in_specs=[pl.BlockSpec((None, 1, TM * TOP_K), lambda i: (i, 0, 0), memory_space=pltpu.SMEM),
                  tok(TOP_K), tok(d), tok(d), _mod_spec(5, group_of_tile),
                  full(nw3), full(wsg), full(wsu), full(wsd),
                  pl.BlockSpec(memory_space=pl.ANY)],
        out_specs=tok(d),
        out_shape=jax.ShapeDtypeStruct((n, d), F32),
        scratch_shapes=[pltpu.VMEM((TOP_K, TM * ROW_TILES, 128), F32), pltpu.SemaphoreType.DMA(())],
        compiler_params=_cparams(("arbitrary",)),
        name="combine",
    )(dest.reshape(n_tiles, 1, TM * TOP_K), w8, x1, h2, modx, nw3, wsg, wsu, wsd, y_rows)


def _pack_w_in(w):
    d = w.shape[0]
    z = lambda n: jnp.zeros((d, n), w.dtype)
    parts = [w[:, 0:1408], w[:, 1408:1420], z(116), w[:, 1420:1932], w[:, 1932:2956], w[:, 2956:2964], z(120),
             w[:, 2964:6036]]
    return jnp.concatenate(parts, axis=1).astype(BF16)


def _expand_w1(w1):
    half = NSA_CMP_STRIDE * HEAD_DIM
    w = jnp.zeros((NSA_CMP_STRIDE, 4, HEAD_DIM, 256), w1.dtype)
    for kv in range(2):
        w = w.at[:, kv, :, 128 * kv:128 * kv + 64].set(w1[kv, :half].reshape(NSA_CMP_STRIDE, HEAD_DIM, HEAD_DIM))
        w = w.at[:, kv, :, 128 * kv + 64:128 * kv + 128].set(w1[kv, half:].reshape(NSA_CMP_STRIDE, HEAD_DIM, HEAD_DIM))
    return w.reshape(NSA_CMP_STRIDE * 256, 256).astype(BF16)


def _cmp_w1_rows(w1):
    half = NSA_CMP_STRIDE * HEAD_DIM
    w = jnp.zeros((NSA_CMP_STRIDE, 2, HEAD_DIM, 256), w1.dtype)
    for kv in range(2):
        w = w.at[:, kv, :, 128 * kv:128 * kv + 64].set(w1[kv, :half].reshape(NSA_CMP_STRIDE, HEAD_DIM, HEAD_DIM))
        w = w.at[:, kv, :, 128 * kv + 64:128 * kv + 128].set(w1[kv, half:].reshape(NSA_CMP_STRIDE, HEAD_DIM, HEAD_DIM))
    return w.reshape(NSA_CMP_STRIDE * 128, 256).astype(BF16)


PROMPT_SLOTS = 8
SAMPLE_SLOTS = 16


def kernel(x_prompt, x_sample, c_prompt, c_sample, cache_moba_kv, cache_nsa_kv, cache_nsa_win, state_ssd_conv, state_ssd, page_table, rel_bias, ada_w, ada_b, norm_w, w_in, nsa_cmp_w1, nsa_cmp_b1, nsa_cmp_w2, nsa_cmp_b2, nsa_cmp_pos, ssd_conv_w, ssd_conv_b, ssd_dt_bias, ssd_a_log, ssd_d, ssd_norm_w, w_branch_moba, w_branch_nsa, w_branch_ssd, w_out, router_w, router_b, exp_w_gate, exp_w_up, exp_w_down, shared_w_gate, shared_w_up, shared_w_down):
    depth = w_in.shape[0]
    bp, lp, d = x_prompt.shape
    bs, ls, _ = x_sample.shape
    n_p = bp * lp
    n_s = bs * ls
    n = n_p + n_s
    assert n_s == TM and lp % TM == 0 and lp % SSD_CHUNK == 0
    n_pages = page_table.shape[1]
    past = n_pages * PAGE
    n_pool = cache_moba_kv.shape[1]
    tiles_p = lp // TM
    n_ptiles = n_p // TM

    def group_of_tile(i):
        return jnp.where(i < n_ptiles, i // tiles_p, bp)

    x = jnp.concatenate([x_prompt.reshape(n_p, d), x_sample.reshape(n_s, d)], axis=0)
    c_all = jnp.concatenate([c_prompt, c_sample, jnp.zeros((4, d), F32)], axis=0)
    kv_cache = jnp.transpose(cache_moba_kv, (0, 1, 3, 4, 5, 2))
    ns_cache = jnp.transpose(cache_nsa_kv, (0, 1, 3, 4, 2))
    pt_flat = page_table.reshape(-1).astype(jnp.int32)
    prompt_pages = jnp.arange(bp * (lp // PAGE), dtype=jnp.int32)

    q_chunk = 128
    tiles_p_bias = bias_tiles(rel_bias, q_chunk)
    tiles_s_bias = bias_tiles(rel_bias, ls)

    n_blk = -(-(n * TOP_K + N_EXPERTS * (EXPERT_ROWS - 1)) // EXPERT_ROWS)
    n_rows = n_blk * EXPERT_ROWS

    sample_pad = 128
    outs = [[] for _ in range(10)]
    for l in range(depth):
        mod = modulation(c_all, ada_w, ada_b, l)
        modx = jnp.concatenate([jnp.broadcast_to(mod[:bp, None, :], (bp, TM, 6 * d)),
                                jnp.repeat(mod[bp:bp + bs], ls, axis=0)[None]], axis=0)
        nw = norm_w[l]
        qm, kvm, qn, nsa, win, gn, z, xbc, dt, gbr = proj_in(x, modx, nw[0:1], _pack_w_in(w_in[l]), group_of_tile)
        w_exp = _expand_w1(nsa_cmp_w1[l])

        kvm_pages = kvm.reshape(n // PAGE, PAGE, 512)
        nsa_pages = nsa.reshape(n // PAGE, PAGE, 256)
        win_pages = win.reshape(n // PAGE, PAGE, 128)
        ks_p, pab_p = page_ctx(kvm_pages, nsa_pages, w_exp, 0, n_p // PAGE)
        km_p, cmp_p = ctx_final(ks_p.reshape(bp, lp // PAGE, 256), pab_p.reshape(bp, lp // NSA_CMP_STRIDE, 256),
                                nsa_cmp_pos, nsa_cmp_w1, nsa_cmp_b1, nsa_cmp_w2, nsa_cmp_b2, l, lp // PAGE)
        om_p, on_p = attention_slots(rel_bias, qm, qn, gn, 0, kvm_pages, nsa_pages, prompt_pages, None, None,
                                     win_pages, 0, km_p, cmp_p, tiles_p_bias,
                                     n_seq=bp, q_rows=q_chunk, n_chunks=lp // q_chunk, q0_base=0,
                                     tiles_per_seq=lp // PAGE, n_win_tiles=lp // PAGE,
                                     n_sel_blocks=lp // NSA_SEL_BLOCK, n_slots=PROMPT_SLOTS)

        ks_c, pab_c = page_ctx_cache(kv_cache, ns_cache, _cmp_w1_rows(nsa_cmp_w1[l]), l)
        ks_s, pab_s = ctx_gather(pt_flat, ks_c, pab_c, bs, n_pages)
        km_s, cmp_s = ctx_final(ks_s, pab_s, nsa_cmp_pos, nsa_cmp_w1, nsa_cmp_b1, nsa_cmp_w2, nsa_cmp_b2, l, n_pages)
        pad_rows = lambda a: jnp.pad(a[n_p:].reshape(bs, ls, -1), ((0, 0), (0, PAGE - ls), (0, 0)))
        tail_kv = jnp.transpose(pad_rows(kvm).reshape(bs, PAGE, 2, MOBA_HEADS, HEAD_DIM), (0, 2, 3, 4, 1))
        tail_ns = jnp.transpose(pad_rows(nsa).reshape(bs, PAGE, 4, HEAD_DIM)[:, :, 2:], (0, 2, 3, 1))
        win_s = jnp.concatenate([cache_nsa_win[l].reshape(bs, -1, 128), pad_rows(win)], axis=1)
        win_s = win_s.reshape(bs * (win_s.shape[1] // PAGE), PAGE, 128)
        lf_s = -(-(past + ls) // MOBA_BLOCK) * MOBA_BLOCK
        om_s, on_s = attention_slots(rel_bias, qm, qn, gn, n_p // ls, kv_cache, ns_cache, pt_flat,
                                     tail_kv, tail_ns, win_s, (past - NSA_WINDOW) // PAGE, km_s, cmp_s, tiles_s_bias,
                                     n_seq=bs, q_rows=ls, n_chunks=1, q0_base=past,
                                     tiles_per_seq=n_pages, n_win_tiles=win_s.shape[0] // bs,
                                     n_sel_blocks=lf_s // NSA_SEL_BLOCK, n_slots=SAMPLE_SLOTS, cache_layer=l)
        om = jnp.concatenate([om_p, om_s], axis=0)
        on = jnp.concatenate([on_p, on_s], axis=0)

        dt_t = dt[:, :SSD_HEADS].T
        ssd_par = (ssd_conv_w[l], ssd_conv_b[l], ssd_dt_bias[l], ssd_a_log[l], ssd_d[l], ssd_norm_w[l])
        y_p, st_p = ssd(xbc, z, dt, dt_t[:, :n_p].reshape(SSD_HEADS, bp, lp).transpose(1, 0, 2),
                        jnp.zeros((bp, 8, SSD_CONV_DIM), F32),
                        jnp.zeros((bp, SSD_HEADS, SSD_HEAD_DIM, SSD_STATE), F32), *ssd_par,
                        n_seq=bp, seq_rows=lp, chunk=SSD_CHUNK, valid_len=lp)
        pad_s = lambda a: jnp.pad(a[n_p:].reshape(bs, ls, -1), ((0, 0), (0, sample_pad - ls), (0, 0))).reshape(bs * sample_pad, -1)
        dt_t_s = jnp.pad(dt_t[:, n_p:].reshape(SSD_HEADS, bs, ls).transpose(1, 0, 2), ((0, 0), (0, 0), (0, sample_pad - ls)))
        conv_prev = jnp.pad(state_ssd_conv[l], ((0, 0), (8 - (SSD_CONV - 1), 0), (0, 0)))
        y_s, st_s = ssd(pad_s(xbc), pad_s(z), pad_s(dt), dt_t_s, conv_prev, state_ssd[l], *ssd_par,
                        n_seq=bs, seq_rows=sample_pad, chunk=sample_pad, valid_len=ls)
        ys = jnp.concatenate([y_p[:n_p], y_s.reshape(bs, sample_pad, -1)[:, :ls].reshape(n_s, -1)], axis=0)

        x1, h2, h2t, logits_t = mix_out(om, on, ys, gbr, x, modx, nw[1:2], nw[2:3],
                                        w_branch_moba[l].astype(BF16), w_branch_nsa[l].astype(BF16),
                                        w_branch_ssd[l].astype(BF16), w_out[l].astype(BF16), router_w[l].T,
                                        group_of_tile)
        eidx, w8, pos8, cnt = router(logits_t, router_b[l])
        cnt = cnt[:, 0].astype(jnp.int32)
        padded = (cnt + EXPERT_ROWS - 1) // EXPERT_ROWS * EXPERT_ROWS
        ends = jnp.cumsum(padded)
        off = ends - padded
        e_ids = jnp.arange(N_EXPERTS, dtype=jnp.int32)
        off_of = jnp.sum(jnp.where(eidx[:, :, None] == e_ids, off, 0), axis=-1)
        dest = (off_of + pos8).T.reshape(-1)
        blk_start = jnp.arange(n_blk, dtype=jnp.int32) * EXPERT_ROWS
        blk_e = jnp.minimum(jnp.sum((ends[None, :] <= blk_start[:, None]).astype(jnp.int32), axis=1), N_EXPERTS - 1)
        n_used = (ends[-1] // EXPERT_ROWS).astype(jnp.int32).reshape(1)
        x_rows = dispatch(h2t, dest, n_rows)
        y_rows = experts(x_rows, blk_e, n_used, exp_w_gate, exp_w_up, exp_w_down, l)
        x = combine(dest, w8.T, x1, h2, modx, nw[3:4], shared_w_gate[l].astype(BF16), shared_w_up[l].astype(BF16),
                    shared_w_down[l].astype(BF16), y_rows, group_of_tile)

        keep = min(NSA_WINDOW, lp)
        outs[0].append(kvm[:n_p].reshape(bp, lp, 2, MOBA_HEADS, HEAD_DIM))
        outs[1].append(kvm[n_p:].reshape(bs, ls, 2, MOBA_HEADS, HEAD_DIM))
        outs[2].append(nsa[:n_p].reshape(bp, lp, 4, HEAD_DIM))
        outs[3].append(nsa[n_p:].reshape(bs, ls, 4, HEAD_DIM))
        outs[4].append(win[:n_p].reshape(bp, lp, 2, HEAD_DIM)[:, lp - keep:])
        outs[5].append(win[n_p:].reshape(bs, ls, 2, HEAD_DIM))
        outs[6].append(xbc[:n_p].reshape(bp, lp, -1)[:, lp - (SSD_CONV - 1):])
        outs[7].append(xbc[n_p:].reshape(bs, ls, -1)[:, ls - (SSD_CONV - 1):])
        outs[8].append(st_p)
        outs[9].append(st_s)

    y_prompt = x[:n_p].reshape(bp, lp, d)
    y_sample = x[n_p:].reshape(bs, ls, d)
    return (y_prompt, y_sample) + tuple(jnp.stack(o) for o in outs)
```
